```python
import math
import jax, jax.numpy as jnp
from jax import lax
import numpy as np

D_MODEL = 1024
BATCH = 8
SEQ = 8192
DEPTH = 1

N_META = 16
BLOCK = 128
PAD_FRONT = BLOCK - N_META
WINDOW = 128
A_HEADS = 4
A_QK_DIM = 64
A_V_DIM = 2 * A_QK_DIM
A_WIDTH = A_HEADS * A_V_DIM
B_HEADS = 8
B_KV_HEADS = 2
B_GROUP = B_HEADS // B_KV_HEADS
B_HEAD_DIM = 64
B_WIDTH = B_HEADS * B_HEAD_DIM
B_KV_WIDTH = B_KV_HEADS * B_HEAD_DIM
N_BUCKETS = 32
MAX_DISTANCE = 128
N_BIAS_HEADS = A_HEADS + B_HEADS
EPS = 1e-6
NEG = -1e30
COLS = [A_HEADS * 2 * A_QK_DIM, A_HEADS * 2 * A_QK_DIM, A_WIDTH, A_WIDTH,
        B_WIDTH, B_KV_WIDTH, B_KV_WIDTH, B_WIDTH, D_MODEL, D_MODEL]
SPLITS = [int(v) for v in np.cumsum(COLS)[:-1]]
D_IN = int(sum(COLS))

kernel_name = "hybrid_diffattn_swa_gated_encoder"


def rmsnorm(x, g):
    xf = x.astype(jnp.float32)
    y = xf * lax.rsqrt(jnp.mean(xf * xf, axis=-1, keepdims=True) + EPS)
    return (y * g.astype(jnp.float32)).astype(x.dtype)


def t5_bucket(rel):
    half = N_BUCKETS // 2
    max_exact = half // 2
    ret = jnp.where(rel > 0, half, 0)
    n = jnp.abs(rel)
    nf = jnp.maximum(n, 1).astype(jnp.float32)
    large = max_exact + (jnp.log(nf / max_exact) / math.log(MAX_DISTANCE / max_exact)
                         * (half - max_exact)).astype(jnp.int32)
    large = jnp.minimum(large, half - 1)
    return ret + jnp.where(n < max_exact, n, large)


def diff_attention(q, k, v, pos, valid, bias_tab, lam, lam_init, subln_g):
    B, P, H, _, dk = q.shape
    dv = v.shape[-1]
    nb = P // BLOCK
    qb = q.reshape(B, nb, BLOCK, H, 2, dk).transpose(1, 0, 2, 3, 4, 5)
    posb = pos.reshape(nb, BLOCK)
    scale = dk ** -0.5

    def block(args):
        q_blk, qpos = args
        s = jnp.einsum('bqhmd,bkhmd->bhmqk', q_blk, k).astype(jnp.float32) * scale
        bias = bias_tab[t5_bucket(pos[None, :] - qpos[:, None])]
        s = s + bias.transpose(2, 0, 1)[None, :, None].astype(jnp.float32)
        s = jnp.where(valid[None, None, None, None, :], s, NEG)
        p = jax.nn.softmax(s, axis=-1)
        a = p[:, :, 0] - lam * p[:, :, 1]
        return jnp.einsum('bhqk,bkhe->bqhe', a.astype(v.dtype), v)

    o = lax.map(block, (qb, posb))
    o = o.transpose(1, 0, 2, 3, 4).reshape(B, P, H, dv)
    o = rmsnorm(o, subln_g) * (1.0 - lam_init)
    return o.reshape(B, P, H * dv)


def windowed_gqa(q, k, v, valid, bias_tab, sink):
    B, P, Hkv, G, d = q.shape
    nb = P // BLOCK
    qb = q.reshape(B, nb, BLOCK, Hkv, G, d).transpose(1, 0, 2, 3, 4, 5)

    def band(t):
        t = jnp.pad(t, ((0, 0), (BLOCK, BLOCK), (0, 0), (0, 0)))
        t = t.reshape(B, nb + 2, BLOCK, Hkv, d)
        t = jnp.concatenate([t[:, :-2], t[:, 1:-1], t[:, 2:]], axis=2)
        return t.transpose(1, 0, 2, 3, 4)

    kb, vb = band(k), band(v)
    vext = jnp.pad(valid, (BLOCK, BLOCK)).reshape(nb + 2, BLOCK)
    kvalid = jnp.concatenate([vext[:-2], vext[1:-1], vext[2:]], axis=1)
    rel = jnp.arange(3 * BLOCK)[None, :] - BLOCK - jnp.arange(BLOCK)[:, None]
    inwin = jnp.abs(rel) <= WINDOW
    bias = bias_tab[t5_bucket(rel)].transpose(2, 0, 1).reshape(Hkv, G, BLOCK, 3 * BLOCK).astype(jnp.float32)
    sink_col = sink.reshape(Hkv, G, 1, 1).astype(jnp.float32)
    scale = d ** -0.5

    def block(args):
        q_blk, k_blk, v_blk, kv_ok = args
        s = jnp.einsum('bqgrd,bkgd->bgrqk', q_blk, k_blk).astype(jnp.float32) * scale + bias
        s = jnp.where(inwin & kv_ok[None, :], s, NEG)
        sk = jnp.broadcast_to(sink_col, (B, Hkv, G, BLOCK, 1))
        p = jax.nn.softmax(jnp.concatenate([s, sk], axis=-1), axis=-1)[..., :-1]
        return jnp.einsum('bgrqk,bkgd->bqgrd', p.astype(v_blk.dtype), v_blk)

    o = lax.map(block, (qb, kb, vb, kvalid))
    return o.transpose(1, 0, 2, 3, 4, 5).reshape(B, P, Hkv * G * d)


def setup_inputs(seed: int = 0) -> dict:
    key = jax.random.key(seed)
    ks = jax.random.split(key, 20)
    f32 = jnp.float32
    nrm = lambda k, shape, s: jax.random.normal(k, shape, f32) * s
    return {
        "x": nrm(ks[0], (BATCH, SEQ, D_MODEL), 1.0),
        "meta_tokens": nrm(ks[1], (N_META, D_MODEL), 1.0),
        "rel_bias": nrm(ks[2], (N_BUCKETS, N_BIAS_HEADS), 0.5),
        "pre_norm_g": 1.0 + nrm(ks[3], (DEPTH, D_MODEL), 0.01),
        "w_in": nrm(ks[4], (DEPTH, D_MODEL, D_IN), D_MODEL ** -0.5),
        "b_in": nrm(ks[5], (DEPTH, D_IN), 0.01),
        "lambda_q1": nrm(ks[6], (DEPTH, A_QK_DIM), 0.1),
        "lambda_k1": nrm(ks[7], (DEPTH, A_QK_DIM), 0.1),
        "lambda_q2": nrm(ks[8], (DEPTH, A_QK_DIM), 0.1),
        "lambda_k2": nrm(ks[9], (DEPTH, A_QK_DIM), 0.1),
        "subln_g": 1.0 + nrm(ks[10], (DEPTH, A_V_DIM), 0.01),
        "sink": nrm(ks[11], (DEPTH, B_HEADS), 0.5),
        "w_out_a": nrm(ks[12], (DEPTH, A_WIDTH, D_MODEL), A_WIDTH ** -0.5),
        "w_out_b": nrm(ks[13], (DEPTH, B_WIDTH, D_MODEL), B_WIDTH ** -0.5),
        "w_out": nrm(ks[14], (DEPTH, D_MODEL, D_MODEL), D_MODEL ** -0.5),
        "post_norm_g": 1.0 + nrm(ks[15], (DEPTH, D_MODEL), 0.01),
    }


def reference(x, meta_tokens, rel_bias, pre_norm_g, w_in, b_in, lambda_q1, lambda_k1,
              lambda_q2, lambda_k2, subln_g, sink, w_out_a, w_out_b, w_out, post_norm_g):
    B = x.shape[0]
    meta = jnp.broadcast_to(meta_tokens[None].astype(x.dtype), (B, N_META, D_MODEL))
    h = jnp.concatenate([meta, x], axis=1)
    L = h.shape[1]
    P = L + PAD_FRONT
    pos = jnp.arange(P, dtype=jnp.int32) - PAD_FRONT
    valid = pos >= 0
    bias_a = rel_bias[:, :A_HEADS]
    bias_b = rel_bias[:, A_HEADS:]

    def padseq(t):
        return jnp.pad(t, ((0, 0), (PAD_FRONT, 0), (0, 0)))

    for layer in range(DEPTH):
        hn = rmsnorm(h, pre_norm_g[layer])
        proj = hn @ w_in[layer] + b_in[layer]
        qa, ka, va, za, qb, kb, vb, zb, ga, gb = jnp.split(proj, SPLITS, axis=-1)

        lam_init = 0.8 - 0.6 * math.exp(-0.3 * layer)
        lam = (jnp.exp(jnp.sum(lambda_q1[layer].astype(jnp.float32) * lambda_k1[layer].astype(jnp.float32)))
               - jnp.exp(jnp.sum(lambda_q2[layer].astype(jnp.float32) * lambda_k2[layer].astype(jnp.float32)))
               + lam_init)
        oa = diff_attention(
            padseq(qa).reshape(B, P, A_HEADS, 2, A_QK_DIM),
            padseq(ka).reshape(B, P, A_HEADS, 2, A_QK_DIM),
            padseq(va).reshape(B, P, A_HEADS, A_V_DIM),
            pos, valid, bias_a, lam, lam_init, subln_g[layer])[:, PAD_FRONT:]
        ob = windowed_gqa(
            padseq(qb).reshape(B, P, B_KV_HEADS, B_GROUP, B_HEAD_DIM),
            padseq(kb).reshape(B, P, B_KV_HEADS, B_HEAD_DIM),
            padseq(vb).reshape(B, P, B_KV_HEADS, B_HEAD_DIM),
            valid, bias_b, sink[layer])[:, PAD_FRONT:]

        ya = (oa * jax.nn.silu(za)) @ w_out_a[layer]
        yb = (ob * jax.nn.silu(zb)) @ w_out_b[layer]
        mixed = jax.nn.sigmoid(ga) * ya + jax.nn.sigmoid(gb) * yb
        h = h + rmsnorm(mixed @ w_out[layer], post_norm_g[layer])
    return h[:, N_META:]
```

```python
import functools
import math

import jax
import jax.numpy as jnp
from jax import lax
from jax.experimental import pallas as pl
from jax.experimental.pallas import tpu as pltpu

N_META = 16
WINDOW = 128
A_HEADS = 4
A_QK_DIM = 64
A_V_DIM = 2 * A_QK_DIM
A_WIDTH = A_HEADS * A_V_DIM
B_HEADS = 8
B_KV_HEADS = 2
B_HEAD_DIM = 64
B_WIDTH = B_HEADS * B_HEAD_DIM
B_KV_WIDTH = B_KV_HEADS * B_HEAD_DIM
N_BUCKETS = 32
MAX_DISTANCE = 128
EPS = 1e-6
NEG = -1e30
LOG2E = math.log2(math.e)
LAM_INIT = 0.8 - 0.6 * math.exp(-0.3 * 0)

LANES = 128
VMEM_LIMIT_BYTES = 56 * 1024 * 1024

_COLS = (A_WIDTH, A_WIDTH, A_WIDTH, A_WIDTH, B_WIDTH, B_KV_WIDTH, B_KV_WIDTH, B_WIDTH)
BF16 = jnp.bfloat16
F32 = jnp.float32


def _t5_bucket(rel):
    half = N_BUCKETS // 2
    max_exact = half // 2
    ret = jnp.where(rel > 0, half, 0)
    n = jnp.abs(rel)
    nf = jnp.maximum(n, 1).astype(F32)
    large = max_exact + (jnp.log(nf / max_exact) / math.log(MAX_DISTANCE / max_exact)
                         * (half - max_exact)).astype(jnp.int32)
    large = jnp.minimum(large, half - 1)
    return ret + jnp.where(n < max_exact, n, large)


def _proj_kernel(x_ref, g_ref, w_ref, b_ref,
                 qa_ref, ka_ref, vat_ref, za_ref, qb_ref, kb_ref, vb_ref, zb_ref, ga_ref, gb_ref,
                 *, d_model, tkv):
    x = x_ref[...]
    hn = (x * lax.rsqrt(jnp.mean(x * x, axis=-1, keepdims=True) + EPS) * g_ref[...]).astype(BF16)

    def seg(c0, n):
        return jnp.dot(hn, w_ref[:, c0:c0 + n], preferred_element_type=F32) + b_ref[:, c0:c0 + n]

    qscale = (A_QK_DIM ** -0.5) * LOG2E
    c = 0
    qa_ref[...] = (seg(c, A_WIDTH) * qscale).astype(BF16); c += A_WIDTH
    ka_ref[...] = seg(c, A_WIDTH).astype(BF16); c += A_WIDTH
    va = seg(c, A_WIDTH); c += A_WIDTH
    for t in range(vat_ref.shape[0]):
        vat_ref[t] = va[t * tkv:(t + 1) * tkv, :].T.astype(BF16)
    za_ref[...] = seg(c, A_WIDTH).astype(BF16); c += A_WIDTH
    qb_ref[...] = (seg(c, B_WIDTH) * qscale).astype(BF16); c += B_WIDTH
    kb_ref[...] = seg(c, B_KV_WIDTH).astype(BF16); c += B_KV_WIDTH
    vb_ref[...] = seg(c, B_KV_WIDTH).astype(BF16); c += B_KV_WIDTH
    zb_ref[...] = seg(c, B_WIDTH).astype(BF16); c += B_WIDTH
    ga_ref[...] = seg(c, d_model).astype(BF16); c += d_model
    gb_ref[...] = seg(c, d_model).astype(BF16)


def _project(x3, g, w, b, *, tm, tkv):
    nb, s, d = x3.shape
    d_in = w.shape[1]
    widths = _COLS + (d, d)
    nt = s // tm
    row = lambda n: pl.BlockSpec((None, tm, n), lambda bi, i: (bi, i, 0))
    const = lambda shape: pl.BlockSpec(shape, lambda bi, i: (0, 0), pipeline_mode=pl.Buffered(1))
    out_specs, out_shapes = [], []
    for idx, n in enumerate(widths):
        if idx == 2:
            out_specs.append(pl.BlockSpec((None, tm // tkv, n, tkv), lambda bi, i: (bi, i, 0, 0)))
            out_shapes.append(jax.ShapeDtypeStruct((nb, s // tkv, n, tkv), BF16))
        else:
            out_specs.append(row(n))
            out_shapes.append(jax.ShapeDtypeStruct((nb, s, n), BF16))
    return pl.pallas_call(
        functools.partial(_proj_kernel, d_model=d, tkv=tkv),
        grid=(nb, nt),
        in_specs=[row(d), const((1, d)), const((d, d_in)), const((1, d_in))],
        out_specs=out_specs,
        out_shape=out_shapes,
        compiler_params=pltpu.CompilerParams(
            dimension_semantics=("arbitrary", "arbitrary"), vmem_limit_bytes=VMEM_LIMIT_BYTES),
        name="proj",
    )(x3, g, w, b)


def _bias_kernel(tab_ref, idx_ref, o_ref, *, head0):
    h = pl.program_id(0) + head0
    idx = idx_ref[...]
    acc = jnp.full(idx.shape, NEG, F32)
    for bkt in range(N_BUCKETS):
        acc = jnp.where(idx == bkt, tab_ref[bkt, h] * LOG2E, acc)
    o_ref[...] = acc


def _bias_tiles(tab, idx, *, head0, n_heads, tr):
    r, c = idx.shape
    return pl.pallas_call(
        functools.partial(_bias_kernel, head0=head0),
        grid=(n_heads, r // tr),
        in_specs=[pl.BlockSpec(memory_space=pltpu.SMEM),
                  pl.BlockSpec((tr, c), lambda h, i: (i, 0))],
        out_specs=pl.BlockSpec((None, tr, c), lambda h, i: (h, i, 0)),
        out_shape=jax.ShapeDtypeStruct((n_heads, r, c), F32),
        compiler_params=pltpu.CompilerParams(dimension_semantics=("arbitrary", "arbitrary")),
        name="bias_tiles",
    )(tab, idx)


def _attn_a_kernel(lam_ref, q_ref, k_ref, vt_ref, km_ref, vmt_ref, bias_ref, bm_ref, z_ref, g_ref,
                   o_ref, qt_ref, m_ref, l_ref, acc_ref, *, tq, tk, nk, ratio, nv):
    i = pl.program_id(2)
    q = q_ref[...].astype(F32)
    lane = lax.broadcasted_iota(jnp.int32, q.shape, 1)
    qt_ref[:, :tq] = jnp.where(lane < A_QK_DIM, q, 0.0).T.astype(BF16)
    qt_ref[:, tq:] = jnp.where(lane >= A_QK_DIM, q, 0.0).T.astype(BF16)
    qt = qt_ref[...]

    def both(fn):
        return jnp.concatenate([fn(slice(0, tq)), fn(slice(tq, 2 * tq))], axis=1)

    bm = bm_ref[...]
    s = jnp.dot(km_ref[...], qt, preferred_element_type=F32) + jnp.concatenate([bm, bm], axis=1)
    m0 = jnp.max(s, axis=0, keepdims=True)
    p = jnp.exp2(s - m0)
    pb = p.astype(BF16)
    vmt = vmt_ref[...]
    m_ref[...] = m0
    l_ref[...] = jnp.sum(p, axis=0, keepdims=True)
    acc_ref[...] = both(lambda c: jnp.dot(vmt, pb[:, c], preferred_element_type=F32))

    def body(j, carry):
        ks = pl.multiple_of(j * tk, tk)
        s = jnp.dot(k_ref[pl.ds(ks, tk), :], qt, preferred_element_type=F32)
        bt = bias_ref[jnp.clip(j - i * ratio + 2, 0, nv - 1)]
        s = s + jnp.concatenate([bt, bt], axis=1)
        m_old = m_ref[...]
        m_new = jnp.maximum(m_old, jnp.max(s, axis=0, keepdims=True))
        alpha = jnp.exp2(m_old - m_new)
        p = jnp.exp2(s - m_new)
        l_ref[...] = alpha * l_ref[...] + jnp.sum(p, axis=0, keepdims=True)
        pb = p.astype(BF16)
        vt = vt_ref[j]
        pv = both(lambda c: jnp.dot(vt, pb[:, c], preferred_element_type=F32))
        acc_ref[...] = alpha * acc_ref[...] + pv
        m_ref[...] = m_new
        return carry

    lax.fori_loop(0, nk, body, 0, unroll=2)

    acc = acc_ref[...]
    inv = 1.0 / l_ref[...]
    lam = lam_ref[0]
    ot = acc[:, :tq] * inv[:, :tq] - lam * (acc[:, tq:] * inv[:, tq:])
    o = ot.T
    y = o * lax.rsqrt(jnp.mean(o * o, axis=-1, keepdims=True) + EPS) * g_ref[...] * (1.0 - LAM_INIT)
    z = z_ref[...].astype(F32)
    o_ref[...] = (y * (z * jax.nn.sigmoid(z))).astype(BF16)


def _attn_a(lam, qa, ka, vat, ka_m, vat_m, bias, bias_m, za, subln_g, *, tq, tk):
    nb, s, _ = qa.shape
    nk = s // tk
    ratio = tq // tk
    nv = bias.shape[1]
    head = lambda n: pl.BlockSpec((None, n, A_V_DIM), lambda b, h, i: (b, i, h))
    return pl.pallas_call(
        functools.partial(_attn_a_kernel, tq=tq, tk=tk, nk=nk, ratio=ratio, nv=nv),
        grid=(nb, A_HEADS, s // tq),
        in_specs=[
            pl.BlockSpec(memory_space=pltpu.SMEM),
            head(tq),
            pl.BlockSpec((None, s, A_V_DIM), lambda b, h, i: (b, 0, h)),
            pl.BlockSpec((None, nk, A_V_DIM, tk), lambda b, h, i: (b, 0, h, 0)),
            pl.BlockSpec((None, N_META, A_V_DIM), lambda b, h, i: (0, 0, h)),
            pl.BlockSpec((None, None, A_V_DIM, N_META), lambda b, h, i: (0, 0, h, 0)),
            pl.BlockSpec((None, nv, tk, tq), lambda b, h, i: (h, 0, 0, 0)),
            pl.BlockSpec((None, N_META, tq), lambda b, h, i: (h, 0, i)),
            head(tq),
            pl.BlockSpec((1, A_V_DIM), lambda b, h, i: (0, 0)),
        ],
        out_specs=head(tq),
        out_shape=jax.ShapeDtypeStruct((nb, s, A_WIDTH), BF16),
        scratch_shapes=[
            pltpu.VMEM((A_V_DIM, 2 * tq), BF16),
            pltpu.VMEM((1, 2 * tq), F32),
            pltpu.VMEM((1, 2 * tq), F32),
            pltpu.VMEM((A_V_DIM, 2 * tq), F32),
        ],
        compiler_params=pltpu.CompilerParams(
            dimension_semantics=("arbitrary", "arbitrary", "arbitrary"),
            vmem_limit_bytes=VMEM_LIMIT_BYTES),
        name="attn_a",
    )(lam, qa, ka, vat, ka_m, vat_m, bias, bias_m, za, subln_g)


_BQ = 256
_BK = _BQ + 2 * WINDOW
_PAD_FRONT = WINDOW - N_META


def _attn_b_kernel(sink_ref, q_ref, k0, k1, k2, k3, v0, v1, v2, v3, bias_ref, z_ref, o_ref, *, s_len):
    i = pl.program_id(1)
    k = jnp.concatenate([k0[...], k1[...], k2[...], k3[...]], axis=0).astype(F32)
    v = jnp.concatenate([v0[...], v1[...], v2[...], v3[...]], axis=0).astype(F32)
    lane = lax.broadcasted_iota(jnp.int32, k.shape, 1)
    lo = lane < B_HEAD_DIM

    def placed(t):
        tr = pltpu.roll(t, B_HEAD_DIM, axis=1)
        low = [jnp.where(lo, t, 0.0).astype(BF16), jnp.where(lo, tr, 0.0).astype(BF16)]
        high = [jnp.where(lo, 0.0, tr).astype(BF16), jnp.where(lo, 0.0, t).astype(BF16)]
        return low, high

    k_low, k_high = placed(k)
    v_low, v_high = placed(v)
    row_ext = i * _BQ + lax.broadcasted_iota(jnp.int32, (1, _BK), 1)
    vmask = jnp.where((row_ext >= _PAD_FRONT) & (row_ext < WINDOW + s_len), 0.0, NEG)

    for slab in range(B_HEADS // 2):
        g = slab // (B_HEADS // (2 * B_KV_HEADS))
        cols = slice(slab * LANES, (slab + 1) * LANES)
        qs = q_ref[:, cols]
        out = None
        for e in range(2):
            h = 2 * slab + e
            kx = (k_low if e == 0 else k_high)[g]
            vx = (v_low if e == 0 else v_high)[g]
            sc = lax.dot_general(qs, kx, (((1,), (1,)), ((), ())), preferred_element_type=F32)
            sc = sc + bias_ref[h] + vmask
            sk = sink_ref[h] * LOG2E
            m = jnp.maximum(jnp.max(sc, axis=-1, keepdims=True), sk)
            p = jnp.exp2(sc - m)
            l = jnp.sum(p, axis=-1, keepdims=True) + jnp.exp2(sk - m)
            pn = (p * (1.0 / l)).astype(BF16)
            oe = jnp.dot(pn, vx, preferred_element_type=F32)
            out = oe if out is None else out + oe
        z = z_ref[:, cols].astype(F32)
        o_ref[:, cols] = (out * (z * jax.nn.sigmoid(z))).astype(BF16)


def _attn_b(sink, qb, kb_ext, vb_ext, bias, zb):
    nb, s, _ = qb.shape
    row = pl.BlockSpec((None, _BQ, B_WIDTH), lambda b, i: (b, i, 0))
    nsub = _BK // WINDOW
    kv = [pl.BlockSpec((None, WINDOW, B_KV_WIDTH), (lambda b, i, t=t: (b, (_BQ // WINDOW) * i + t, 0)))
          for t in range(nsub)]
    return pl.pallas_call(
        functools.partial(_attn_b_kernel, s_len=s),
        grid=(nb, s // _BQ),
        in_specs=[pl.BlockSpec(memory_space=pltpu.SMEM), row] + kv + kv
                 + [pl.BlockSpec((B_HEADS, _BQ, _BK), lambda b, i: (0, 0, 0)), row],
        out_specs=row,
        out_shape=jax.ShapeDtypeStruct((nb, s, B_WIDTH), BF16),
        compiler_params=pltpu.CompilerParams(
            dimension_semantics=("arbitrary", "arbitrary"), vmem_limit_bytes=VMEM_LIMIT_BYTES),
        name="attn_b",
    )(sink, qb, *([kb_ext] * nsub), *([vb_ext] * nsub), bias, zb)


def _out_kernel(x_ref, a_ref, b_ref, ga_ref, gb_ref, woa_ref, wob_ref, wo_ref, g_ref, o_ref):
    ya = jnp.dot(a_ref[...], woa_ref[...], preferred_element_type=F32)
    yb = jnp.dot(b_ref[...], wob_ref[...], preferred_element_type=F32)
    mixed = (jax.nn.sigmoid(ga_ref[...].astype(F32)) * ya
             + jax.nn.sigmoid(gb_ref[...].astype(F32)) * yb)
    y = jnp.dot(mixed.astype(BF16), wo_ref[...], preferred_element_type=F32)
    y = y * lax.rsqrt(jnp.mean(y * y, axis=-1, keepdims=True) + EPS) * g_ref[...]
    o_ref[...] = x_ref[...] + y


def _output(x3, a, b, ga, gb, woa, wob, wo, g, *, tm):
    nb, s, d = x3.shape
    row = lambda n: pl.BlockSpec((None, tm, n), lambda bi, i: (bi, i, 0))
    const = lambda shape: pl.BlockSpec(shape, lambda bi, i: (0, 0), pipeline_mode=pl.Buffered(1))
    return pl.pallas_call(
        _out_kernel,
        grid=(nb, s // tm),
        in_specs=[row(d), row(A_WIDTH), row(B_WIDTH), row(d), row(d),
                  const((A_WIDTH, d)), const((B_WIDTH, d)), const((d, d)), const((1, d))],
        out_specs=row(d),
        out_shape=jax.ShapeDtypeStruct((nb, s, d), F32),
        compiler_params=pltpu.CompilerParams(
            dimension_semantics=("arbitrary", "arbitrary"), vmem_limit_bytes=VMEM_LIMIT_BYTES),
        name="out_proj",
    )(x3, a, b, ga, gb, woa, wob, wo, g)


def kernel(x, meta_tokens, rel_bias, pre_norm_g, w_in, b_in, lambda_q1, lambda_k1, lambda_q2, lambda_k2,
           subln_g, sink, w_out_a, w_out_b, w_out, post_norm_g):
    nb, s, d = x.shape
    tm = min(512, s)
    tq = min(512, s)
    tk = min(256, s)
    assert s % tm == 0 and s % tq == 0 and tq % tk == 0 and s % _BQ == 0

    w = w_in[0].astype(BF16)
    b = b_in[0][None, :]
    g_pre = pre_norm_g[0][None, :]
    lam = (jnp.exp(jnp.sum(lambda_q1[0].astype(F32) * lambda_k1[0].astype(F32)))
           - jnp.exp(jnp.sum(lambda_q2[0].astype(F32) * lambda_k2[0].astype(F32))) + LAM_INIT).reshape(1)

    qa, ka, vat, za, qb, kb, vb, zb, ga, gb = _project(x, g_pre, w, b, tm=tm, tkv=tk)
    meta = _project(meta_tokens[None], g_pre, w, b, tm=N_META, tkv=N_META)
    ka_m, vat_m, kb_m, vb_m = meta[1], meta[2], meta[5], meta[6]

    ratio = tq // tk
    nv = ratio + 4
    off = (jnp.arange(nv, dtype=jnp.int32) - 2)[:, None, None] * tk
    rel_a = off + jnp.arange(tk, dtype=jnp.int32)[None, :, None] - jnp.arange(tq, dtype=jnp.int32)[None, None, :]
    bias_a = _bias_tiles(rel_bias, _t5_bucket(rel_a).reshape(nv * tk, tq), head0=0, n_heads=A_HEADS,
                         tr=tk).reshape(A_HEADS, nv, tk, tq)
    rel_m = jnp.arange(N_META, dtype=jnp.int32)[:, None] - (jnp.arange(s, dtype=jnp.int32)[None, :] + N_META)
    bias_m = _bias_tiles(rel_bias, _t5_bucket(rel_m), head0=0, n_heads=A_HEADS, tr=N_META)
    rel_b = (jnp.arange(_BK, dtype=jnp.int32)[None, :] - WINDOW) - jnp.arange(_BQ, dtype=jnp.int32)[:, None]
    idx_b = jnp.where(jnp.abs(rel_b) <= WINDOW, _t5_bucket(rel_b), N_BUCKETS)
    bias_b = _bias_tiles(rel_bias, idx_b, head0=A_HEADS, n_heads=B_HEADS, tr=_BQ)

    oa = _attn_a(lam, qa, ka, vat, ka_m, vat_m, bias_a, bias_m, za, subln_g[0][None, :], tq=tq, tk=tk)

    def extend(tok, m):
        return jnp.concatenate([
            jnp.zeros((nb, _PAD_FRONT, B_KV_WIDTH), BF16),
            jnp.broadcast_to(m, (nb, N_META, B_KV_WIDTH)), tok,
            jnp.zeros((nb, WINDOW, B_KV_WIDTH), BF16)], axis=1)

    ob = _attn_b(sink[0], qb, extend(kb, kb_m), extend(vb, vb_m), bias_b, zb)

    return _output(x, oa, ob, ga, gb, w_out_a[0].astype(BF16), w_out_b[0].astype(BF16),
                   w_out[0].astype(BF16), post_norm_g[0][None, :], tm=tm)
```

```python
import functools
import math

import jax
import jax.numpy as jnp
from jax import lax
from jax.experimental import pallas as pl
from jax.experimental.pallas import tpu as pltpu

N_META = 16
WINDOW = 128
A_HEADS = 4
A_QK_DIM = 64
A_V_DIM = 2 * A_QK_DIM
A_WIDTH = A_HEADS * A_V_DIM
B_HEADS = 8
B_KV_HEADS = 2
B_HEAD_DIM = 64
B_WIDTH = B_HEADS * B_HEAD_DIM
B_KV_WIDTH = B_KV_HEADS * B_HEAD_DIM
N_BUCKETS = 32
MAX_DISTANCE = 128
EPS = 1e-6
NEG = -1e30
LOG2E = math.log2(math.e)
LAM_INIT = 0.8 - 0.6 * math.exp(-0.3 * 0)

LANES = 128
VMEM_LIMIT_BYTES = 56 * 1024 * 1024

_COLS = (A_WIDTH, A_WIDTH, A_WIDTH, A_WIDTH, B_WIDTH, B_KV_WIDTH, B_KV_WIDTH, B_WIDTH)
BF16 = jnp.bfloat16
F32 = jnp.float32


def _t5_bucket(rel):
    half = N_BUCKETS // 2
    max_exact = half // 2
    ret = jnp.where(rel > 0, half, 0)
    n = jnp.abs(rel)
    nf = jnp.maximum(n, 1).astype(F32)
    large = max_exact + (jnp.log(nf / max_exact) / math.log(MAX_DISTANCE / max_exact)
                         * (half - max_exact)).astype(jnp.int32)
    large = jnp.minimum(large, half - 1)
    return ret + jnp.where(n < max_exact, n, large)


def _proj_kernel(x_ref, g_ref, w_ref, b_ref,
                 qa_ref, ka_ref, vat_ref, za_ref, qb_ref, kb_ref, vb_ref, zb_ref, ga_ref, gb_ref,
                 *, d_model, tkv):
    x = x_ref[...]
    hn = (x * lax.rsqrt(jnp.mean(x * x, axis=-1, keepdims=True) + EPS) * g_ref[...]).astype(BF16)

    def seg(c0, n):
        return jnp.dot(hn, w_ref[:, c0:c0 + n], preferred_element_type=F32) + b_ref[:, c0:c0 + n]

    qscale = (A_QK_DIM ** -0.5) * LOG2E
    c = 0
    qa_ref[...] = (seg(c, A_WIDTH) * qscale).astype(BF16); c += A_WIDTH
    ka_ref[...] = seg(c, A_WIDTH).astype(BF16); c += A_WIDTH
    va = seg(c, A_WIDTH); c += A_WIDTH
    for t in range(vat_ref.shape[0]):
        vat_ref[t] = va[t * tkv:(t + 1) * tkv, :].T.astype(BF16)
    za_ref[...] = seg(c, A_WIDTH).astype(BF16); c += A_WIDTH
    qb_ref[...] = (seg(c, B_WIDTH) * qscale).astype(BF16); c += B_WIDTH
    kb_ref[...] = seg(c, B_KV_WIDTH).astype(BF16); c += B_KV_WIDTH
    vb_ref[...] = seg(c, B_KV_WIDTH).astype(BF16); c += B_KV_WIDTH
    zb_ref[...] = seg(c, B_WIDTH).astype(BF16); c += B_WIDTH
    ga_ref[...] = seg(c, d_model).astype(BF16); c += d_model
    gb_ref[...] = seg(c, d_model).astype(BF16)


def _project(x3, g, w, b, *, tm, tkv):
    nb, s, d = x3.shape
    d_in = w.shape[1]
    widths = _COLS + (d, d)
    nt = s // tm
    row = lambda n: pl.BlockSpec((None, tm, n), lambda bi, i: (bi, i, 0))
    const = lambda shape: pl.BlockSpec(shape, lambda bi, i: (0, 0), pipeline_mode=pl.Buffered(1))
    out_specs, out_shapes = [], []
    for idx, n in enumerate(widths):
        if idx == 2:
            out_specs.append(pl.BlockSpec((None, tm // tkv, n, tkv), lambda bi, i: (bi, i, 0, 0)))
            out_shapes.append(jax.ShapeDtypeStruct((nb, s // tkv, n, tkv), BF16))
        else:
            out_specs.append(row(n))
            out_shapes.append(jax.ShapeDtypeStruct((nb, s, n), BF16))
    return pl.pallas_call(
        functools.partial(_proj_kernel, d_model=d, tkv=tkv),
        grid=(nb, nt),
        in_specs=[row(d), const((1, d)), const((d, d_in)), const((1, d_in))],
        out_specs=out_specs,
        out_shape=out_shapes,
        compiler_params=pltpu.CompilerParams(
            dimension_semantics=("arbitrary", "arbitrary"), vmem_limit_bytes=VMEM_LIMIT_BYTES),
        name="proj",
    )(x3, g, w, b)


def _bias_kernel(tab_ref, idx_ref, o_ref, *, head0):
    h = pl.program_id(0) + head0
    idx = idx_ref[...]
    acc = jnp.full(idx.shape, NEG, F32)
    for bkt in range(N_BUCKETS):
        acc = jnp.where(idx == bkt, tab_ref[bkt, h] * LOG2E, acc)
    o_ref[...] = acc


def _bias_tiles(tab, idx, *, head0, n_heads, tr):
    r, c = idx.shape
    return pl.pallas_call(
        functools.partial(_bias_kernel, head0=head0),
        grid=(n_heads, r // tr),
        in_specs=[pl.BlockSpec(memory_space=pltpu.SMEM),
                  pl.BlockSpec((tr, c), lambda h, i: (i, 0))],
        out_specs=pl.BlockSpec((None, tr, c), lambda h, i: (h, i, 0)),
        out_shape=jax.ShapeDtypeStruct((n_heads, r, c), F32),
        compiler_params=pltpu.CompilerParams(dimension_semantics=("arbitrary", "arbitrary")),
        name="bias_tiles",
    )(tab, idx)


def _attn_a_kernel(lam_ref, q_ref, k_ref, vt_ref, km_ref, vmt_ref, bias_ref, bm_ref, z_ref, g_ref,
                   o_ref, qt_ref, m_ref, l_ref, acc_ref, s_ref, cm_ref, *, tq, tk, nk, ratio, nv):
    i = pl.program_id(2)
    q = q_ref[...].astype(F32)
    lane = lax.broadcasted_iota(jnp.int32, q.shape, 1)
    qt_ref[:, :tq] = jnp.where(lane < A_QK_DIM, q, 0.0).T.astype(BF16)
    qt_ref[:, tq:] = jnp.where(lane >= A_QK_DIM, q, 0.0).T.astype(BF16)
    qt = qt_ref[...]

    def both(fn):
        return jnp.concatenate([fn(slice(0, tq)), fn(slice(tq, 2 * tq))], axis=1)

    bm = bm_ref[...]
    s = jnp.dot(km_ref[...], qt, preferred_element_type=F32) + jnp.concatenate([bm, bm], axis=1)
    m0 = jnp.max(s, axis=0, keepdims=True)
    p = jnp.exp2(s - m0)
    pb = p.astype(BF16)
    vmt = vmt_ref[...]
    m_ref[...] = m0
    l_ref[...] = jnp.sum(p, axis=0, keepdims=True)
    acc_ref[...] = both(lambda c: jnp.dot(vmt, pb[:, c], preferred_element_type=F32))

    def produce(j, slot):
        ks = pl.multiple_of(j * tk, tk)
        s = jnp.dot(k_ref[pl.ds(ks, tk), :], qt_ref[...], preferred_element_type=F32)
        bt = bias_ref[jnp.clip(j - i * ratio + 2, 0, nv - 1)]
        s = s + jnp.concatenate([bt, bt], axis=1)
        s_ref[slot] = s
        cm_ref[slot] = jnp.max(s, axis=0, keepdims=True)

    def consume(j, slot):
        m_old = m_ref[...]
        m_new = jnp.maximum(m_old, cm_ref[slot])
        alpha = jnp.exp2(m_old - m_new)
        p = jnp.exp2(s_ref[slot] - m_new)
        l_ref[...] = alpha * l_ref[...] + jnp.sum(p, axis=0, keepdims=True)
        pb = p.astype(BF16)
        vt = vt_ref[j]
        pv = both(lambda c: jnp.dot(vt, pb[:, c], preferred_element_type=F32))
        acc_ref[...] = alpha * acc_ref[...] + pv
        m_ref[...] = m_new

    produce(0, 0)

    def pair(jj, carry):
        j0 = 2 * jj
        produce(j0 + 1, 1)
        consume(j0, 0)
        produce(jnp.minimum(j0 + 2, nk - 1), 0)
        consume(j0 + 1, 1)
        return carry

    lax.fori_loop(0, nk // 2, pair, 0)

    acc = acc_ref[...]
    inv = 1.0 / l_ref[...]
    lam = lam_ref[0]
    ot = acc[:, :tq] * inv[:, :tq] - lam * (acc[:, tq:] * inv[:, tq:])
    o = ot.T
    y = o * lax.rsqrt(jnp.mean(o * o, axis=-1, keepdims=True) + EPS) * g_ref[...] * (1.0 - LAM_INIT)
    z = z_ref[...].astype(F32)
    o_ref[...] = (y * (z * jax.nn.sigmoid(z))).astype(BF16)


def _attn_a(lam, qa, ka, vat, ka_m, vat_m, bias, bias_m, za, subln_g, *, tq, tk):
    nb, s, _ = qa.shape
    nk = s // tk
    ratio = tq // tk
    nv = bias.shape[1]
    head = lambda n: pl.BlockSpec((None, n, A_V_DIM), lambda b, h, i: (b, i, h))
    return pl.pallas_call(
        functools.partial(_attn_a_kernel, tq=tq, tk=tk, nk=nk, ratio=ratio, nv=nv),
        grid=(nb, A_HEADS, s // tq),
        in_specs=[
            pl.BlockSpec(memory_space=pltpu.SMEM),
            head(tq),
            pl.BlockSpec((None, s, A_V_DIM), lambda b, h, i: (b, 0, h)),
            pl.BlockSpec((None, nk, A_V_DIM, tk), lambda b, h, i: (b, 0, h, 0)),
            pl.BlockSpec((None, N_META, A_V_DIM), lambda b, h, i: (0, 0, h)),
            pl.BlockSpec((None, None, A_V_DIM, N_META), lambda b, h, i: (0, 0, h, 0)),
            pl.BlockSpec((None, nv, tk, tq), lambda b, h, i: (h, 0, 0, 0)),
            pl.BlockSpec((None, N_META, tq), lambda b, h, i: (h, 0, i)),
            head(tq),
            pl.BlockSpec((1, A_V_DIM), lambda b, h, i: (0, 0)),
        ],
        out_specs=head(tq),
        out_shape=jax.ShapeDtypeStruct((nb, s, A_WIDTH), BF16),
        scratch_shapes=[
            pltpu.VMEM((A_V_DIM, 2 * tq), BF16),
            pltpu.VMEM((1, 2 * tq), F32),
            pltpu.VMEM((1, 2 * tq), F32),
            pltpu.VMEM((A_V_DIM, 2 * tq), F32),
            pltpu.VMEM((2, tk, 2 * tq), F32),
            pltpu.VMEM((2, 1, 2 * tq), F32),
        ],
        compiler_params=pltpu.CompilerParams(
            dimension_semantics=("arbitrary", "arbitrary", "arbitrary"),
            vmem_limit_bytes=VMEM_LIMIT_BYTES),
        name="attn_a",
    )(lam, qa, ka, vat, ka_m, vat_m, bias, bias_m, za, subln_g)


_BQ = 256
_BK = _BQ + 2 * WINDOW
_PAD_FRONT = WINDOW - N_META


def _attn_b_kernel(sink_ref, q_ref, k0, k1, k2, k3, v0, v1, v2, v3, bias_ref, z_ref, o_ref, *, s_len):
    i = pl.program_id(1)
    k = jnp.concatenate([k0[...], k1[...], k2[...], k3[...]], axis=0).astype(F32)
    v = jnp.concatenate([v0[...], v1[...], v2[...], v3[...]], axis=0).astype(F32)
    lane = lax.broadcasted_iota(jnp.int32, k.shape, 1)
    lo = lane < B_HEAD_DIM

    def placed(t):
        tr = pltpu.roll(t, B_HEAD_DIM, axis=1)
        low = [jnp.where(lo, t, 0.0).astype(BF16), jnp.where(lo, tr, 0.0).astype(BF16)]
        high = [jnp.where(lo, 0.0, tr).astype(BF16), jnp.where(lo, 0.0, t).astype(BF16)]
        return low, high

    k_low, k_high = placed(k)
    v_low, v_high = placed(v)
    row_ext = i * _BQ + lax.broadcasted_iota(jnp.int32, (1, _BK), 1)
    vmask = jnp.where((row_ext >= _PAD_FRONT) & (row_ext < WINDOW + s_len), 0.0, NEG)

    for slab in range(B_HEADS // 2):
        g = slab // (B_HEADS // (2 * B_KV_HEADS))
        cols = slice(slab * LANES, (slab + 1) * LANES)
        qs = q_ref[:, cols]
        out = None
        for e in range(2):
            h = 2 * slab + e
            kx = (k_low if e == 0 else k_high)[g]
            vx = (v_low if e == 0 else v_high)[g]
            sc = lax.dot_general(qs, kx, (((1,), (1,)), ((), ())), preferred_element_type=F32)
            sc = sc + bias_ref[h] + vmask
            sk = sink_ref[h] * LOG2E
            m = jnp.maximum(jnp.max(sc, axis=-1, keepdims=True), sk)
            p = jnp.exp2(sc - m)
            l = jnp.sum(p, axis=-1, keepdims=True) + jnp.exp2(sk - m)
            pn = (p * (1.0 / l)).astype(BF16)
            oe = jnp.dot(pn, vx, preferred_element_type=F32)
            out = oe if out is None else out + oe
        z = z_ref[:, cols].astype(F32)
        o_ref[:, cols] = (out * (z * jax.nn.sigmoid(z))).astype(BF16)


def _attn_b(sink, qb, kb_ext, vb_ext, bias, zb):
    nb, s, _ = qb.shape
    row = pl.BlockSpec((None, _BQ, B_WIDTH), lambda b, i: (b, i, 0))
    nsub = _BK // WINDOW
    kv = [pl.BlockSpec((None, WINDOW, B_KV_WIDTH), (lambda b, i, t=t: (b, (_BQ // WINDOW) * i + t, 0)))
          for t in range(nsub)]
    return pl.pallas_call(
        functools.partial(_attn_b_kernel, s_len=s),
        grid=(nb, s // _BQ),
        in_specs=[pl.BlockSpec(memory_space=pltpu.SMEM), row] + kv + kv
                 + [pl.BlockSpec((B_HEADS, _BQ, _BK), lambda b, i: (0, 0, 0)), row],
        out_specs=row,
        out_shape=jax.ShapeDtypeStruct((nb, s, B_WIDTH), BF16),
        compiler_params=pltpu.CompilerParams(
            dimension_semantics=("arbitrary", "arbitrary"), vmem_limit_bytes=VMEM_LIMIT_BYTES),
        name="attn_b",
    )(sink, qb, *([kb_ext] * nsub), *([vb_ext] * nsub), bias, zb)


def _out_kernel(x_ref, a_ref, b_ref, ga_ref, gb_ref, woa_ref, wob_ref, wo_ref, g_ref, o_ref):
    ya = jnp.dot(a_ref[...], woa_ref[...], preferred_element_type=F32)
    yb = jnp.dot(b_ref[...], wob_ref[...], preferred_element_type=F32)
    mixed = (jax.nn.sigmoid(ga_ref[...].astype(F32)) * ya
             + jax.nn.sigmoid(gb_ref[...].astype(F32)) * yb)
    y = jnp.dot(mixed.astype(BF16), wo_ref[...], preferred_element_type=F32)
    y = y * lax.rsqrt(jnp.mean(y * y, axis=-1, keepdims=True) + EPS) * g_ref[...]
    o_ref[...] = x_ref[...] + y


def _output(x3, a, b, ga, gb, woa, wob, wo, g, *, tm):
    nb, s, d = x3.shape
    row = lambda n: pl.BlockSpec((None, tm, n), lambda bi, i: (bi, i, 0))
    const = lambda shape: pl.BlockSpec(shape, lambda bi, i: (0, 0), pipeline_mode=pl.Buffered(1))
    return pl.pallas_call(
        _out_kernel,
        grid=(nb, s // tm),
        in_specs=[row(d), row(A_WIDTH), row(B_WIDTH), row(d), row(d),
                  const((A_WIDTH, d)), const((B_WIDTH, d)), const((d, d)), const((1, d))],
        out_specs=row(d),
        out_shape=jax.ShapeDtypeStruct((nb, s, d), F32),
        compiler_params=pltpu.CompilerParams(
            dimension_semantics=("arbitrary", "arbitrary"), vmem_limit_bytes=VMEM_LIMIT_BYTES),
        name="out_proj",
    )(x3, a, b, ga, gb, woa, wob, wo, g)


def kernel(x, meta_tokens, rel_bias, pre_norm_g, w_in, b_in, lambda_q1, lambda_k1, lambda_q2, lambda_k2,
           subln_g, sink, w_out_a, w_out_b, w_out, post_norm_g):
    nb, s, d = x.shape
    tm = min(512, s)
    tq = min(512, s)
    tk = min(256, s)
    assert s % tm == 0 and s % tq == 0 and tq % tk == 0 and s % _BQ == 0

    w = w_in[0].astype(BF16)
    b = b_in[0][None, :]
    g_pre = pre_norm_g[0][None, :]
    lam = (jnp.exp(jnp.sum(lambda_q1[0].astype(F32) * lambda_k1[0].astype(F32)))
           - jnp.exp(jnp.sum(lambda_q2[0].astype(F32) * lambda_k2[0].astype(F32))) + LAM_INIT).reshape(1)

    qa, ka, vat, za, qb, kb, vb, zb, ga, gb = _project(x, g_pre, w, b, tm=tm, tkv=tk)
    meta = _project(meta_tokens[None], g_pre, w, b, tm=N_META, tkv=N_META)
    ka_m, vat_m, kb_m, vb_m = meta[1], meta[2], meta[5], meta[6]

    ratio = tq // tk
    nv = ratio + 4
    off = (jnp.arange(nv, dtype=jnp.int32) - 2)[:, None, None] * tk
    rel_a = off + jnp.arange(tk, dtype=jnp.int32)[None, :, None] - jnp.arange(tq, dtype=jnp.int32)[None, None, :]
    bias_a = _bias_tiles(rel_bias, _t5_bucket(rel_a).reshape(nv * tk, tq), head0=0, n_heads=A_HEADS,
                         tr=tk).reshape(A_HEADS, nv, tk, tq)
    rel_m = jnp.arange(N_META, dtype=jnp.int32)[:, None] - (jnp.arange(s, dtype=jnp.int32)[None, :] + N_META)
    bias_m = _bias_tiles(rel_bias, _t5_bucket(rel_m), head0=0, n_heads=A_HEADS, tr=N_META)
    rel_b = (jnp.arange(_BK, dtype=jnp.int32)[None, :] - WINDOW) - jnp.arange(_BQ, dtype=jnp.int32)[:, None]
    idx_b = jnp.where(jnp.abs(rel_b) <= WINDOW, _t5_bucket(rel_b), N_BUCKETS)
    bias_b = _bias_tiles(rel_bias, idx_b, head0=A_HEADS, n_heads=B_HEADS, tr=_BQ)

    oa = _attn_a(lam, qa, ka, vat, ka_m, vat_m, bias_a, bias_m, za, subln_g[0][None, :], tq=tq, tk=tk)

    def extend(tok, m):
        return jnp.concatenate([
            jnp.zeros((nb, _PAD_FRONT, B_KV_WIDTH), BF16),
            jnp.broadcast_to(m, (nb, N_META, B_KV_WIDTH)), tok,
            jnp.zeros((nb, WINDOW, B_KV_WIDTH), BF16)], axis=1)

    ob = _attn_b(sink[0], qb, extend(kb, kb_m), extend(vb, vb_m), bias_b, zb)

    return _output(x, oa, ob, ga, gb, w_out_a[0].astype(BF16), w_out_b[0].astype(BF16),
                   w_out[0].astype(BF16), post_norm_g[0][None, :], tm=tm)
```

```python
import functools
import math

import jax
import jax.numpy as jnp
from jax import lax
from jax.experimental import pallas as pl
from jax.experimental.pallas import tpu as pltpu

N_META = 16
WINDOW = 128
A_HEADS = 4
A_QK_DIM = 64
A_V_DIM = 2 * A_QK_DIM
A_WIDTH = A_HEADS * A_V_DIM
B_HEADS = 8
B_KV_HEADS = 2
B_HEAD_DIM = 64
B_WIDTH = B_HEADS * B_HEAD_DIM
B_KV_WIDTH = B_KV_HEADS * B_HEAD_DIM
N_BUCKETS = 32
MAX_DISTANCE = 128
EPS = 1e-6
NEG = -1e30
LOG2E = math.log2(math.e)
LAM_INIT = 0.8 - 0.6 * math.exp(-0.3 * 0)

LANES = 128
SUM_ROWS = 16
VMEM_LIMIT_BYTES = 56 * 1024 * 1024

_COLS = (A_WIDTH, A_WIDTH, A_WIDTH, A_WIDTH, B_WIDTH, B_KV_WIDTH, B_KV_WIDTH, B_WIDTH)
BF16 = jnp.bfloat16
F32 = jnp.float32


def _t5_bucket(rel):
    half = N_BUCKETS // 2
    max_exact = half // 2
    ret = jnp.where(rel > 0, half, 0)
    n = jnp.abs(rel)
    nf = jnp.maximum(n, 1).astype(F32)
    large = max_exact + (jnp.log(nf / max_exact) / math.log(MAX_DISTANCE / max_exact)
                         * (half - max_exact)).astype(jnp.int32)
    large = jnp.minimum(large, half - 1)
    return ret + jnp.where(n < max_exact, n, large)


def _proj_kernel(x_ref, g_ref, w_ref, b_ref,
                 qa_ref, ka_ref, vat_ref, za_ref, qb_ref, kb_ref, vb_ref, zb_ref, ga_ref, gb_ref,
                 *, d_model, tkv):
    x = x_ref[...]
    hn = (x * lax.rsqrt(jnp.mean(x * x, axis=-1, keepdims=True) + EPS) * g_ref[...]).astype(BF16)

    def seg(c0, n):
        return jnp.dot(hn, w_ref[:, c0:c0 + n], preferred_element_type=F32) + b_ref[:, c0:c0 + n]

    qscale = (A_QK_DIM ** -0.5) * LOG2E
    c = 0
    qa_ref[...] = (seg(c, A_WIDTH) * qscale).astype(BF16); c += A_WIDTH
    ka_ref[...] = seg(c, A_WIDTH).astype(BF16); c += A_WIDTH
    va = seg(c, A_WIDTH); c += A_WIDTH
    for t in range(vat_ref.shape[0]):
        vat_ref[t] = va[t * tkv:(t + 1) * tkv, :].T.astype(BF16)
    za_ref[...] = seg(c, A_WIDTH).astype(BF16); c += A_WIDTH
    qb_ref[...] = (seg(c, B_WIDTH) * qscale).astype(BF16); c += B_WIDTH
    kb_ref[...] = seg(c, B_KV_WIDTH).astype(BF16); c += B_KV_WIDTH
    vb_ref[...] = seg(c, B_KV_WIDTH).astype(BF16); c += B_KV_WIDTH
    zb_ref[...] = seg(c, B_WIDTH).astype(BF16); c += B_WIDTH
    ga_ref[...] = seg(c, d_model).astype(BF16); c += d_model
    gb_ref[...] = seg(c, d_model).astype(BF16)


def _project(x3, g, w, b, *, tm, tkv):
    nb, s, d = x3.shape
    d_in = w.shape[1]
    widths = _COLS + (d, d)
    nt = s // tm
    row = lambda n: pl.BlockSpec((None, tm, n), lambda bi, i: (bi, i, 0))
    const = lambda shape: pl.BlockSpec(shape, lambda bi, i: (0, 0), pipeline_mode=pl.Buffered(1))
    out_specs, out_shapes = [], []
    for idx, n in enumerate(widths):
        if idx == 2:
            out_specs.append(pl.BlockSpec((None, tm // tkv, n, tkv), lambda bi, i: (bi, i, 0, 0)))
            out_shapes.append(jax.ShapeDtypeStruct((nb, s // tkv, n, tkv), BF16))
        else:
            out_specs.append(row(n))
            out_shapes.append(jax.ShapeDtypeStruct((nb, s, n), BF16))
    return pl.pallas_call(
        functools.partial(_proj_kernel, d_model=d, tkv=tkv),
        grid=(nb, nt),
        in_specs=[row(d), const((1, d)), const((d, d_in)), const((1, d_in))],
        out_specs=out_specs,
        out_shape=out_shapes,
        compiler_params=pltpu.CompilerParams(
            dimension_semantics=("arbitrary", "arbitrary"), vmem_limit_bytes=VMEM_LIMIT_BYTES),
        name="proj",
    )(x3, g, w, b)


def _bias_kernel(tab_ref, idx_ref, o_ref, *, head0):
    h = pl.program_id(0) + head0
    idx = idx_ref[...]
    acc = jnp.full(idx.shape, NEG, F32)
    for bkt in range(N_BUCKETS):
        acc = jnp.where(idx == bkt, tab_ref[bkt, h] * LOG2E, acc)
    o_ref[...] = acc


def _bias_tiles(tab, idx, *, head0, n_heads, tr):
    r, c = idx.shape
    return pl.pallas_call(
        functools.partial(_bias_kernel, head0=head0),
        grid=(n_heads, r // tr),
        in_specs=[pl.BlockSpec(memory_space=pltpu.SMEM),
                  pl.BlockSpec((tr, c), lambda h, i: (i, 0))],
        out_specs=pl.BlockSpec((None, tr, c), lambda h, i: (h, i, 0)),
        out_shape=jax.ShapeDtypeStruct((n_heads, r, c), F32),
        compiler_params=pltpu.CompilerParams(dimension_semantics=("arbitrary", "arbitrary")),
        name="bias_tiles",
    )(tab, idx)


def _attn_a_kernel(lam_ref, q_ref, k_ref, vt_ref, km_ref, vmt_ref, bias_ref, bm_ref, z_ref, g_ref,
                   o_ref, qt_ref, m_ref, acc_ref, s_ref, cm_ref, *, tq, tk, nk, ratio, nv):
    i = pl.program_id(2)
    q = q_ref[...].astype(F32)
    lane = lax.broadcasted_iota(jnp.int32, q.shape, 1)
    qt_ref[:, :tq] = jnp.where(lane < A_QK_DIM, q, 0.0).T.astype(BF16)
    qt_ref[:, tq:] = jnp.where(lane >= A_QK_DIM, q, 0.0).T.astype(BF16)
    qt = qt_ref[...]

    def both(fn):
        return jnp.concatenate([fn(slice(0, tq)), fn(slice(tq, 2 * tq))], axis=1)

    def pv_and_sum(vt, pb):
        vt1 = jnp.concatenate([vt, jnp.ones((SUM_ROWS, vt.shape[1]), BF16)], axis=0)
        return both(lambda c: jnp.dot(vt1, pb[:, c], preferred_element_type=F32))

    bm = bm_ref[...]
    s = jnp.dot(km_ref[...], qt, preferred_element_type=F32) + jnp.concatenate([bm, bm], axis=1)
    m0 = jnp.max(s, axis=0, keepdims=True)
    m_ref[...] = m0
    acc_ref[...] = pv_and_sum(vmt_ref[...], jnp.exp2(s - m0).astype(BF16))

    def produce(j, slot):
        ks = pl.multiple_of(j * tk, tk)
        s = jnp.dot(k_ref[pl.ds(ks, tk), :], qt_ref[...], preferred_element_type=F32)
        s_ref[slot] = s
        cm_ref[slot] = jnp.max(s, axis=0, keepdims=True)

    def update(j, s, shift, cmax):
        m_old = m_ref[...]
        m_new = jnp.maximum(m_old, cmax)
        alpha = jnp.exp2(m_old - m_new)
        pb = jnp.exp2(s - (m_new - shift)).astype(BF16)
        acc_ref[...] = alpha * acc_ref[...] + pv_and_sum(vt_ref[j], pb)
        m_ref[...] = m_new

    def consume_far(const_row):
        def fn(j, slot):
            update(j, s_ref[slot], const_row, cm_ref[slot] + const_row)
        return fn

    def consume_near(j, slot):
        bt = bias_ref[jnp.clip(j - i * ratio + 2, 0, nv - 1)]
        s = s_ref[slot] + jnp.concatenate([bt, bt], axis=1)
        update(j, s, 0.0, jnp.max(s, axis=0, keepdims=True))

    def pairs(consume):
        def body(jj, carry):
            j0 = 2 * jj
            produce(j0 + 1, 1)
            consume(j0, 0)
            produce(jnp.minimum(j0 + 2, nk - 1), 0)
            consume(j0 + 1, 1)
            return carry
        return body

    def far_const(v):
        row = bias_ref[v, 0:1, :]
        return jnp.concatenate([row, row], axis=1)

    n_pairs = nk // 2
    p_lo = jnp.maximum(i * ratio - 1, 0) // 2
    p_hi = jnp.minimum((i * ratio + ratio) // 2 + 1, n_pairs)
    produce(0, 0)
    lax.fori_loop(0, p_lo, pairs(consume_far(far_const(0))), 0)
    lax.fori_loop(p_lo, p_hi, pairs(consume_near), 0)
    lax.fori_loop(p_hi, n_pairs, pairs(consume_far(far_const(nv - 1))), 0)

    acc = acc_ref[...]
    inv = 1.0 / acc[A_V_DIM:A_V_DIM + 1, :]
    lam = lam_ref[0]
    ot = acc[:A_V_DIM, :tq] * inv[:, :tq] - lam * (acc[:A_V_DIM, tq:] * inv[:, tq:])
    o = ot.T
    y = o * lax.rsqrt(jnp.mean(o * o, axis=-1, keepdims=True) + EPS) * g_ref[...] * (1.0 - LAM_INIT)
    z = z_ref[...].astype(F32)
    o_ref[...] = (y * (z * jax.nn.sigmoid(z))).astype(BF16)


def _attn_a(lam, qa, ka, vat, ka_m, vat_m, bias, bias_m, za, subln_g, *, tq, tk):
    nb, s, _ = qa.shape
    nk = s // tk
    ratio = tq // tk
    nv = bias.shape[1]
    head = lambda n: pl.BlockSpec((None, n, A_V_DIM), lambda b, h, i: (b, i, h))
    return pl.pallas_call(
        functools.partial(_attn_a_kernel, tq=tq, tk=tk, nk=nk, ratio=ratio, nv=nv),
        grid=(nb, A_HEADS, s // tq),
        in_specs=[
            pl.BlockSpec(memory_space=pltpu.SMEM),
            head(tq),
            pl.BlockSpec((None, s, A_V_DIM), lambda b, h, i: (b, 0, h)),
            pl.BlockSpec((None, nk, A_V_DIM, tk), lambda b, h, i: (b, 0, h, 0)),
            pl.BlockSpec((None, N_META, A_V_DIM), lambda b, h, i: (0, 0, h)),
            pl.BlockSpec((None, None, A_V_DIM, N_META), lambda b, h, i: (0, 0, h, 0)),
            pl.BlockSpec((None, nv, tk, tq), lambda b, h, i: (h, 0, 0, 0)),
            pl.BlockSpec((None, N_META, tq), lambda b, h, i: (h, 0, i)),
            head(tq),
            pl.BlockSpec((1, A_V_DIM), lambda b, h, i: (0, 0)),
        ],
        out_specs=head(tq),
        out_shape=jax.ShapeDtypeStruct((nb, s, A_WIDTH), BF16),
        scratch_shapes=[
            pltpu.VMEM((A_V_DIM, 2 * tq), BF16),
            pltpu.VMEM((1, 2 * tq), F32),
            pltpu.VMEM((A_V_DIM + SUM_ROWS, 2 * tq), F32),
            pltpu.VMEM((2, tk, 2 * tq), F32),
            pltpu.VMEM((2, 1, 2 * tq), F32),
        ],
        compiler_params=pltpu.CompilerParams(
            dimension_semantics=("arbitrary", "arbitrary", "arbitrary"),
            vmem_limit_bytes=VMEM_LIMIT_BYTES),
        name="attn_a",
    )(lam, qa, ka, vat, ka_m, vat_m, bias, bias_m, za, subln_g)


_BQ = 256
_BK = _BQ + 2 * WINDOW
_PAD_FRONT = WINDOW - N_META


def _attn_b_kernel(sink_ref, q_ref, k0, k1, k2, k3, v0, v1, v2, v3, bias_ref, z_ref, o_ref, *, s_len):
    i = pl.program_id(1)
    k = jnp.concatenate([k0[...], k1[...], k2[...], k3[...]], axis=0).astype(F32)
    v = jnp.concatenate([v0[...], v1[...], v2[...], v3[...]], axis=0).astype(F32)
    lane = lax.broadcasted_iota(jnp.int32, k.shape, 1)
    lo = lane < B_HEAD_DIM

    def placed(t):
        tr = pltpu.roll(t, B_HEAD_DIM, axis=1)
        low = [jnp.where(lo, t, 0.0).astype(BF16), jnp.where(lo, tr, 0.0).astype(BF16)]
        high = [jnp.where(lo, 0.0, tr).astype(BF16), jnp.where(lo, 0.0, t).astype(BF16)]
        return low, high

    k_low, k_high = placed(k)
    v_low, v_high = placed(v)
    row_ext = i * _BQ + lax.broadcasted_iota(jnp.int32, (1, _BK), 1)
    vmask = jnp.where((row_ext >= _PAD_FRONT) & (row_ext < WINDOW + s_len), 0.0, NEG)

    for slab in range(B_HEADS // 2):
        g = slab // (B_HEADS // (2 * B_KV_HEADS))
        cols = slice(slab * LANES, (slab + 1) * LANES)
        qs = q_ref[:, cols]
        out = None
        for e in range(2):
            h = 2 * slab + e
            kx = (k_low if e == 0 else k_high)[g]
            vx = (v_low if e == 0 else v_high)[g]
            sc = lax.dot_general(qs, kx, (((1,), (1,)), ((), ())), preferred_element_type=F32)
            sc = sc + bias_ref[h] + vmask
            sk = sink_ref[h] * LOG2E
            m = jnp.maximum(jnp.max(sc, axis=-1, keepdims=True), sk)
            p = jnp.exp2(sc - m)
            l = jnp.sum(p, axis=-1, keepdims=True) + jnp.exp2(sk - m)
            pn = (p * (1.0 / l)).astype(BF16)
            oe = jnp.dot(pn, vx, preferred_element_type=F32)
            out = oe if out is None else out + oe
        z = z_ref[:, cols].astype(F32)
        o_ref[:, cols] = (out * (z * jax.nn.sigmoid(z))).astype(BF16)


def _attn_b(sink, qb, kb_ext, vb_ext, bias, zb):
    nb, s, _ = qb.shape
    row = pl.BlockSpec((None, _BQ, B_WIDTH), lambda b, i: (b, i, 0))
    nsub = _BK // WINDOW
    kv = [pl.BlockSpec((None, WINDOW, B_KV_WIDTH), (lambda b, i, t=t: (b, (_BQ // WINDOW) * i + t, 0)))
          for t in range(nsub)]
    return pl.pallas_call(
        functools.partial(_attn_b_kernel, s_len=s),
        grid=(nb, s // _BQ),
        in_specs=[pl.BlockSpec(memory_space=pltpu.SMEM), row] + kv + kv
                 + [pl.BlockSpec((B_HEADS, _BQ, _BK), lambda b, i: (0, 0, 0)), row],
        out_specs=row,
        out_shape=jax.ShapeDtypeStruct((nb, s, B_WIDTH), BF16),
        compiler_params=pltpu.CompilerParams(
            dimension_semantics=("arbitrary", "arbitrary"), vmem_limit_bytes=VMEM_LIMIT_BYTES),
        name="attn_b",
    )(sink, qb, *([kb_ext] * nsub), *([vb_ext] * nsub), bias, zb)


def _out_kernel(x_ref, a_ref, b_ref, ga_ref, gb_ref, woa_ref, wob_ref, wo_ref, g_ref, o_ref):
    ya = jnp.dot(a_ref[...], woa_ref[...], preferred_element_type=F32)
    yb = jnp.dot(b_ref[...], wob_ref[...], preferred_element_type=F32)
    mixed = (jax.nn.sigmoid(ga_ref[...].astype(F32)) * ya
             + jax.nn.sigmoid(gb_ref[...].astype(F32)) * yb)
    y = jnp.dot(mixed.astype(BF16), wo_ref[...], preferred_element_type=F32)
    y = y * lax.rsqrt(jnp.mean(y * y, axis=-1, keepdims=True) + EPS) * g_ref[...]
    o_ref[...] = x_ref[...] + y


def _output(x3, a, b, ga, gb, woa, wob, wo, g, *, tm):
    nb, s, d = x3.shape
    row = lambda n: pl.BlockSpec((None, tm, n), lambda bi, i: (bi, i, 0))
    const = lambda shape: pl.BlockSpec(shape, lambda bi, i: (0, 0), pipeline_mode=pl.Buffered(1))
    return pl.pallas_call(
        _out_kernel,
        grid=(nb, s // tm),
        in_specs=[row(d), row(A_WIDTH), row(B_WIDTH), row(d), row(d),
                  const((A_WIDTH, d)), const((B_WIDTH, d)), const((d, d)), const((1, d))],
        out_specs=row(d),
        out_shape=jax.ShapeDtypeStruct((nb, s, d), F32),
        compiler_params=pltpu.CompilerParams(
            dimension_semantics=("arbitrary", "arbitrary"), vmem_limit_bytes=VMEM_LIMIT_BYTES),
        name="out_proj",
    )(x3, a, b, ga, gb, woa, wob, wo, g)


def kernel(x, meta_tokens, rel_bias, pre_norm_g, w_in, b_in, lambda_q1, lambda_k1, lambda_q2, lambda_k2,
           subln_g, sink, w_out_a, w_out_b, w_out, post_norm_g):
    nb, s, d = x.shape
    tm = min(512, s)
    tq = min(512, s)
    tk = min(256, s)
    assert s % tm == 0 and s % tq == 0 and tq % tk == 0 and s % _BQ == 0

    w = w_in[0].astype(BF16)
    b = b_in[0][None, :]
    g_pre = pre_norm_g[0][None, :]
    lam = (jnp.exp(jnp.sum(lambda_q1[0].astype(F32) * lambda_k1[0].astype(F32)))
           - jnp.exp(jnp.sum(lambda_q2[0].astype(F32) * lambda_k2[0].astype(F32))) + LAM_INIT).reshape(1)

    qa, ka, vat, za, qb, kb, vb, zb, ga, gb = _project(x, g_pre, w, b, tm=tm, tkv=tk)
    meta = _project(meta_tokens[None], g_pre, w, b, tm=N_META, tkv=N_META)
    ka_m, vat_m, kb_m, vb_m = meta[1], meta[2], meta[5], meta[6]

    ratio = tq // tk
    nv = ratio + 4
    off = (jnp.arange(nv, dtype=jnp.int32) - 2)[:, None, None] * tk
    rel_a = off + jnp.arange(tk, dtype=jnp.int32)[None, :, None] - jnp.arange(tq, dtype=jnp.int32)[None, None, :]
    bias_a = _bias_tiles(rel_bias, _t5_bucket(rel_a).reshape(nv * tk, tq), head0=0, n_heads=A_HEADS,
                         tr=tk).reshape(A_HEADS, nv, tk, tq)
    rel_m = jnp.arange(N_META, dtype=jnp.int32)[:, None] - (jnp.arange(s, dtype=jnp.int32)[None, :] + N_META)
    bias_m = _bias_tiles(rel_bias, _t5_bucket(rel_m), head0=0, n_heads=A_HEADS, tr=N_META)
    rel_b = (jnp.arange(_BK, dtype=jnp.int32)[None, :] - WINDOW) - jnp.arange(_BQ, dtype=jnp.int32)[:, None]
    idx_b = jnp.where(jnp.abs(rel_b) <= WINDOW, _t5_bucket(rel_b), N_BUCKETS)
    bias_b = _bias_tiles(rel_bias, idx_b, head0=A_HEADS, n_heads=B_HEADS, tr=_BQ)

    oa = _attn_a(lam, qa, ka, vat, ka_m, vat_m, bias_a, bias_m, za, subln_g[0][None, :], tq=tq, tk=tk)

    def extend(tok, m):
        return jnp.concatenate([
            jnp.zeros((nb, _PAD_FRONT, B_KV_WIDTH), BF16),
            jnp.broadcast_to(m, (nb, N_META, B_KV_WIDTH)), tok,
            jnp.zeros((nb, WINDOW, B_KV_WIDTH), BF16)], axis=1)

    ob = _attn_b(sink[0], qb, extend(kb, kb_m), extend(vb, vb_m), bias_b, zb)

    return _output(x, oa, ob, ga, gb, w_out_a[0].astype(BF16), w_out_b[0].astype(BF16),
                   w_out[0].astype(BF16), post_norm_g[0][None, :], tm=tm)
```

```python
import functools
import math

import jax
import jax.numpy as jnp
from jax import lax
from jax.experimental import pallas as pl
from jax.experimental.pallas import tpu as pltpu

N_META = 16
WINDOW = 128
A_HEADS = 4
A_QK_DIM = 64
A_V_DIM = 2 * A_QK_DIM
A_WIDTH = A_HEADS * A_V_DIM
B_HEADS = 8
B_KV_HEADS = 2
B_HEAD_DIM = 64
B_WIDTH = B_HEADS * B_HEAD_DIM
B_KV_WIDTH = B_KV_HEADS * B_HEAD_DIM
N_BUCKETS = 32
MAX_DISTANCE = 128
EPS = 1e-6
NEG = -1e30
LOG2E = math.log2(math.e)
LAM_INIT = 0.8 - 0.6 * math.exp(-0.3 * 0)

LANES = 128
SUM_ROWS = 16
VMEM_LIMIT_BYTES = 56 * 1024 * 1024

_COLS = (A_WIDTH, A_WIDTH, A_WIDTH, A_WIDTH, B_WIDTH, B_KV_WIDTH, B_KV_WIDTH, B_WIDTH)
BF16 = jnp.bfloat16
F32 = jnp.float32


def _t5_bucket(rel):
    half = N_BUCKETS // 2
    max_exact = half // 2
    ret = jnp.where(rel > 0, half, 0)
    n = jnp.abs(rel)
    nf = jnp.maximum(n, 1).astype(F32)
    large = max_exact + (jnp.log(nf / max_exact) / math.log(MAX_DISTANCE / max_exact)
                         * (half - max_exact)).astype(jnp.int32)
    large = jnp.minimum(large, half - 1)
    return ret + jnp.where(n < max_exact, n, large)


def _proj_kernel(x_ref, g_ref, w_ref, b_ref,
                 qa_ref, ka_ref, vat_ref, za_ref, qb_ref, kb_ref, vb_ref, zb_ref, ga_ref, gb_ref,
                 *, d_model, tkv):
    x = x_ref[...]
    hn = (x * lax.rsqrt(jnp.mean(x * x, axis=-1, keepdims=True) + EPS) * g_ref[...]).astype(BF16)

    def seg(c0, n):
        return jnp.dot(hn, w_ref[:, c0:c0 + n], preferred_element_type=F32) + b_ref[:, c0:c0 + n]

    qscale = (A_QK_DIM ** -0.5) * LOG2E
    c = 0
    qa_ref[...] = (seg(c, A_WIDTH) * qscale).astype(BF16); c += A_WIDTH
    ka_ref[...] = seg(c, A_WIDTH).astype(BF16); c += A_WIDTH
    va = seg(c, A_WIDTH); c += A_WIDTH
    for t in range(vat_ref.shape[0]):
        vat_ref[t] = va[t * tkv:(t + 1) * tkv, :].T.astype(BF16)
    za_ref[...] = seg(c, A_WIDTH).astype(BF16); c += A_WIDTH
    qb_ref[...] = (seg(c, B_WIDTH) * qscale).astype(BF16); c += B_WIDTH
    kb_ref[...] = seg(c, B_KV_WIDTH).astype(BF16); c += B_KV_WIDTH
    vb_ref[...] = seg(c, B_KV_WIDTH).astype(BF16); c += B_KV_WIDTH
    zb_ref[...] = seg(c, B_WIDTH).astype(BF16); c += B_WIDTH
    ga_ref[...] = seg(c, d_model).astype(BF16); c += d_model
    gb_ref[...] = seg(c, d_model).astype(BF16)


def _project(x3, g, w, b, *, tm, tkv):
    nb, s, d = x3.shape
    d_in = w.shape[1]
    widths = _COLS + (d, d)
    nt = s // tm
    row = lambda n: pl.BlockSpec((None, tm, n), lambda bi, i: (bi, i, 0))
    const = lambda shape: pl.BlockSpec(shape, lambda bi, i: (0, 0), pipeline_mode=pl.Buffered(1))
    out_specs, out_shapes = [], []
    for idx, n in enumerate(widths):
        if idx == 2:
            out_specs.append(pl.BlockSpec((None, tm // tkv, n, tkv), lambda bi, i: (bi, i, 0, 0)))
            out_shapes.append(jax.ShapeDtypeStruct((nb, s // tkv, n, tkv), BF16))
        else:
            out_specs.append(row(n))
            out_shapes.append(jax.ShapeDtypeStruct((nb, s, n), BF16))
    return pl.pallas_call(
        functools.partial(_proj_kernel, d_model=d, tkv=tkv),
        grid=(nb, nt),
        in_specs=[row(d), const((1, d)), const((d, d_in)), const((1, d_in))],
        out_specs=out_specs,
        out_shape=out_shapes,
        compiler_params=pltpu.CompilerParams(
            dimension_semantics=("arbitrary", "arbitrary"), vmem_limit_bytes=VMEM_LIMIT_BYTES),
        name="proj",
    )(x3, g, w, b)


def _bias_kernel(tab_ref, idx_ref, o_ref, *, head0):
    h = pl.program_id(0) + head0
    idx = idx_ref[...]
    acc = jnp.full(idx.shape, NEG, F32)
    for bkt in range(N_BUCKETS):
        acc = jnp.where(idx == bkt, tab_ref[bkt, h] * LOG2E, acc)
    o_ref[...] = acc


def _bias_tiles(tab, idx, *, head0, n_heads, tr):
    r, c = idx.shape
    return pl.pallas_call(
        functools.partial(_bias_kernel, head0=head0),
        grid=(n_heads, r // tr),
        in_specs=[pl.BlockSpec(memory_space=pltpu.SMEM),
                  pl.BlockSpec((tr, c), lambda h, i: (i, 0))],
        out_specs=pl.BlockSpec((None, tr, c), lambda h, i: (h, i, 0)),
        out_shape=jax.ShapeDtypeStruct((n_heads, r, c), F32),
        compiler_params=pltpu.CompilerParams(dimension_semantics=("arbitrary", "arbitrary")),
        name="bias_tiles",
    )(tab, idx)


def _attn_a_kernel(lam_ref, q_ref, k_ref, vt_ref, km_ref, vmt_ref, bias_ref, bm_ref, z_ref, g_ref,
                   o_ref, qt_ref, m_ref, acc_ref, s_ref, cm_ref, *, tq, tk, nk, ratio, nv, tb):
    i = pl.program_id(2)
    q = q_ref[...].astype(F32)
    lane = lax.broadcasted_iota(jnp.int32, q.shape, 1)
    qt_ref[:, :tq] = jnp.where(lane < A_QK_DIM, q, 0.0).T.astype(BF16)
    qt_ref[:, tq:] = jnp.where(lane >= A_QK_DIM, q, 0.0).T.astype(BF16)
    qt = qt_ref[...]

    def both(fn):
        return jnp.concatenate([fn(slice(0, tq)), fn(slice(tq, 2 * tq))], axis=1)

    def pv_and_sum(vt, pb):
        vt1 = jnp.concatenate([vt, jnp.ones((SUM_ROWS, vt.shape[1]), BF16)], axis=0)
        return both(lambda c: jnp.dot(vt1, pb[:, c], preferred_element_type=F32))

    bm = bm_ref[...]
    s = jnp.dot(km_ref[...], qt, preferred_element_type=F32) + jnp.concatenate([bm, bm], axis=1)
    m0 = jnp.max(s, axis=0, keepdims=True)
    m_ref[...] = m0
    acc_ref[...] = pv_and_sum(vmt_ref[...], jnp.exp2(s - m0).astype(BF16))

    def produce(j, slot):
        ks = pl.multiple_of(j * tk, tk)
        s = jnp.dot(k_ref[pl.ds(ks, tk), :], qt_ref[...], preferred_element_type=F32)
        s_ref[slot] = s
        cm_ref[slot] = jnp.max(s, axis=0, keepdims=True)

    def update(j, s, shift, cmax):
        m_old = m_ref[...]
        m_new = jnp.maximum(m_old, cmax)
        alpha = jnp.exp2(m_old - m_new)
        pb = jnp.exp2(s - (m_new - shift)).astype(BF16)
        acc_ref[...] = alpha * acc_ref[...] + pv_and_sum(vt_ref[j], pb)
        m_ref[...] = m_new

    def consume_far(const_row):
        def fn(j, slot):
            update(j, s_ref[slot], const_row, cm_ref[slot] + const_row)
        return fn

    def consume_near(j, slot):
        bt = bias_ref[jnp.clip(j - i * ratio + 2, 0, nv - 1)]
        s = s_ref[slot] + jnp.concatenate([bt, bt], axis=1)
        update(j, s, 0.0, jnp.max(s, axis=0, keepdims=True))

    def group(consume):
        def body(g, carry):
            j0 = tb * g
            for u in range(tb):
                nxt = j0 + u + 1
                if u == tb - 1:
                    nxt = jnp.minimum(nxt, nk - 1)
                produce(nxt, (u + 1) % 2)
                consume(j0 + u, u % 2)
            return carry
        return body

    def far_const(v):
        row = bias_ref[v, 0:1, :]
        return jnp.concatenate([row, row], axis=1)

    n_groups = nk // tb
    g_lo = jnp.maximum(i * ratio - 1, 0) // tb
    g_hi = jnp.minimum((i * ratio + ratio) // tb + 1, n_groups)
    produce(0, 0)
    lax.fori_loop(0, g_lo, group(consume_far(far_const(0))), 0)
    lax.fori_loop(g_lo, g_hi, group(consume_near), 0)
    lax.fori_loop(g_hi, n_groups, group(consume_far(far_const(nv - 1))), 0)

    acc = acc_ref[...]
    inv = 1.0 / acc[A_V_DIM:A_V_DIM + 1, :]
    lam = lam_ref[0]
    ot = acc[:A_V_DIM, :tq] * inv[:, :tq] - lam * (acc[:A_V_DIM, tq:] * inv[:, tq:])
    o = ot.T
    y = o * lax.rsqrt(jnp.mean(o * o, axis=-1, keepdims=True) + EPS) * g_ref[...] * (1.0 - LAM_INIT)
    z = z_ref[...].astype(F32)
    o_ref[...] = (y * (z * jax.nn.sigmoid(z))).astype(BF16)


def _attn_a(lam, qa, ka, vat, ka_m, vat_m, bias, bias_m, za, subln_g, *, tq, tk):
    nb, s, _ = qa.shape
    nk = s // tk
    ratio = tq // tk
    nv = bias.shape[1]
    tb = 4 if nk % 4 == 0 else 2
    assert nk % tb == 0
    head = lambda n: pl.BlockSpec((None, n, A_V_DIM), lambda b, h, i: (b, i, h))
    return pl.pallas_call(
        functools.partial(_attn_a_kernel, tq=tq, tk=tk, nk=nk, ratio=ratio, nv=nv, tb=tb),
        grid=(nb, A_HEADS, s // tq),
        in_specs=[
            pl.BlockSpec(memory_space=pltpu.SMEM),
            head(tq),
            pl.BlockSpec((None, s, A_V_DIM), lambda b, h, i: (b, 0, h)),
            pl.BlockSpec((None, nk, A_V_DIM, tk), lambda b, h, i: (b, 0, h, 0)),
            pl.BlockSpec((None, N_META, A_V_DIM), lambda b, h, i: (0, 0, h)),
            pl.BlockSpec((None, None, A_V_DIM, N_META), lambda b, h, i: (0, 0, h, 0)),
            pl.BlockSpec((None, nv, tk, tq), lambda b, h, i: (h, 0, 0, 0)),
            pl.BlockSpec((None, N_META, tq), lambda b, h, i: (h, 0, i)),
            head(tq),
            pl.BlockSpec((1, A_V_DIM), lambda b, h, i: (0, 0)),
        ],
        out_specs=head(tq),
        out_shape=jax.ShapeDtypeStruct((nb, s, A_WIDTH), BF16),
        scratch_shapes=[
            pltpu.VMEM((A_V_DIM, 2 * tq), BF16),
            pltpu.VMEM((1, 2 * tq), F32),
            pltpu.VMEM((A_V_DIM + SUM_ROWS, 2 * tq), F32),
            pltpu.VMEM((2, tk, 2 * tq), F32),
            pltpu.VMEM((2, 1, 2 * tq), F32),
        ],
        compiler_params=pltpu.CompilerParams(
            dimension_semantics=("arbitrary", "arbitrary", "arbitrary"),
            vmem_limit_bytes=VMEM_LIMIT_BYTES),
        name="attn_a",
    )(lam, qa, ka, vat, ka_m, vat_m, bias, bias_m, za, subln_g)


_BQ = 256
_BK = _BQ + 2 * WINDOW
_PAD_FRONT = WINDOW - N_META


def _attn_b_kernel(sink_ref, q_ref, k0, k1, k2, k3, v0, v1, v2, v3, bias_ref, z_ref, o_ref, *, s_len):
    i = pl.program_id(1)
    k = jnp.concatenate([k0[...], k1[...], k2[...], k3[...]], axis=0).astype(F32)
    v = jnp.concatenate([v0[...], v1[...], v2[...], v3[...]], axis=0).astype(F32)
    lane = lax.broadcasted_iota(jnp.int32, k.shape, 1)
    lo = lane < B_HEAD_DIM

    def placed(t):
        tr = pltpu.roll(t, B_HEAD_DIM, axis=1)
        low = [jnp.where(lo, t, 0.0).astype(BF16), jnp.where(lo, tr, 0.0).astype(BF16)]
        high = [jnp.where(lo, 0.0, tr).astype(BF16), jnp.where(lo, 0.0, t).astype(BF16)]
        return low, high

    k_low, k_high = placed(k)
    v_low, v_high = placed(v)
    row_ext = i * _BQ + lax.broadcasted_iota(jnp.int32, (1, _BK), 1)
    vmask = jnp.where((row_ext >= _PAD_FRONT) & (row_ext < WINDOW + s_len), 0.0, NEG)

    for slab in range(B_HEADS // 2):
        g = slab // (B_HEADS // (2 * B_KV_HEADS))
        cols = slice(slab * LANES, (slab + 1) * LANES)
        qs = q_ref[:, cols]
        out = None
        for e in range(2):
            h = 2 * slab + e
            kx = (k_low if e == 0 else k_high)[g]
            vx = (v_low if e == 0 else v_high)[g]
            sc = lax.dot_general(qs, kx, (((1,), (1,)), ((), ())), preferred_element_type=F32)
            sc = sc + bias_ref[h] + vmask
            sk = sink_ref[h] * LOG2E
            m = jnp.maximum(jnp.max(sc, axis=-1, keepdims=True), sk)
            p = jnp.exp2(sc - m)
            l = jnp.sum(p, axis=-1, keepdims=True) + jnp.exp2(sk - m)
            pn = (p * (1.0 / l)).astype(BF16)
            oe = jnp.dot(pn, vx, preferred_element_type=F32)
            out = oe if out is None else out + oe
        z = z_ref[:, cols].astype(F32)
        o_ref[:, cols] = (out * (z * jax.nn.sigmoid(z))).astype(BF16)


def _attn_b(sink, qb, kb_ext, vb_ext, bias, zb):
    nb, s, _ = qb.shape
    row = pl.BlockSpec((None, _BQ, B_WIDTH), lambda b, i: (b, i, 0))
    nsub = _BK // WINDOW
    kv = [pl.BlockSpec((None, WINDOW, B_KV_WIDTH), (lambda b, i, t=t: (b, (_BQ // WINDOW) * i + t, 0)))
          for t in range(nsub)]
    return pl.pallas_call(
        functools.partial(_attn_b_kernel, s_len=s),
        grid=(nb, s // _BQ),
        in_specs=[pl.BlockSpec(memory_space=pltpu.SMEM), row] + kv + kv
                 + [pl.BlockSpec((B_HEADS, _BQ, _BK), lambda b, i: (0, 0, 0)), row],
        out_specs=row,
        out_shape=jax.ShapeDtypeStruct((nb, s, B_WIDTH), BF16),
        compiler_params=pltpu.CompilerParams(
            dimension_semantics=("arbitrary", "arbitrary"), vmem_limit_bytes=VMEM_LIMIT_BYTES),
        name="attn_b",
    )(sink, qb, *([kb_ext] * nsub), *([vb_ext] * nsub), bias, zb)


def _out_kernel(x_ref, a_ref, b_ref, ga_ref, gb_ref, woa_ref, wob_ref, wo_ref, g_ref, o_ref):
    ya = jnp.dot(a_ref[...], woa_ref[...], preferred_element_type=F32)
    yb = jnp.dot(b_ref[...], wob_ref[...], preferred_element_type=F32)
    mixed = (jax.nn.sigmoid(ga_ref[...].astype(F32)) * ya
             + jax.nn.sigmoid(gb_ref[...].astype(F32)) * yb)
    y = jnp.dot(mixed.astype(BF16), wo_ref[...], preferred_element_type=F32)
    y = y * lax.rsqrt(jnp.mean(y * y, axis=-1, keepdims=True) + EPS) * g_ref[...]
    o_ref[...] = x_ref[...] + y


def _output(x3, a, b, ga, gb, woa, wob, wo, g, *, tm):
    nb, s, d = x3.shape
    row = lambda n: pl.BlockSpec((None, tm, n), lambda bi, i: (bi, i, 0))
    const = lambda shape: pl.BlockSpec(shape, lambda bi, i: (0, 0), pipeline_mode=pl.Buffered(1))
    return pl.pallas_call(
        _out_kernel,
        grid=(nb, s // tm),
        in_specs=[row(d), row(A_WIDTH), row(B_WIDTH), row(d), row(d),
                  const((A_WIDTH, d)), const((B_WIDTH, d)), const((d, d)), const((1, d))],
        out_specs=row(d),
        out_shape=jax.ShapeDtypeStruct((nb, s, d), F32),
        compiler_params=pltpu.CompilerParams(
            dimension_semantics=("arbitrary", "arbitrary"), vmem_limit_bytes=VMEM_LIMIT_BYTES),
        name="out_proj",
    )(x3, a, b, ga, gb, woa, wob, wo, g)


def kernel(x, meta_tokens, rel_bias, pre_norm_g, w_in, b_in, lambda_q1, lambda_k1, lambda_q2, lambda_k2,
           subln_g, sink, w_out_a, w_out_b, w_out, post_norm_g):
    nb, s, d = x.shape
    tm = min(512, s)
    tq = min(512, s)
    tk = min(256, s)
    assert s % tm == 0 and s % tq == 0 and tq % tk == 0 and s % _BQ == 0

    w = w_in[0].astype(BF16)
    b = b_in[0][None, :]
    g_pre = pre_norm_g[0][None, :]
    lam = (jnp.exp(jnp.sum(lambda_q1[0].astype(F32) * lambda_k1[0].astype(F32)))
           - jnp.exp(jnp.sum(lambda_q2[0].astype(F32) * lambda_k2[0].astype(F32))) + LAM_INIT).reshape(1)

    qa, ka, vat, za, qb, kb, vb, zb, ga, gb = _project(x, g_pre, w, b, tm=tm, tkv=tk)
    meta = _project(meta_tokens[None], g_pre, w, b, tm=N_META, tkv=N_META)
    ka_m, vat_m, kb_m, vb_m = meta[1], meta[2], meta[5], meta[6]

    ratio = tq // tk
    nv = ratio + 4
    off = (jnp.arange(nv, dtype=jnp.int32) - 2)[:, None, None] * tk
    rel_a = off + jnp.arange(tk, dtype=jnp.int32)[None, :, None] - jnp.arange(tq, dtype=jnp.int32)[None, None, :]
    bias_a = _bias_tiles(rel_bias, _t5_bucket(rel_a).reshape(nv * tk, tq), head0=0, n_heads=A_HEADS,
                         tr=tk).reshape(A_HEADS, nv, tk, tq)
    rel_m = jnp.arange(N_META, dtype=jnp.int32)[:, None] - (jnp.arange(s, dtype=jnp.int32)[None, :] + N_META)
    bias_m = _bias_tiles(rel_bias, _t5_bucket(rel_m), head0=0, n_heads=A_HEADS, tr=N_META)
    rel_b = (jnp.arange(_BK, dtype=jnp.int32)[None, :] - WINDOW) - jnp.arange(_BQ, dtype=jnp.int32)[:, None]
    idx_b = jnp.where(jnp.abs(rel_b) <= WINDOW, _t5_bucket(rel_b), N_BUCKETS)
    bias_b = _bias_tiles(rel_bias, idx_b, head0=A_HEADS, n_heads=B_HEADS, tr=_BQ)

    oa = _attn_a(lam, qa, ka, vat, ka_m, vat_m, bias_a, bias_m, za, subln_g[0][None, :], tq=tq, tk=tk)

    def extend(tok, m):
        return jnp.concatenate([
            jnp.zeros((nb, _PAD_FRONT, B_KV_WIDTH), BF16),
            jnp.broadcast_to(m, (nb, N_META, B_KV_WIDTH)), tok,
            jnp.zeros((nb, WINDOW, B_KV_WIDTH), BF16)], axis=1)

    ob = _attn_b(sink[0], qb, extend(kb, kb_m), extend(vb, vb_m), bias_b, zb)

    return _output(x, oa, ob, ga, gb, w_out_a[0].astype(BF16), w_out_b[0].astype(BF16),
                   w_out[0].astype(BF16), post_norm_g[0][None, :], tm=tm)
```

```python
import functools
import math

import jax
import jax.numpy as jnp
from jax import lax
from jax.experimental import pallas as pl
from jax.experimental.pallas import tpu as pltpu

N_META = 16
WINDOW = 128
A_HEADS = 4
A_QK_DIM = 64
A_V_DIM = 2 * A_QK_DIM
A_WIDTH = A_HEADS * A_V_DIM
B_HEADS = 8
B_KV_HEADS = 2
B_HEAD_DIM = 64
B_WIDTH = B_HEADS * B_HEAD_DIM
B_KV_WIDTH = B_KV_HEADS * B_HEAD_DIM
N_BUCKETS = 32
MAX_DISTANCE = 128
EPS = 1e-6
NEG = -1e30
LOG2E = math.log2(math.e)
LAM_INIT = 0.8 - 0.6 * math.exp(-0.3 * 0)

LANES = 128
OVERFLOW_GUARD = 2.0 ** 100
SUM_ROWS = 16
VMEM_LIMIT_BYTES = 56 * 1024 * 1024

_COLS = (A_WIDTH, A_WIDTH, A_WIDTH, A_WIDTH, B_WIDTH, B_KV_WIDTH, B_KV_WIDTH, B_WIDTH)
BF16 = jnp.bfloat16
F32 = jnp.float32


def _t5_bucket(rel):
    half = N_BUCKETS // 2
    max_exact = half // 2
    ret = jnp.where(rel > 0, half, 0)
    n = jnp.abs(rel)
    nf = jnp.maximum(n, 1).astype(F32)
    large = max_exact + (jnp.log(nf / max_exact) / math.log(MAX_DISTANCE / max_exact)
                         * (half - max_exact)).astype(jnp.int32)
    large = jnp.minimum(large, half - 1)
    return ret + jnp.where(n < max_exact, n, large)


def _proj_kernel(x_ref, g_ref, w_ref, b_ref,
                 qa_ref, ka_ref, vat_ref, za_ref, qb_ref, kb_ref, vb_ref, zb_ref, ga_ref, gb_ref,
                 *, d_model, tkv):
    x = x_ref[...]
    hn = (x * lax.rsqrt(jnp.mean(x * x, axis=-1, keepdims=True) + EPS) * g_ref[...]).astype(BF16)

    def seg(c0, n):
        return jnp.dot(hn, w_ref[:, c0:c0 + n], preferred_element_type=F32) + b_ref[:, c0:c0 + n]

    qscale = (A_QK_DIM ** -0.5) * LOG2E
    c = 0
    qa_ref[...] = (seg(c, A_WIDTH) * qscale).astype(BF16); c += A_WIDTH
    ka_ref[...] = seg(c, A_WIDTH).astype(BF16); c += A_WIDTH
    va = seg(c, A_WIDTH); c += A_WIDTH
    for t in range(vat_ref.shape[0]):
        vat_ref[t] = va[t * tkv:(t + 1) * tkv, :].T.astype(BF16)
    za_ref[...] = seg(c, A_WIDTH).astype(BF16); c += A_WIDTH
    qb_ref[...] = (seg(c, B_WIDTH) * qscale).astype(BF16); c += B_WIDTH
    kb_ref[...] = seg(c, B_KV_WIDTH).astype(BF16); c += B_KV_WIDTH
    vb_ref[...] = seg(c, B_KV_WIDTH).astype(BF16); c += B_KV_WIDTH
    zb_ref[...] = seg(c, B_WIDTH).astype(BF16); c += B_WIDTH
    ga_ref[...] = seg(c, d_model).astype(BF16); c += d_model
    gb_ref[...] = seg(c, d_model).astype(BF16)


def _project(x3, g, w, b, *, tm, tkv):
    nb, s, d = x3.shape
    d_in = w.shape[1]
    widths = _COLS + (d, d)
    nt = s // tm
    row = lambda n: pl.BlockSpec((None, tm, n), lambda bi, i: (bi, i, 0))
    const = lambda shape: pl.BlockSpec(shape, lambda bi, i: (0, 0), pipeline_mode=pl.Buffered(1))
    out_specs, out_shapes = [], []
    for idx, n in enumerate(widths):
        if idx == 2:
            out_specs.append(pl.BlockSpec((None, tm // tkv, n, tkv), lambda bi, i: (bi, i, 0, 0)))
            out_shapes.append(jax.ShapeDtypeStruct((nb, s // tkv, n, tkv), BF16))
        else:
            out_specs.append(row(n))
            out_shapes.append(jax.ShapeDtypeStruct((nb, s, n), BF16))
    return pl.pallas_call(
        functools.partial(_proj_kernel, d_model=d, tkv=tkv),
        grid=(nb, nt),
        in_specs=[row(d), const((1, d)), const((d, d_in)), const((1, d_in))],
        out_specs=out_specs,
        out_shape=out_shapes,
        compiler_params=pltpu.CompilerParams(
            dimension_semantics=("arbitrary", "arbitrary"), vmem_limit_bytes=VMEM_LIMIT_BYTES),
        name="proj",
    )(x3, g, w, b)


def _bias_kernel(tab_ref, idx_ref, o_ref, *, head0):
    h = pl.program_id(0) + head0
    idx = idx_ref[...]
    acc = jnp.full(idx.shape, NEG, F32)
    for bkt in range(N_BUCKETS):
        acc = jnp.where(idx == bkt, tab_ref[bkt, h] * LOG2E, acc)
    o_ref[...] = acc


def _bias_tiles(tab, idx, *, head0, n_heads, tr):
    r, c = idx.shape
    return pl.pallas_call(
        functools.partial(_bias_kernel, head0=head0),
        grid=(n_heads, r // tr),
        in_specs=[pl.BlockSpec(memory_space=pltpu.SMEM),
                  pl.BlockSpec((tr, c), lambda h, i: (i, 0))],
        out_specs=pl.BlockSpec((None, tr, c), lambda h, i: (h, i, 0)),
        out_shape=jax.ShapeDtypeStruct((n_heads, r, c), F32),
        compiler_params=pltpu.CompilerParams(dimension_semantics=("arbitrary", "arbitrary")),
        name="bias_tiles",
    )(tab, idx)


def _attn_a_kernel(lam_ref, q_ref, k_ref, vt_ref, km_ref, vmt_ref, bias_ref, bm_ref, z_ref, g_ref,
                   o_ref, qt_ref, m_ref, acc_ref, s_ref, *, tq, tk, nk, ratio, nv, tb):
    i = pl.program_id(2)
    q = q_ref[...].astype(F32)
    lane = lax.broadcasted_iota(jnp.int32, q.shape, 1)
    qt_ref[:, :tq] = jnp.where(lane < A_QK_DIM, q, 0.0).T.astype(BF16)
    qt_ref[:, tq:] = jnp.where(lane >= A_QK_DIM, q, 0.0).T.astype(BF16)
    qt = qt_ref[...]

    def both(fn):
        return jnp.concatenate([fn(slice(0, tq)), fn(slice(tq, 2 * tq))], axis=1)

    def pv_and_sum(vt, pb):
        vt1 = jnp.concatenate([vt, jnp.ones((SUM_ROWS, vt.shape[1]), BF16)], axis=0)
        return both(lambda c: jnp.dot(vt1, pb[:, c], preferred_element_type=F32))

    def raw_scores(j):
        ks = pl.multiple_of(j * tk, tk)
        return jnp.dot(k_ref[pl.ds(ks, tk), :], qt, preferred_element_type=F32)

    def bias2(j):
        bt = bias_ref[jnp.clip(j - i * ratio + 2, 0, nv - 1)]
        return jnp.concatenate([bt, bt], axis=1)

    def tile_scores(j):
        return raw_scores(j) + bias2(j)

    bm = bm_ref[...]
    s_meta = jnp.dot(km_ref[...], qt, preferred_element_type=F32) + jnp.concatenate([bm, bm], axis=1)
    m0 = jnp.max(s_meta, axis=0, keepdims=True)

    def init_from_meta():
        acc_ref[...] = pv_and_sum(vmt_ref[...], jnp.exp2(s_meta - m0).astype(BF16))

    init_from_meta()
    s_ref[0] = raw_scores(0)

    def group(g, carry):
        pv = None
        for u in range(tb):
            j = g * tb + u
            nxt = j + 1 if u < tb - 1 else jnp.minimum(j + 1, nk - 1)
            s_ref[(u + 1) % 2] = raw_scores(nxt)
            p = jnp.exp2(s_ref[u % 2] + bias2(j) - m0)
            d = pv_and_sum(vt_ref[j], p.astype(BF16))
            pv = d if pv is None else pv + d
        acc_ref[...] += pv
        return carry

    lax.fori_loop(0, nk // tb, group, 0)

    overflow = jnp.max(jnp.where(jnp.abs(acc_ref[...]) < OVERFLOW_GUARD, 0.0, 1.0)) > 0.0

    @pl.when(overflow)
    def _():
        init_from_meta()
        m_ref[...] = m0

        def step(j, carry):
            s = tile_scores(j)
            m_old = m_ref[...]
            m_new = jnp.maximum(m_old, jnp.max(s, axis=0, keepdims=True))
            alpha = jnp.exp2(m_old - m_new)
            acc_ref[...] = alpha * acc_ref[...] + pv_and_sum(vt_ref[j], jnp.exp2(s - m_new).astype(BF16))
            m_ref[...] = m_new
            return carry

        lax.fori_loop(0, nk, step, 0)

    acc = acc_ref[...]
    inv = 1.0 / acc[A_V_DIM:A_V_DIM + 1, :]
    lam = lam_ref[0]
    ot = acc[:A_V_DIM, :tq] * inv[:, :tq] - lam * (acc[:A_V_DIM, tq:] * inv[:, tq:])
    o = ot.T
    y = o * lax.rsqrt(jnp.mean(o * o, axis=-1, keepdims=True) + EPS) * g_ref[...] * (1.0 - LAM_INIT)
    z = z_ref[...].astype(F32)
    o_ref[...] = (y * (z * jax.nn.sigmoid(z))).astype(BF16)


def _attn_a(lam, qa, ka, vat, ka_m, vat_m, bias, bias_m, za, subln_g, *, tq, tk):
    nb, s, _ = qa.shape
    nk = s // tk
    ratio = tq // tk
    nv = bias.shape[1]
    tb = 8 if nk % 8 == 0 else (4 if nk % 4 == 0 else 1)
    assert nk % tb == 0
    head = lambda n: pl.BlockSpec((None, n, A_V_DIM), lambda b, h, i: (b, i, h))
    return pl.pallas_call(
        functools.partial(_attn_a_kernel, tq=tq, tk=tk, nk=nk, ratio=ratio, nv=nv, tb=tb),
        grid=(nb, A_HEADS, s // tq),
        in_specs=[
            pl.BlockSpec(memory_space=pltpu.SMEM),
            head(tq),
            pl.BlockSpec((None, s, A_V_DIM), lambda b, h, i: (b, 0, h)),
            pl.BlockSpec((None, nk, A_V_DIM, tk), lambda b, h, i: (b, 0, h, 0)),
            pl.BlockSpec((None, N_META, A_V_DIM), lambda b, h, i: (0, 0, h)),
            pl.BlockSpec((None, None, A_V_DIM, N_META), lambda b, h, i: (0, 0, h, 0)),
            pl.BlockSpec((None, nv, tk, tq), lambda b, h, i: (h, 0, 0, 0)),
            pl.BlockSpec((None, N_META, tq), lambda b, h, i: (h, 0, i)),
            head(tq),
            pl.BlockSpec((1, A_V_DIM), lambda b, h, i: (0, 0)),
        ],
        out_specs=head(tq),
        out_shape=jax.ShapeDtypeStruct((nb, s, A_WIDTH), BF16),
        scratch_shapes=[
            pltpu.VMEM((A_V_DIM, 2 * tq), BF16),
            pltpu.VMEM((1, 2 * tq), F32),
            pltpu.VMEM((A_V_DIM + SUM_ROWS, 2 * tq), F32),
            pltpu.VMEM((2, tk, 2 * tq), F32),
        ],
        compiler_params=pltpu.CompilerParams(
            dimension_semantics=("arbitrary", "arbitrary", "arbitrary"),
            vmem_limit_bytes=VMEM_LIMIT_BYTES),
        name="attn_a",
    )(lam, qa, ka, vat, ka_m, vat_m, bias, bias_m, za, subln_g)


_BQ = 256
_BK = _BQ + 2 * WINDOW
_PAD_FRONT = WINDOW - N_META


def _attn_b_kernel(sink_ref, q_ref, k0, k1, k2, k3, v0, v1, v2, v3, bias_ref, z_ref, o_ref, *, s_len):
    i = pl.program_id(1)
    k = jnp.concatenate([k0[...], k1[...], k2[...], k3[...]], axis=0).astype(F32)
    v = jnp.concatenate([v0[...], v1[...], v2[...], v3[...]], axis=0).astype(F32)
    lane = lax.broadcasted_iota(jnp.int32, k.shape, 1)
    lo = lane < B_HEAD_DIM

    def placed(t):
        tr = pltpu.roll(t, B_HEAD_DIM, axis=1)
        low = [jnp.where(lo, t, 0.0).astype(BF16), jnp.where(lo, tr, 0.0).astype(BF16)]
        high = [jnp.where(lo, 0.0, tr).astype(BF16), jnp.where(lo, 0.0, t).astype(BF16)]
        return low, high

    k_low, k_high = placed(k)
    v_low, v_high = placed(v)
    row_ext = i * _BQ + lax.broadcasted_iota(jnp.int32, (1, _BK), 1)
    vmask = jnp.where((row_ext >= _PAD_FRONT) & (row_ext < WINDOW + s_len), 0.0, NEG)

    for slab in range(B_HEADS // 2):
        g = slab // (B_HEADS // (2 * B_KV_HEADS))
        cols = slice(slab * LANES, (slab + 1) * LANES)
        qs = q_ref[:, cols]
        out = None
        for e in range(2):
            h = 2 * slab + e
            kx = (k_low if e == 0 else k_high)[g]
            vx = (v_low if e == 0 else v_high)[g]
            sc = lax.dot_general(qs, kx, (((1,), (1,)), ((), ())), preferred_element_type=F32)
            sc = sc + bias_ref[h] + vmask
            sk = sink_ref[h] * LOG2E
            m = jnp.maximum(jnp.max(sc, axis=-1, keepdims=True), sk)
            p = jnp.exp2(sc - m)
            l = jnp.sum(p, axis=-1, keepdims=True) + jnp.exp2(sk - m)
            pn = (p * (1.0 / l)).astype(BF16)
            oe = jnp.dot(pn, vx, preferred_element_type=F32)
            out = oe if out is None else out + oe
        z = z_ref[:, cols].astype(F32)
        o_ref[:, cols] = (out * (z * jax.nn.sigmoid(z))).astype(BF16)


def _attn_b(sink, qb, kb_ext, vb_ext, bias, zb):
    nb, s, _ = qb.shape
    row = pl.BlockSpec((None, _BQ, B_WIDTH), lambda b, i: (b, i, 0))
    nsub = _BK // WINDOW
    kv = [pl.BlockSpec((None, WINDOW, B_KV_WIDTH), (lambda b, i, t=t: (b, (_BQ // WINDOW) * i + t, 0)))
          for t in range(nsub)]
    return pl.pallas_call(
        functools.partial(_attn_b_kernel, s_len=s),
        grid=(nb, s // _BQ),
        in_specs=[pl.BlockSpec(memory_space=pltpu.SMEM), row] + kv + kv
                 + [pl.BlockSpec((B_HEADS, _BQ, _BK), lambda b, i: (0, 0, 0)), row],
        out_specs=row,
        out_shape=jax.ShapeDtypeStruct((nb, s, B_WIDTH), BF16),
        compiler_params=pltpu.CompilerParams(
            dimension_semantics=("arbitrary", "arbitrary"), vmem_limit_bytes=VMEM_LIMIT_BYTES),
        name="attn_b",
    )(sink, qb, *([kb_ext] * nsub), *([vb_ext] * nsub), bias, zb)


def _out_kernel(x_ref, a_ref, b_ref, ga_ref, gb_ref, woa_ref, wob_ref, wo_ref, g_ref, o_ref):
    ya = jnp.dot(a_ref[...], woa_ref[...], preferred_element_type=F32)
    yb = jnp.dot(b_ref[...], wob_ref[...], preferred_element_type=F32)
    mixed = (jax.nn.sigmoid(ga_ref[...].astype(F32)) * ya
             + jax.nn.sigmoid(gb_ref[...].astype(F32)) * yb)
    y = jnp.dot(mixed.astype(BF16), wo_ref[...], preferred_element_type=F32)
    y = y * lax.rsqrt(jnp.mean(y * y, axis=-1, keepdims=True) + EPS) * g_ref[...]
    o_ref[...] = x_ref[...] + y


def _output(x3, a, b, ga, gb, woa, wob, wo, g, *, tm):
    nb, s, d = x3.shape
    row = lambda n: pl.BlockSpec((None, tm, n), lambda bi, i: (bi, i, 0))
    const = lambda shape: pl.BlockSpec(shape, lambda bi, i: (0, 0), pipeline_mode=pl.Buffered(1))
    return pl.pallas_call(
        _out_kernel,
        grid=(nb, s // tm),
        in_specs=[row(d), row(A_WIDTH), row(B_WIDTH), row(d), row(d),
                  const((A_WIDTH, d)), const((B_WIDTH, d)), const((d, d)), const((1, d))],
        out_specs=row(d),
        out_shape=jax.ShapeDtypeStruct((nb, s, d), F32),
        compiler_params=pltpu.CompilerParams(
            dimension_semantics=("arbitrary", "arbitrary"), vmem_limit_bytes=VMEM_LIMIT_BYTES),
        name="out_proj",
    )(x3, a, b, ga, gb, woa, wob, wo, g)


def kernel(x, meta_tokens, rel_bias, pre_norm_g, w_in, b_in, lambda_q1, lambda_k1, lambda_q2, lambda_k2,
           subln_g, sink, w_out_a, w_out_b, w_out, post_norm_g):
    nb, s, d = x.shape
    tm = min(512, s)
    tq = min(512, s)
    tk = min(256, s)
    assert s % tm == 0 and s % tq == 0 and tq % tk == 0 and s % _BQ == 0

    w = w_in[0].astype(BF16)
    b = b_in[0][None, :]
    g_pre = pre_norm_g[0][None, :]
    lam = (jnp.exp(jnp.sum(lambda_q1[0].astype(F32) * lambda_k1[0].astype(F32)))
           - jnp.exp(jnp.sum(lambda_q2[0].astype(F32) * lambda_k2[0].astype(F32))) + LAM_INIT).reshape(1)

    qa, ka, vat, za, qb, kb, vb, zb, ga, gb = _project(x, g_pre, w, b, tm=tm, tkv=tk)
    meta = _project(meta_tokens[None], g_pre, w, b, tm=N_META, tkv=N_META)
    ka_m, vat_m, kb_m, vb_m = meta[1], meta[2], meta[5], meta[6]

    ratio = tq // tk
    nv = ratio + 4
    off = (jnp.arange(nv, dtype=jnp.int32) - 2)[:, None, None] * tk
    rel_a = off + jnp.arange(tk, dtype=jnp.int32)[None, :, None] - jnp.arange(tq, dtype=jnp.int32)[None, None, :]
    bias_a = _bias_tiles(rel_bias, _t5_bucket(rel_a).reshape(nv * tk, tq), head0=0, n_heads=A_HEADS,
                         tr=tk).reshape(A_HEADS, nv, tk, tq)
    rel_m = jnp.arange(N_META, dtype=jnp.int32)[:, None] - (jnp.arange(s, dtype=jnp.int32)[None, :] + N_META)
    bias_m = _bias_tiles(rel_bias, _t5_bucket(rel_m), head0=0, n_heads=A_HEADS, tr=N_META)
    rel_b = (jnp.arange(_BK, dtype=jnp.int32)[None, :] - WINDOW) - jnp.arange(_BQ, dtype=jnp.int32)[:, None]
    idx_b = jnp.where(jnp.abs(rel_b) <= WINDOW, _t5_bucket(rel_b), N_BUCKETS)
    bias_b = _bias_tiles(rel_bias, idx_b, head0=A_HEADS, n_heads=B_HEADS, tr=_BQ)

    oa = _attn_a(lam, qa, ka, vat, ka_m, vat_m, bias_a, bias_m, za, subln_g[0][None, :], tq=tq, tk=tk)

    def extend(tok, m):
        return jnp.concatenate([
            jnp.zeros((nb, _PAD_FRONT, B_KV_WIDTH), BF16),
            jnp.broadcast_to(m, (nb, N_META, B_KV_WIDTH)), tok,
            jnp.zeros((nb, WINDOW, B_KV_WIDTH), BF16)], axis=1)

    ob = _attn_b(sink[0], qb, extend(kb, kb_m), extend(vb, vb_m), bias_b, zb)

    return _output(x, oa, ob, ga, gb, w_out_a[0].astype(BF16), w_out_b[0].astype(BF16),
                   w_out[0].astype(BF16), post_norm_g[0][None, :], tm=tm)
```

```python
import functools
import math

import jax
import jax.numpy as jnp
from jax import lax
from jax.experimental import pallas as pl
from jax.experimental.pallas import tpu as pltpu

N_META = 16
WINDOW = 128
A_HEADS = 4
A_QK_DIM = 64
A_V_DIM = 2 * A_QK_DIM
A_WIDTH = A_HEADS * A_V_DIM
B_HEADS = 8
B_KV_HEADS = 2
B_HEAD_DIM = 64
B_WIDTH = B_HEADS * B_HEAD_DIM
B_KV_WIDTH = B_KV_HEADS * B_HEAD_DIM
N_BUCKETS = 32
MAX_DISTANCE = 128
EPS = 1e-6
NEG = -1e30
LOG2E = math.log2(math.e)
LAM_INIT = 0.8 - 0.6 * math.exp(-0.3 * 0)

LANES = 128
OVERFLOW_GUARD = 2.0 ** 100
F32_FINITE = 3.0e38
SUM_ROWS = 16
VMEM_LIMIT_BYTES = 56 * 1024 * 1024

_COLS = (A_WIDTH, A_WIDTH, A_WIDTH, A_WIDTH, B_WIDTH, B_KV_WIDTH, B_KV_WIDTH, B_WIDTH)
BF16 = jnp.bfloat16
F32 = jnp.float32


def _t5_bucket(rel):
    half = N_BUCKETS // 2
    max_exact = half // 2
    ret = jnp.where(rel > 0, half, 0)
    n = jnp.abs(rel)
    nf = jnp.maximum(n, 1).astype(F32)
    large = max_exact + (jnp.log(nf / max_exact) / math.log(MAX_DISTANCE / max_exact)
                         * (half - max_exact)).astype(jnp.int32)
    large = jnp.minimum(large, half - 1)
    return ret + jnp.where(n < max_exact, n, large)


def _proj_kernel(x_ref, g_ref, w_ref, b_ref,
                 qat_ref, ka_ref, vat_ref, za_ref, qb_ref, kb_ref, vb_ref, zb_ref, ga_ref, gb_ref,
                 *, d_model, tq, tkv):
    x = x_ref[...]
    hn = (x * lax.rsqrt(jnp.mean(x * x, axis=-1, keepdims=True) + EPS) * g_ref[...]).astype(BF16)

    def seg(c0, n):
        return jnp.dot(hn, w_ref[:, c0:c0 + n], preferred_element_type=F32) + b_ref[:, c0:c0 + n]

    qscale = (A_QK_DIM ** -0.5) * LOG2E
    c = 0
    qa = seg(c, A_WIDTH) * qscale; c += A_WIDTH
    for t in range(qat_ref.shape[0]):
        qat_ref[t] = qa[t * tq:(t + 1) * tq, :].T.astype(BF16)
    ka_ref[...] = seg(c, A_WIDTH).astype(BF16); c += A_WIDTH
    va = seg(c, A_WIDTH); c += A_WIDTH
    for t in range(vat_ref.shape[0]):
        vat_ref[t] = va[t * tkv:(t + 1) * tkv, :].T.astype(BF16)
    za_ref[...] = seg(c, A_WIDTH).astype(BF16); c += A_WIDTH
    qb_ref[...] = (seg(c, B_WIDTH) * qscale).astype(BF16); c += B_WIDTH
    kb_ref[...] = seg(c, B_KV_WIDTH).astype(BF16); c += B_KV_WIDTH
    vb_ref[...] = seg(c, B_KV_WIDTH).astype(BF16); c += B_KV_WIDTH
    zb_ref[...] = seg(c, B_WIDTH).astype(BF16); c += B_WIDTH
    ga_ref[...] = seg(c, d_model).astype(BF16); c += d_model
    gb_ref[...] = seg(c, d_model).astype(BF16)


def _project(x3, g, w, b, *, tm, tq, tkv):
    nb, s, d = x3.shape
    d_in = w.shape[1]
    widths = _COLS + (d, d)
    nt = s // tm
    row = lambda n: pl.BlockSpec((None, tm, n), lambda bi, i: (bi, i, 0))
    const = lambda shape: pl.BlockSpec(shape, lambda bi, i: (0, 0), pipeline_mode=pl.Buffered(1))
    out_specs, out_shapes = [], []
    for idx, n in enumerate(widths):
        if idx in (0, 2):
            tt = tq if idx == 0 else tkv
            out_specs.append(pl.BlockSpec((None, tm // tt, n, tt), lambda bi, i: (bi, i, 0, 0)))
            out_shapes.append(jax.ShapeDtypeStruct((nb, s // tt, n, tt), BF16))
        else:
            out_specs.append(row(n))
            out_shapes.append(jax.ShapeDtypeStruct((nb, s, n), BF16))
    return pl.pallas_call(
        functools.partial(_proj_kernel, d_model=d, tq=tq, tkv=tkv),
        grid=(nb, nt),
        in_specs=[row(d), const((1, d)), const((d, d_in)), const((1, d_in))],
        out_specs=out_specs,
        out_shape=out_shapes,
        compiler_params=pltpu.CompilerParams(
            dimension_semantics=("arbitrary", "arbitrary"), vmem_limit_bytes=VMEM_LIMIT_BYTES),
        name="proj",
    )(x3, g, w, b)


def _bias_kernel(tab_ref, idx_ref, o_ref, *, head0):
    h = pl.program_id(0) + head0
    idx = idx_ref[...]
    acc = jnp.full(idx.shape, NEG, F32)
    for bkt in range(N_BUCKETS):
        acc = jnp.where(idx == bkt, tab_ref[bkt, h] * LOG2E, acc)
    o_ref[...] = acc


def _bias_tiles(tab, idx, *, head0, n_heads, tr):
    r, c = idx.shape
    return pl.pallas_call(
        functools.partial(_bias_kernel, head0=head0),
        grid=(n_heads, r // tr),
        in_specs=[pl.BlockSpec(memory_space=pltpu.SMEM),
                  pl.BlockSpec((tr, c), lambda h, i: (i, 0))],
        out_specs=pl.BlockSpec((None, tr, c), lambda h, i: (h, i, 0)),
        out_shape=jax.ShapeDtypeStruct((n_heads, r, c), F32),
        compiler_params=pltpu.CompilerParams(dimension_semantics=("arbitrary", "arbitrary")),
        name="bias_tiles",
    )(tab, idx)


def _attn_a_kernel(lam_ref, q_ref, k_ref, vt_ref, km_ref, vmt_ref, bias_ref, bm_ref, z_ref, g_ref,
                   o_ref, qt_ref, m_ref, acc_ref, s_ref, *, tq, tk, nk, ratio, nv, tb):
    i = pl.program_id(2)
    q = q_ref[...]
    row = lax.broadcasted_iota(jnp.int32, q.shape, 0)
    qt_ref[:, :tq] = jnp.where(row < A_QK_DIM, q, jnp.zeros_like(q))
    qt_ref[:, tq:] = jnp.where(row >= A_QK_DIM, q, jnp.zeros_like(q))
    qt = qt_ref[...]

    def both(fn):
        return jnp.concatenate([fn(slice(0, tq)), fn(slice(tq, 2 * tq))], axis=1)

    def pv_and_sum(vt, pb):
        vt1 = jnp.concatenate([vt, jnp.ones((SUM_ROWS, vt.shape[1]), BF16)], axis=0)
        return both(lambda c: jnp.dot(vt1, pb[:, c], preferred_element_type=F32))

    def raw_scores(j):
        ks = pl.multiple_of(j * tk, tk)
        return jnp.dot(k_ref[pl.ds(ks, tk), :], qt, preferred_element_type=F32)

    def bias2(j):
        bt = bias_ref[jnp.clip(j - i * ratio + 2, 0, nv - 1)]
        return jnp.concatenate([bt, bt], axis=1)

    def tile_scores(j):
        return raw_scores(j) + bias2(j)

    bm = bm_ref[...]
    s_meta = jnp.dot(km_ref[...], qt, preferred_element_type=F32) + jnp.concatenate([bm, bm], axis=1)
    s_ref[0] = raw_scores(0)
    m0 = jnp.max(s_meta, axis=0, keepdims=True)

    def init_from_meta():
        acc_ref[...] = pv_and_sum(vmt_ref[...], jnp.exp2(s_meta - m0).astype(BF16))

    init_from_meta()

    def group(g, carry):
        pv = None
        for u in range(tb):
            j = g * tb + u
            nxt = j + 1 if u < tb - 1 else jnp.minimum(j + 1, nk - 1)
            s_ref[(u + 1) % 2] = raw_scores(nxt)
            p = jnp.exp2(s_ref[u % 2] + bias2(j) - m0)
            d = pv_and_sum(vt_ref[j], p.astype(BF16))
            pv = d if pv is None else pv + d
        acc_ref[...] += pv
        return carry

    lax.fori_loop(0, nk // tb, group, 0)

    def finish():
        acc = acc_ref[...]
        l = acc[A_V_DIM:A_V_DIM + 1, :]
        inv = 1.0 / l
        ot = acc[:A_V_DIM, :tq] * inv[:, :tq] - lam_ref[0] * (acc[:A_V_DIM, tq:] * inv[:, tq:])
        ssq = jnp.sum(ot * ot, axis=0, keepdims=True)
        yt = ot * lax.rsqrt(ssq * (1.0 / A_V_DIM) + EPS)
        z = z_ref[...].astype(F32)
        o_ref[...] = (yt.T * g_ref[...] * (1.0 - LAM_INIT) * (z * jax.nn.sigmoid(z))).astype(BF16)
        return l, ssq

    l, ssq = finish()
    overflow = (jnp.max(jnp.where(l < OVERFLOW_GUARD, 0.0, 1.0))
                + jnp.max(jnp.where(ssq < F32_FINITE, 0.0, 1.0))) > 0.0

    @pl.when(overflow)
    def _():
        init_from_meta()
        m_ref[...] = m0

        def step(j, carry):
            s = tile_scores(j)
            m_old = m_ref[...]
            m_new = jnp.maximum(m_old, jnp.max(s, axis=0, keepdims=True))
            alpha = jnp.exp2(m_old - m_new)
            acc_ref[...] = alpha * acc_ref[...] + pv_and_sum(vt_ref[j], jnp.exp2(s - m_new).astype(BF16))
            m_ref[...] = m_new
            return carry

        lax.fori_loop(0, nk, step, 0)
        finish()


def _attn_a(lam, qat, ka, vat, ka_m, vat_m, bias, bias_m, za, subln_g, *, tq, tk):
    nb, s, _ = ka.shape
    nk = s // tk
    ratio = tq // tk
    nv = bias.shape[1]
    tb = next(c for c in (16, 8, 4, 2, 1) if nk % c == 0)
    assert nk % tb == 0
    head = lambda n: pl.BlockSpec((None, n, A_V_DIM), lambda b, h, i: (b, i, h))
    return pl.pallas_call(
        functools.partial(_attn_a_kernel, tq=tq, tk=tk, nk=nk, ratio=ratio, nv=nv, tb=tb),
        grid=(nb, A_HEADS, s // tq),
        in_specs=[
            pl.BlockSpec(memory_space=pltpu.SMEM),
            pl.BlockSpec((None, None, A_V_DIM, tq), lambda b, h, i: (b, i, h, 0)),
            pl.BlockSpec((None, s, A_V_DIM), lambda b, h, i: (b, 0, h)),
            pl.BlockSpec((None, nk, A_V_DIM, tk), lambda b, h, i: (b, 0, h, 0)),
            pl.BlockSpec((None, N_META, A_V_DIM), lambda b, h, i: (0, 0, h)),
            pl.BlockSpec((None, None, A_V_DIM, N_META), lambda b, h, i: (0, 0, h, 0)),
            pl.BlockSpec((None, nv, tk, tq), lambda b, h, i: (h, 0, 0, 0)),
            pl.BlockSpec((None, N_META, tq), lambda b, h, i: (h, 0, i)),
            head(tq),
            pl.BlockSpec((1, A_V_DIM), lambda b, h, i: (0, 0)),
        ],
        out_specs=head(tq),
        out_shape=jax.ShapeDtypeStruct((nb, s, A_WIDTH), BF16),
        scratch_shapes=[
            pltpu.VMEM((A_V_DIM, 2 * tq), BF16),
            pltpu.VMEM((1, 2 * tq), F32),
            pltpu.VMEM((A_V_DIM + SUM_ROWS, 2 * tq), F32),
            pltpu.VMEM((2, tk, 2 * tq), F32),
        ],
        compiler_params=pltpu.CompilerParams(
            dimension_semantics=("arbitrary", "arbitrary", "arbitrary"),
            vmem_limit_bytes=VMEM_LIMIT_BYTES),
        name="attn_a",
    )(lam, qat, ka, vat, ka_m, vat_m, bias, bias_m, za, subln_g)


_BQ = 256
_BK = _BQ + 2 * WINDOW
_PAD_FRONT = WINDOW - N_META


def _attn_b_kernel(sink_ref, q_ref, k0, k1, k2, k3, v0, v1, v2, v3, bias_ref, z_ref, o_ref, *, s_len):
    i = pl.program_id(1)
    k = jnp.concatenate([k0[...], k1[...], k2[...], k3[...]], axis=0).astype(F32)
    v = jnp.concatenate([v0[...], v1[...], v2[...], v3[...]], axis=0).astype(F32)
    lane = lax.broadcasted_iota(jnp.int32, k.shape, 1)
    lo = lane < B_HEAD_DIM

    def placed(t):
        tr = pltpu.roll(t, B_HEAD_DIM, axis=1)
        low = [jnp.where(lo, t, 0.0).astype(BF16), jnp.where(lo, tr, 0.0).astype(BF16)]
        high = [jnp.where(lo, 0.0, tr).astype(BF16), jnp.where(lo, 0.0, t).astype(BF16)]
        return low, high

    k_low, k_high = placed(k)
    v_low, v_high = placed(v)
    row_ext = i * _BQ + lax.broadcasted_iota(jnp.int32, (1, _BK), 1)
    vmask = jnp.where((row_ext >= _PAD_FRONT) & (row_ext < WINDOW + s_len), 0.0, NEG)

    for slab in range(B_HEADS // 2):
        g = slab // (B_HEADS // (2 * B_KV_HEADS))
        cols = slice(slab * LANES, (slab + 1) * LANES)
        qs = q_ref[:, cols]
        out = None
        for e in range(2):
            h = 2 * slab + e
            kx = (k_low if e == 0 else k_high)[g]
            vx = (v_low if e == 0 else v_high)[g]
            sc = lax.dot_general(qs, kx, (((1,), (1,)), ((), ())), preferred_element_type=F32)
            sc = sc + bias_ref[h] + vmask
            sk = sink_ref[h] * LOG2E
            m = jnp.maximum(jnp.max(sc, axis=-1, keepdims=True), sk)
            p = jnp.exp2(sc - m)
            l = jnp.sum(p, axis=-1, keepdims=True) + jnp.exp2(sk - m)
            pn = (p * (1.0 / l)).astype(BF16)
            oe = jnp.dot(pn, vx, preferred_element_type=F32)
            out = oe if out is None else out + oe
        z = z_ref[:, cols].astype(F32)
        o_ref[:, cols] = (out * (z * jax.nn.sigmoid(z))).astype(BF16)


def _attn_b(sink, qb, kb_ext, vb_ext, bias, zb):
    nb, s, _ = qb.shape
    row = pl.BlockSpec((None, _BQ, B_WIDTH), lambda b, i: (b, i, 0))
    nsub = _BK // WINDOW
    kv = [pl.BlockSpec((None, WINDOW, B_KV_WIDTH), (lambda b, i, t=t: (b, (_BQ // WINDOW) * i + t, 0)))
          for t in range(nsub)]
    return pl.pallas_call(
        functools.partial(_attn_b_kernel, s_len=s),
        grid=(nb, s // _BQ),
        in_specs=[pl.BlockSpec(memory_space=pltpu.SMEM), row] + kv + kv
                 + [pl.BlockSpec((B_HEADS, _BQ, _BK), lambda b, i: (0, 0, 0)), row],
        out_specs=row,
        out_shape=jax.ShapeDtypeStruct((nb, s, B_WIDTH), BF16),
        compiler_params=pltpu.CompilerParams(
            dimension_semantics=("arbitrary", "arbitrary"), vmem_limit_bytes=VMEM_LIMIT_BYTES),
        name="attn_b",
    )(sink, qb, *([kb_ext] * nsub), *([vb_ext] * nsub), bias, zb)


def _out_kernel(x_ref, a_ref, b_ref, ga_ref, gb_ref, woa_ref, wob_ref, wo_ref, g_ref, o_ref):
    ya = jnp.dot(a_ref[...], woa_ref[...], preferred_element_type=F32)
    yb = jnp.dot(b_ref[...], wob_ref[...], preferred_element_type=F32)
    mixed = (jax.nn.sigmoid(ga_ref[...].astype(F32)) * ya
             + jax.nn.sigmoid(gb_ref[...].astype(F32)) * yb)
    y = jnp.dot(mixed.astype(BF16), wo_ref[...], preferred_element_type=F32)
    y = y * lax.rsqrt(jnp.mean(y * y, axis=-1, keepdims=True) + EPS) * g_ref[...]
    o_ref[...] = x_ref[...] + y


def _output(x3, a, b, ga, gb, woa, wob, wo, g, *, tm):
    nb, s, d = x3.shape
    row = lambda n: pl.BlockSpec((None, tm, n), lambda bi, i: (bi, i, 0))
    const = lambda shape: pl.BlockSpec(shape, lambda bi, i: (0, 0), pipeline_mode=pl.Buffered(1))
    return pl.pallas_call(
        _out_kernel,
        grid=(nb, s // tm),
        in_specs=[row(d), row(A_WIDTH), row(B_WIDTH), row(d), row(d),
                  const((A_WIDTH, d)), const((B_WIDTH, d)), const((d, d)), const((1, d))],
        out_specs=row(d),
        out_shape=jax.ShapeDtypeStruct((nb, s, d), F32),
        compiler_params=pltpu.CompilerParams(
            dimension_semantics=("arbitrary", "arbitrary"), vmem_limit_bytes=VMEM_LIMIT_BYTES),
        name="out_proj",
    )(x3, a, b, ga, gb, woa, wob, wo, g)


def kernel(x, meta_tokens, rel_bias, pre_norm_g, w_in, b_in, lambda_q1, lambda_k1, lambda_q2, lambda_k2,
           subln_g, sink, w_out_a, w_out_b, w_out, post_norm_g):
    nb, s, d = x.shape
    tm = min(512, s)
    tq = min(512, s)
    tk = min(256, s)
    assert s % tm == 0 and s % tq == 0 and tq % tk == 0 and s % _BQ == 0

    w = w_in[0].astype(BF16)
    b = b_in[0][None, :]
    g_pre = pre_norm_g[0][None, :]
    lam = (jnp.exp(jnp.sum(lambda_q1[0].astype(F32) * lambda_k1[0].astype(F32)))
           - jnp.exp(jnp.sum(lambda_q2[0].astype(F32) * lambda_k2[0].astype(F32))) + LAM_INIT).reshape(1)

    qat, ka, vat, za, qb, kb, vb, zb, ga, gb = _project(x, g_pre, w, b, tm=tm, tq=tq, tkv=tk)
    meta = _project(meta_tokens[None], g_pre, w, b, tm=N_META, tq=N_META, tkv=N_META)
    ka_m, vat_m, kb_m, vb_m = meta[1], meta[2], meta[5], meta[6]

    ratio = tq // tk
    nv = ratio + 4
    off = (jnp.arange(nv, dtype=jnp.int32) - 2)[:, None, None] * tk
    rel_a = off + jnp.arange(tk, dtype=jnp.int32)[None, :, None] - jnp.arange(tq, dtype=jnp.int32)[None, None, :]
    bias_a = _bias_tiles(rel_bias, _t5_bucket(rel_a).reshape(nv * tk, tq), head0=0, n_heads=A_HEADS,
                         tr=tk).reshape(A_HEADS, nv, tk, tq)
    rel_m = jnp.arange(N_META, dtype=jnp.int32)[:, None] - (jnp.arange(s, dtype=jnp.int32)[None, :] + N_META)
    bias_m = _bias_tiles(rel_bias, _t5_bucket(rel_m), head0=0, n_heads=A_HEADS, tr=N_META)
    rel_b = (jnp.arange(_BK, dtype=jnp.int32)[None, :] - WINDOW) - jnp.arange(_BQ, dtype=jnp.int32)[:, None]
    idx_b = jnp.where(jnp.abs(rel_b) <= WINDOW, _t5_bucket(rel_b), N_BUCKETS)
    bias_b = _bias_tiles(rel_bias, idx_b, head0=A_HEADS, n_heads=B_HEADS, tr=_BQ)

    oa = _attn_a(lam, qat, ka, vat, ka_m, vat_m, bias_a, bias_m, za, subln_g[0][None, :], tq=tq, tk=tk)

    def extend(tok, m):
        return jnp.concatenate([
            jnp.zeros((nb, _PAD_FRONT, B_KV_WIDTH), BF16),
            jnp.broadcast_to(m, (nb, N_META, B_KV_WIDTH)), tok,
            jnp.zeros((nb, WINDOW, B_KV_WIDTH), BF16)], axis=1)

    ob = _attn_b(sink[0], qb, extend(kb, kb_m), extend(vb, vb_m), bias_b, zb)

    return _output(x, oa, ob, ga, gb, w_out_a[0].astype(BF16), w_out_b[0].astype(BF16),
                   w_out[0].astype(BF16), post_norm_g[0][None, :], tm=tm)
```

```python
import functools
import math

import jax
import jax.numpy as jnp
from jax import lax
from jax.experimental import pallas as pl
from jax.experimental.pallas import tpu as pltpu

N_META = 16
WINDOW = 128
A_HEADS = 4
A_QK_DIM = 64
A_V_DIM = 2 * A_QK_DIM
A_WIDTH = A_HEADS * A_V_DIM
B_HEADS = 8
B_KV_HEADS = 2
B_HEAD_DIM = 64
B_WIDTH = B_HEADS * B_HEAD_DIM
B_KV_WIDTH = B_KV_HEADS * B_HEAD_DIM
N_BUCKETS = 32
MAX_DISTANCE = 128
EPS = 1e-6
NEG = -1e30
LOG2E = math.log2(math.e)
LAM_INIT = 0.8 - 0.6 * math.exp(-0.3 * 0)

LANES = 128
OVERFLOW_GUARD = 2.0 ** 100
F32_FINITE = 3.0e38
SUM_ROWS = 16
VMEM_LIMIT_BYTES = 56 * 1024 * 1024

_COLS = (A_WIDTH, A_WIDTH, A_WIDTH, A_WIDTH, B_WIDTH, B_KV_WIDTH, B_KV_WIDTH, B_WIDTH)
BF16 = jnp.bfloat16
F32 = jnp.float32


def _t5_bucket(rel):
    half = N_BUCKETS // 2
    max_exact = half // 2
    ret = jnp.where(rel > 0, half, 0)
    n = jnp.abs(rel)
    nf = jnp.maximum(n, 1).astype(F32)
    large = max_exact + (jnp.log(nf / max_exact) / math.log(MAX_DISTANCE / max_exact)
                         * (half - max_exact)).astype(jnp.int32)
    large = jnp.minimum(large, half - 1)
    return ret + jnp.where(n < max_exact, n, large)


def _proj_kernel(x_ref, g_ref, w_ref, b_ref,
                 qat_ref, ka_ref, vat_ref, za_ref, qb_ref, kb_ref, vb_ref, zb_ref, ga_ref, gb_ref,
                 *, d_model, tq, tkv):
    x = x_ref[...]
    hn = (x * lax.rsqrt(jnp.mean(x * x, axis=-1, keepdims=True) + EPS) * g_ref[...]).astype(BF16)

    def seg(c0, n):
        return jnp.dot(hn, w_ref[:, c0:c0 + n], preferred_element_type=F32) + b_ref[:, c0:c0 + n]

    qscale = (A_QK_DIM ** -0.5) * LOG2E
    c = 0
    qa = seg(c, A_WIDTH) * qscale; c += A_WIDTH
    for t in range(qat_ref.shape[0]):
        qat_ref[t] = qa[t * tq:(t + 1) * tq, :].T.astype(BF16)
    ka_ref[...] = seg(c, A_WIDTH).astype(BF16); c += A_WIDTH
    va = seg(c, A_WIDTH); c += A_WIDTH
    for t in range(vat_ref.shape[0]):
        vat_ref[t] = va[t * tkv:(t + 1) * tkv, :].T.astype(BF16)
    za_ref[...] = seg(c, A_WIDTH).astype(BF16); c += A_WIDTH
    qb_ref[...] = (seg(c, B_WIDTH) * qscale).astype(BF16); c += B_WIDTH
    kb_ref[...] = seg(c, B_KV_WIDTH).astype(BF16); c += B_KV_WIDTH
    vb_ref[...] = seg(c, B_KV_WIDTH).astype(BF16); c += B_KV_WIDTH
    zb_ref[...] = seg(c, B_WIDTH).astype(BF16); c += B_WIDTH
    ga_ref[...] = seg(c, d_model).astype(BF16); c += d_model
    gb_ref[...] = seg(c, d_model).astype(BF16)


def _project(x3, g, w, b, *, tm, tq, tkv):
    nb, s, d = x3.shape
    d_in = w.shape[1]
    widths = _COLS + (d, d)
    nt = s // tm
    row = lambda n: pl.BlockSpec((None, tm, n), lambda bi, i: (bi, i, 0))
    const = lambda shape: pl.BlockSpec(shape, lambda bi, i: (0, 0), pipeline_mode=pl.Buffered(1))
    out_specs, out_shapes = [], []
    for idx, n in enumerate(widths):
        if idx in (0, 2):
            tt = tq if idx == 0 else tkv
            out_specs.append(pl.BlockSpec((None, tm // tt, n, tt), lambda bi, i: (bi, i, 0, 0)))
            out_shapes.append(jax.ShapeDtypeStruct((nb, s // tt, n, tt), BF16))
        else:
            out_specs.append(row(n))
            out_shapes.append(jax.ShapeDtypeStruct((nb, s, n), BF16))
    return pl.pallas_call(
        functools.partial(_proj_kernel, d_model=d, tq=tq, tkv=tkv),
        grid=(nb, nt),
        in_specs=[row(d), const((1, d)), const((d, d_in)), const((1, d_in))],
        out_specs=out_specs,
        out_shape=out_shapes,
        compiler_params=pltpu.CompilerParams(
            dimension_semantics=("arbitrary", "arbitrary"), vmem_limit_bytes=VMEM_LIMIT_BYTES),
        name="proj",
    )(x3, g, w, b)


def _bias_kernel(tab_ref, idx_ref, o_ref, *, head0):
    h = pl.program_id(0) + head0
    idx = idx_ref[...]
    acc = jnp.full(idx.shape, NEG, F32)
    for bkt in range(N_BUCKETS):
        acc = jnp.where(idx == bkt, tab_ref[bkt, h] * LOG2E, acc)
    o_ref[...] = acc


def _bias_tiles(tab, idx, *, head0, n_heads, tr):
    r, c = idx.shape
    return pl.pallas_call(
        functools.partial(_bias_kernel, head0=head0),
        grid=(n_heads, r // tr),
        in_specs=[pl.BlockSpec(memory_space=pltpu.SMEM),
                  pl.BlockSpec((tr, c), lambda h, i: (i, 0))],
        out_specs=pl.BlockSpec((None, tr, c), lambda h, i: (h, i, 0)),
        out_shape=jax.ShapeDtypeStruct((n_heads, r, c), F32),
        compiler_params=pltpu.CompilerParams(dimension_semantics=("arbitrary", "arbitrary")),
        name="bias_tiles",
    )(tab, idx)


def _attn_a_kernel(lam_ref, q_ref, k_ref, vt_ref, km_ref, vmt_ref, bias_ref, bm_ref, z_ref, g_ref,
                   o_ref, qt_ref, m_ref, acc_ref, p_ref, *, tq, tk, nk, ratio, nv, tb):
    i = pl.program_id(2)
    q = q_ref[...]
    row = lax.broadcasted_iota(jnp.int32, q.shape, 0)
    qt_ref[:, :tq] = jnp.where(row < A_QK_DIM, q, jnp.zeros_like(q))
    qt_ref[:, tq:] = jnp.where(row >= A_QK_DIM, q, jnp.zeros_like(q))
    qt = qt_ref[...]

    def both(fn):
        return jnp.concatenate([fn(slice(0, tq)), fn(slice(tq, 2 * tq))], axis=1)

    def pv_and_sum(vt, pb):
        vt1 = jnp.concatenate([vt, jnp.ones((SUM_ROWS, vt.shape[1]), BF16)], axis=0)
        return both(lambda c: jnp.dot(vt1, pb[:, c], preferred_element_type=F32))

    def raw_scores(j):
        ks = pl.multiple_of(j * tk, tk)
        return jnp.dot(k_ref[pl.ds(ks, tk), :], qt, preferred_element_type=F32)

    def bias2(j):
        bt = bias_ref[jnp.clip(j - i * ratio + 2, 0, nv - 1)]
        return jnp.concatenate([bt, bt], axis=1)

    def tile_scores(j):
        return raw_scores(j) + bias2(j)

    bm = bm_ref[...]
    s_meta = jnp.dot(km_ref[...], qt, preferred_element_type=F32) + jnp.concatenate([bm, bm], axis=1)
    m0 = jnp.max(s_meta, axis=0, keepdims=True)

    def init_from_meta():
        acc_ref[...] = pv_and_sum(vmt_ref[...], jnp.exp2(s_meta - m0).astype(BF16))

    def stage_p(j, slot):
        p_ref[slot] = jnp.exp2(tile_scores(j) - m0).astype(BF16)

    stage_p(0, 0)
    init_from_meta()

    def group(g, carry):
        pv = None
        for u in range(tb):
            j = g * tb + u
            nxt = j + 1 if u < tb - 1 else jnp.minimum(j + 1, nk - 1)
            stage_p(nxt, (u + 1) % 2)
            d = pv_and_sum(vt_ref[j], p_ref[u % 2])
            pv = d if pv is None else pv + d
        acc_ref[...] += pv
        return carry

    lax.fori_loop(0, nk // tb, group, 0)

    def finish():
        acc = acc_ref[...]
        l = acc[A_V_DIM:A_V_DIM + 1, :]
        inv = 1.0 / l
        ot = acc[:A_V_DIM, :tq] * inv[:, :tq] - lam_ref[0] * (acc[:A_V_DIM, tq:] * inv[:, tq:])
        ssq = jnp.sum(ot * ot, axis=0, keepdims=True)
        yt = ot * lax.rsqrt(ssq * (1.0 / A_V_DIM) + EPS)
        z = z_ref[...].astype(F32)
        o_ref[...] = (yt.T * g_ref[...] * (1.0 - LAM_INIT) * (z * jax.nn.sigmoid(z))).astype(BF16)
        return l, ssq

    l, ssq = finish()
    overflow = (jnp.max(jnp.where(l < OVERFLOW_GUARD, 0.0, 1.0))
                + jnp.max(jnp.where(ssq < F32_FINITE, 0.0, 1.0))) > 0.0

    @pl.when(overflow)
    def _():
        init_from_meta()
        m_ref[...] = m0

        def step(j, carry):
            s = tile_scores(j)
            m_old = m_ref[...]
            m_new = jnp.maximum(m_old, jnp.max(s, axis=0, keepdims=True))
            alpha = jnp.exp2(m_old - m_new)
            acc_ref[...] = alpha * acc_ref[...] + pv_and_sum(vt_ref[j], jnp.exp2(s - m_new).astype(BF16))
            m_ref[...] = m_new
            return carry

        lax.fori_loop(0, nk, step, 0)
        finish()


def _attn_a(lam, qat, ka, vat, ka_m, vat_m, bias, bias_m, za, subln_g, *, tq, tk):
    nb, s, _ = ka.shape
    nk = s // tk
    ratio = tq // tk
    nv = bias.shape[1]
    tb = next(c for c in (16, 8, 4, 2, 1) if nk % c == 0)
    assert nk % tb == 0
    head = lambda n: pl.BlockSpec((None, n, A_V_DIM), lambda b, h, i: (b, i, h))
    return pl.pallas_call(
        functools.partial(_attn_a_kernel, tq=tq, tk=tk, nk=nk, ratio=ratio, nv=nv, tb=tb),
        grid=(nb, A_HEADS, s // tq),
        in_specs=[
            pl.BlockSpec(memory_space=pltpu.SMEM),
            pl.BlockSpec((None, None, A_V_DIM, tq), lambda b, h, i: (b, i, h, 0)),
            pl.BlockSpec((None, s, A_V_DIM), lambda b, h, i: (b, 0, h)),
            pl.BlockSpec((None, nk, A_V_DIM, tk), lambda b, h, i: (b, 0, h, 0)),
            pl.BlockSpec((None, N_META, A_V_DIM), lambda b, h, i: (0, 0, h)),
            pl.BlockSpec((None, None, A_V_DIM, N_META), lambda b, h, i: (0, 0, h, 0)),
            pl.BlockSpec((None, nv, tk, tq), lambda b, h, i: (h, 0, 0, 0)),
            pl.BlockSpec((None, N_META, tq), lambda b, h, i: (h, 0, i)),
            head(tq),
            pl.BlockSpec((1, A_V_DIM), lambda b, h, i: (0, 0)),
        ],
        out_specs=head(tq),
        out_shape=jax.ShapeDtypeStruct((nb, s, A_WIDTH), BF16),
        scratch_shapes=[
            pltpu.VMEM((A_V_DIM, 2 * tq), BF16),
            pltpu.VMEM((1, 2 * tq), F32),
            pltpu.VMEM((A_V_DIM + SUM_ROWS, 2 * tq), F32),
            pltpu.VMEM((2, tk, 2 * tq), BF16),
        ],
        compiler_params=pltpu.CompilerParams(
            dimension_semantics=("arbitrary", "arbitrary", "arbitrary"),
            vmem_limit_bytes=VMEM_LIMIT_BYTES),
        name="attn_a",
    )(lam, qat, ka, vat, ka_m, vat_m, bias, bias_m, za, subln_g)


_BQ = 256
_BK = _BQ + 2 * WINDOW
_PAD_FRONT = WINDOW - N_META


def _attn_b_kernel(sink_ref, q_ref, k0, k1, k2, k3, v0, v1, v2, v3, bias_ref, z_ref, o_ref, *, s_len):
    i = pl.program_id(1)
    k = jnp.concatenate([k0[...], k1[...], k2[...], k3[...]], axis=0).astype(F32)
    v = jnp.concatenate([v0[...], v1[...], v2[...], v3[...]], axis=0).astype(F32)
    lane = lax.broadcasted_iota(jnp.int32, k.shape, 1)
    lo = lane < B_HEAD_DIM

    def placed(t):
        tr = pltpu.roll(t, B_HEAD_DIM, axis=1)
        low = [jnp.where(lo, t, 0.0).astype(BF16), jnp.where(lo, tr, 0.0).astype(BF16)]
        high = [jnp.where(lo, 0.0, tr).astype(BF16), jnp.where(lo, 0.0, t).astype(BF16)]
        return low, high

    k_low, k_high = placed(k)
    v_low, v_high = placed(v)
    row_ext = i * _BQ + lax.broadcasted_iota(jnp.int32, (1, _BK), 1)
    vmask = jnp.where((row_ext >= _PAD_FRONT) & (row_ext < WINDOW + s_len), 0.0, NEG)

    for slab in range(B_HEADS // 2):
        g = slab // (B_HEADS // (2 * B_KV_HEADS))
        cols = slice(slab * LANES, (slab + 1) * LANES)
        qs = q_ref[:, cols]
        out = None
        for e in range(2):
            h = 2 * slab + e
            kx = (k_low if e == 0 else k_high)[g]
            vx = (v_low if e == 0 else v_high)[g]
            sc = lax.dot_general(qs, kx, (((1,), (1,)), ((), ())), preferred_element_type=F32)
            sc = sc + bias_ref[h] + vmask
            sk = sink_ref[h] * LOG2E
            m = jnp.maximum(jnp.max(sc, axis=-1, keepdims=True), sk)
            p = jnp.exp2(sc - m)
            l = jnp.sum(p, axis=-1, keepdims=True) + jnp.exp2(sk - m)
            pn = (p * (1.0 / l)).astype(BF16)
            oe = jnp.dot(pn, vx, preferred_element_type=F32)
            out = oe if out is None else out + oe
        z = z_ref[:, cols].astype(F32)
        o_ref[:, cols] = (out * (z * jax.nn.sigmoid(z))).astype(BF16)


def _attn_b(sink, qb, kb_ext, vb_ext, bias, zb):
    nb, s, _ = qb.shape
    row = pl.BlockSpec((None, _BQ, B_WIDTH), lambda b, i: (b, i, 0))
    nsub = _BK // WINDOW
    kv = [pl.BlockSpec((None, WINDOW, B_KV_WIDTH), (lambda b, i, t=t: (b, (_BQ // WINDOW) * i + t, 0)))
          for t in range(nsub)]
    return pl.pallas_call(
        functools.partial(_attn_b_kernel, s_len=s),
        grid=(nb, s // _BQ),
        in_specs=[pl.BlockSpec(memory_space=pltpu.SMEM), row] + kv + kv
                 + [pl.BlockSpec((B_HEADS, _BQ, _BK), lambda b, i: (0, 0, 0)), row],
        out_specs=row,
        out_shape=jax.ShapeDtypeStruct((nb, s, B_WIDTH), BF16),
        compiler_params=pltpu.CompilerParams(
            dimension_semantics=("arbitrary", "arbitrary"), vmem_limit_bytes=VMEM_LIMIT_BYTES),
        name="attn_b",
    )(sink, qb, *([kb_ext] * nsub), *([vb_ext] * nsub), bias, zb)


def _out_kernel(x_ref, a_ref, b_ref, ga_ref, gb_ref, woa_ref, wob_ref, wo_ref, g_ref, o_ref):
    ya = jnp.dot(a_ref[...], woa_ref[...], preferred_element_type=F32)
    yb = jnp.dot(b_ref[...], wob_ref[...], preferred_element_type=F32)
    mixed = (jax.nn.sigmoid(ga_ref[...].astype(F32)) * ya
             + jax.nn.sigmoid(gb_ref[...].astype(F32)) * yb)
    y = jnp.dot(mixed.astype(BF16), wo_ref[...], preferred_element_type=F32)
    y = y * lax.rsqrt(jnp.mean(y * y, axis=-1, keepdims=True) + EPS) * g_ref[...]
    o_ref[...] = x_ref[...] + y


def _output(x3, a, b, ga, gb, woa, wob, wo, g, *, tm):
    nb, s, d = x3.shape
    row = lambda n: pl.BlockSpec((None, tm, n), lambda bi, i: (bi, i, 0))
    const = lambda shape: pl.BlockSpec(shape, lambda bi, i: (0, 0), pipeline_mode=pl.Buffered(1))
    return pl.pallas_call(
        _out_kernel,
        grid=(nb, s // tm),
        in_specs=[row(d), row(A_WIDTH), row(B_WIDTH), row(d), row(d),
                  const((A_WIDTH, d)), const((B_WIDTH, d)), const((d, d)), const((1, d))],
        out_specs=row(d),
        out_shape=jax.ShapeDtypeStruct((nb, s, d), F32),
        compiler_params=pltpu.CompilerParams(
            dimension_semantics=("arbitrary", "arbitrary"), vmem_limit_bytes=VMEM_LIMIT_BYTES),
        name="out_proj",
    )(x3, a, b, ga, gb, woa, wob, wo, g)


def kernel(x, meta_tokens, rel_bias, pre_norm_g, w_in, b_in, lambda_q1, lambda_k1, lambda_q2, lambda_k2,
           subln_g, sink, w_out_a, w_out_b, w_out, post_norm_g):
    nb, s, d = x.shape
    tm = min(512, s)
    tq = min(512, s)
    tk = min(256, s)
    assert s % tm == 0 and s % tq == 0 and tq % tk == 0 and s % _BQ == 0

    w = w_in[0].astype(BF16)
    b = b_in[0][None, :]
    g_pre = pre_norm_g[0][None, :]
    lam = (jnp.exp(jnp.sum(lambda_q1[0].astype(F32) * lambda_k1[0].astype(F32)))
           - jnp.exp(jnp.sum(lambda_q2[0].astype(F32) * lambda_k2[0].astype(F32))) + LAM_INIT).reshape(1)

    qat, ka, vat, za, qb, kb, vb, zb, ga, gb = _project(x, g_pre, w, b, tm=tm, tq=tq, tkv=tk)
    meta = _project(meta_tokens[None], g_pre, w, b, tm=N_META, tq=N_META, tkv=N_META)
    ka_m, vat_m, kb_m, vb_m = meta[1], meta[2], meta[5], meta[6]

    ratio = tq // tk
    nv = ratio + 4
    off = (jnp.arange(nv, dtype=jnp.int32) - 2)[:, None, None] * tk
    rel_a = off + jnp.arange(tk, dtype=jnp.int32)[None, :, None] - jnp.arange(tq, dtype=jnp.int32)[None, None, :]
    bias_a = _bias_tiles(rel_bias, _t5_bucket(rel_a).reshape(nv * tk, tq), head0=0, n_heads=A_HEADS,
                         tr=tk).reshape(A_HEADS, nv, tk, tq)
    rel_m = jnp.arange(N_META, dtype=jnp.int32)[:, None] - (jnp.arange(s, dtype=jnp.int32)[None, :] + N_META)
    bias_m = _bias_tiles(rel_bias, _t5_bucket(rel_m), head0=0, n_heads=A_HEADS, tr=N_META)
    rel_b = (jnp.arange(_BK, dtype=jnp.int32)[None, :] - WINDOW) - jnp.arange(_BQ, dtype=jnp.int32)[:, None]
    idx_b = jnp.where(jnp.abs(rel_b) <= WINDOW, _t5_bucket(rel_b), N_BUCKETS)
    bias_b = _bias_tiles(rel_bias, idx_b, head0=A_HEADS, n_heads=B_HEADS, tr=_BQ)

    oa = _attn_a(lam, qat, ka, vat, ka_m, vat_m, bias_a, bias_m, za, subln_g[0][None, :], tq=tq, tk=tk)

    def extend(tok, m):
        return jnp.concatenate([
            jnp.zeros((nb, _PAD_FRONT, B_KV_WIDTH), BF16),
            jnp.broadcast_to(m, (nb, N_META, B_KV_WIDTH)), tok,
            jnp.zeros((nb, WINDOW, B_KV_WIDTH), BF16)], axis=1)

    ob = _attn_b(sink[0], qb, extend(kb, kb_m), extend(vb, vb_m), bias_b, zb)

    return _output(x, oa, ob, ga, gb, w_out_a[0].astype(BF16), w_out_b[0].astype(BF16),
                   w_out[0].astype(BF16), post_norm_g[0][None, :], tm=tm)
```

```python
import functools
import math

import jax
import jax.numpy as jnp
from jax import lax
from jax.experimental import pallas as pl
from jax.experimental.pallas import tpu as pltpu

N_META = 16
WINDOW = 128
A_HEADS = 4
A_QK_DIM = 64
A_V_DIM = 2 * A_QK_DIM
A_WIDTH = A_HEADS * A_V_DIM
B_HEADS = 8
B_KV_HEADS = 2
B_HEAD_DIM = 64
B_WIDTH = B_HEADS * B_HEAD_DIM
B_KV_WIDTH = B_KV_HEADS * B_HEAD_DIM
N_BUCKETS = 32
MAX_DISTANCE = 128
EPS = 1e-6
NEG = -1e30
LOG2E = math.log2(math.e)
LAM_INIT = 0.8 - 0.6 * math.exp(-0.3 * 0)

LANES = 128
OVERFLOW_GUARD = 2.0 ** 100
F32_FINITE = 3.0e38
SUM_ROWS = 16
VMEM_LIMIT_BYTES = 56 * 1024 * 1024

_COLS = (A_WIDTH, A_WIDTH, A_WIDTH, A_WIDTH, B_WIDTH, B_KV_WIDTH, B_KV_WIDTH, B_WIDTH)
BF16 = jnp.bfloat16
F32 = jnp.float32


def _t5_bucket(rel):
    half = N_BUCKETS // 2
    max_exact = half // 2
    ret = jnp.where(rel > 0, half, 0)
    n = jnp.abs(rel)
    nf = jnp.maximum(n, 1).astype(F32)
    large = max_exact + (jnp.log(nf / max_exact) / math.log(MAX_DISTANCE / max_exact)
                         * (half - max_exact)).astype(jnp.int32)
    large = jnp.minimum(large, half - 1)
    return ret + jnp.where(n < max_exact, n, large)


def _proj_kernel(x_ref, g_ref, w_ref, b_ref,
                 qat_ref, ka_ref, vat_ref, za_ref, qbt_ref, kb_ref, vb_ref, zb_ref, ga_ref, gb_ref,
                 *, d_model, tq, tqb, tkv):
    x = x_ref[...]
    hn = (x * lax.rsqrt(jnp.mean(x * x, axis=-1, keepdims=True) + EPS) * g_ref[...]).astype(BF16)

    def seg(c0, n):
        return jnp.dot(hn, w_ref[:, c0:c0 + n], preferred_element_type=F32) + b_ref[:, c0:c0 + n]

    qscale = (A_QK_DIM ** -0.5) * LOG2E
    c = 0
    qa = seg(c, A_WIDTH) * qscale; c += A_WIDTH
    for t in range(qat_ref.shape[0]):
        qat_ref[t] = qa[t * tq:(t + 1) * tq, :].T.astype(BF16)
    ka_ref[...] = seg(c, A_WIDTH).astype(BF16); c += A_WIDTH
    va = seg(c, A_WIDTH); c += A_WIDTH
    for t in range(vat_ref.shape[0]):
        vat_ref[t] = va[t * tkv:(t + 1) * tkv, :].T.astype(BF16)
    za_ref[...] = seg(c, A_WIDTH).astype(BF16); c += A_WIDTH
    qb = seg(c, B_WIDTH) * qscale; c += B_WIDTH
    for t in range(qbt_ref.shape[0]):
        qbt_ref[t] = qb[t * tqb:(t + 1) * tqb, :].T.astype(BF16)
    kb_ref[...] = seg(c, B_KV_WIDTH).astype(BF16); c += B_KV_WIDTH
    vb_ref[...] = seg(c, B_KV_WIDTH).astype(BF16); c += B_KV_WIDTH
    zb_ref[...] = seg(c, B_WIDTH).astype(BF16); c += B_WIDTH
    ga_ref[...] = seg(c, d_model).astype(BF16); c += d_model
    gb_ref[...] = seg(c, d_model).astype(BF16)


def _project(x3, g, w, b, *, tm, tq, tqb, tkv):
    nb, s, d = x3.shape
    d_in = w.shape[1]
    widths = _COLS + (d, d)
    nt = s // tm
    row = lambda n: pl.BlockSpec((None, tm, n), lambda bi, i: (bi, i, 0))
    const = lambda shape: pl.BlockSpec(shape, lambda bi, i: (0, 0), pipeline_mode=pl.Buffered(1))
    out_specs, out_shapes = [], []
    for idx, n in enumerate(widths):
        if idx in (0, 2, 4):
            tt = {0: tq, 2: tkv, 4: tqb}[idx]
            out_specs.append(pl.BlockSpec((None, tm // tt, n, tt), lambda bi, i: (bi, i, 0, 0)))
            out_shapes.append(jax.ShapeDtypeStruct((nb, s // tt, n, tt), BF16))
        else:
            out_specs.append(row(n))
            out_shapes.append(jax.ShapeDtypeStruct((nb, s, n), BF16))
    return pl.pallas_call(
        functools.partial(_proj_kernel, d_model=d, tq=tq, tqb=tqb, tkv=tkv),
        grid=(nb, nt),
        in_specs=[row(d), const((1, d)), const((d, d_in)), const((1, d_in))],
        out_specs=out_specs,
        out_shape=out_shapes,
        compiler_params=pltpu.CompilerParams(
            dimension_semantics=("arbitrary", "arbitrary"), vmem_limit_bytes=VMEM_LIMIT_BYTES),
        name="proj",
    )(x3, g, w, b)


def _bias_kernel(tab_ref, idx_ref, o_ref, *, head0):
    h = pl.program_id(0) + head0
    idx = idx_ref[...]
    acc = jnp.full(idx.shape, NEG, F32)
    for bkt in range(N_BUCKETS):
        acc = jnp.where(idx == bkt, tab_ref[bkt, h] * LOG2E, acc)
    o_ref[...] = acc


def _bias_tiles(tab, idx, *, head0, n_heads, tr, heads_on_lanes=False):
    r, c = idx.shape
    if heads_on_lanes:
        out_spec = pl.BlockSpec((tr, c), lambda h, i: (i, h))
        out_shape = jax.ShapeDtypeStruct((r, n_heads * c), F32)
    else:
        out_spec = pl.BlockSpec((None, tr, c), lambda h, i: (h, i, 0))
        out_shape = jax.ShapeDtypeStruct((n_heads, r, c), F32)
    return pl.pallas_call(
        functools.partial(_bias_kernel, head0=head0),
        grid=(n_heads, r // tr),
        in_specs=[pl.BlockSpec(memory_space=pltpu.SMEM),
                  pl.BlockSpec((tr, c), lambda h, i: (i, 0))],
        out_specs=out_spec,
        out_shape=out_shape,
        compiler_params=pltpu.CompilerParams(dimension_semantics=("arbitrary", "arbitrary")),
        name="bias_tiles",
    )(tab, idx)


def _attn_a_kernel(lam_ref, q_ref, k_ref, vt_ref, km_ref, vmt_ref, bias_ref, bm_ref, z_ref, g_ref,
                   o_ref, qt_ref, m_ref, acc_ref, p_ref, *, tq, tk, nk, ratio, nv, tb):
    i = pl.program_id(2)
    q = q_ref[...]
    row = lax.broadcasted_iota(jnp.int32, q.shape, 0)
    qt_ref[:, :tq] = jnp.where(row < A_QK_DIM, q, jnp.zeros_like(q))
    qt_ref[:, tq:] = jnp.where(row >= A_QK_DIM, q, jnp.zeros_like(q))
    qt = qt_ref[...]

    def both(fn):
        return jnp.concatenate([fn(slice(0, tq)), fn(slice(tq, 2 * tq))], axis=1)

    def pv_and_sum(vt, pb):
        vt1 = jnp.concatenate([vt, jnp.ones((SUM_ROWS, vt.shape[1]), BF16)], axis=0)
        return both(lambda c: jnp.dot(vt1, pb[:, c], preferred_element_type=F32))

    def raw_scores(j):
        ks = pl.multiple_of(j * tk, tk)
        return jnp.dot(k_ref[pl.ds(ks, tk), :], qt, preferred_element_type=F32)

    def bias2(j):
        bt = bias_ref[jnp.clip(j - i * ratio + 2, 0, nv - 1)]
        return jnp.concatenate([bt, bt], axis=1)

    def tile_scores(j):
        return raw_scores(j) + bias2(j)

    bm = bm_ref[...]
    s_meta = jnp.dot(km_ref[...], qt, preferred_element_type=F32) + jnp.concatenate([bm, bm], axis=1)
    m0 = jnp.max(s_meta, axis=0, keepdims=True)

    def init_from_meta():
        acc_ref[...] = pv_and_sum(vmt_ref[...], jnp.exp2(s_meta - m0).astype(BF16))

    def stage_p(j, slot):
        p_ref[slot] = jnp.exp2(tile_scores(j) - m0).astype(BF16)

    stage_p(0, 0)
    init_from_meta()

    def group(g, carry):
        pv = None
        for u in range(tb):
            j = g * tb + u
            nxt = j + 1 if u < tb - 1 else jnp.minimum(j + 1, nk - 1)
            stage_p(nxt, (u + 1) % 2)
            d = pv_and_sum(vt_ref[j], p_ref[u % 2])
            pv = d if pv is None else pv + d
        acc_ref[...] += pv
        return carry

    lax.fori_loop(0, nk // tb, group, 0)

    def finish():
        acc = acc_ref[...]
        l = acc[A_V_DIM:A_V_DIM + 1, :]
        inv = 1.0 / l
        ot = acc[:A_V_DIM, :tq] * inv[:, :tq] - lam_ref[0] * (acc[:A_V_DIM, tq:] * inv[:, tq:])
        ssq = jnp.sum(ot * ot, axis=0, keepdims=True)
        yt = ot * lax.rsqrt(ssq * (1.0 / A_V_DIM) + EPS)
        z = z_ref[...].astype(F32)
        o_ref[...] = (yt.T * g_ref[...] * (1.0 - LAM_INIT) * (z * jax.nn.sigmoid(z))).astype(BF16)
        return l, ssq

    l, ssq = finish()
    overflow = (jnp.max(jnp.where(l < OVERFLOW_GUARD, 0.0, 1.0))
                + jnp.max(jnp.where(ssq < F32_FINITE, 0.0, 1.0))) > 0.0

    @pl.when(overflow)
    def _():
        init_from_meta()
        m_ref[...] = m0

        def step(j, carry):
            s = tile_scores(j)
            m_old = m_ref[...]
            m_new = jnp.maximum(m_old, jnp.max(s, axis=0, keepdims=True))
            alpha = jnp.exp2(m_old - m_new)
            acc_ref[...] = alpha * acc_ref[...] + pv_and_sum(vt_ref[j], jnp.exp2(s - m_new).astype(BF16))
            m_ref[...] = m_new
            return carry

        lax.fori_loop(0, nk, step, 0)
        finish()


def _attn_a(lam, qat, ka, vat, ka_m, vat_m, bias, bias_m, za, subln_g, *, tq, tk):
    nb, s, _ = ka.shape
    nk = s // tk
    ratio = tq // tk
    nv = bias.shape[1]
    tb = next(c for c in (16, 8, 4, 2, 1) if nk % c == 0)
    assert nk % tb == 0
    head = lambda n: pl.BlockSpec((None, n, A_V_DIM), lambda b, h, i: (b, i, h))
    return pl.pallas_call(
        functools.partial(_attn_a_kernel, tq=tq, tk=tk, nk=nk, ratio=ratio, nv=nv, tb=tb),
        grid=(nb, A_HEADS, s // tq),
        in_specs=[
            pl.BlockSpec(memory_space=pltpu.SMEM),
            pl.BlockSpec((None, None, A_V_DIM, tq), lambda b, h, i: (b, i, h, 0)),
            pl.BlockSpec((None, s, A_V_DIM), lambda b, h, i: (b, 0, h)),
            pl.BlockSpec((None, nk, A_V_DIM, tk), lambda b, h, i: (b, 0, h, 0)),
            pl.BlockSpec((None, N_META, A_V_DIM), lambda b, h, i: (0, 0, h)),
            pl.BlockSpec((None, None, A_V_DIM, N_META), lambda b, h, i: (0, 0, h, 0)),
            pl.BlockSpec((None, nv, tk, tq), lambda b, h, i: (h, 0, 0, 0)),
            pl.BlockSpec((None, N_META, tq), lambda b, h, i: (h, 0, i)),
            head(tq),
            pl.BlockSpec((1, A_V_DIM), lambda b, h, i: (0, 0)),
        ],
        out_specs=head(tq),
        out_shape=jax.ShapeDtypeStruct((nb, s, A_WIDTH), BF16),
        scratch_shapes=[
            pltpu.VMEM((A_V_DIM, 2 * tq), BF16),
            pltpu.VMEM((1, 2 * tq), F32),
            pltpu.VMEM((A_V_DIM + SUM_ROWS, 2 * tq), F32),
            pltpu.VMEM((2, tk, 2 * tq), BF16),
        ],
        compiler_params=pltpu.CompilerParams(
            dimension_semantics=("arbitrary", "arbitrary", "arbitrary"),
            vmem_limit_bytes=VMEM_LIMIT_BYTES),
        name="attn_a",
    )(lam, qat, ka, vat, ka_m, vat_m, bias, bias_m, za, subln_g)


_BT = WINDOW
_BSUB = 4
_BKT = _BT + 2 * WINDOW
_PAD_FRONT = WINDOW - N_META
_B_SLABS = B_HEADS // 2
_B_GROUP = B_HEADS // B_KV_HEADS


def _attn_b_kernel(sink_ref, q_ref, *rest, n_sub_total):
    nkb = _BSUB + 2
    k_refs, v_refs = rest[:nkb], rest[nkb:2 * nkb]
    bias_ref, z_ref, o_ref, qt_ref = rest[2 * nkb:]
    i = pl.program_id(1)
    sink_row = jnp.concatenate(
        [jnp.full((1, _BT), sink_ref[h] * LOG2E, F32) for h in range(B_HEADS)], axis=1)
    zeros = jnp.zeros((B_HEAD_DIM, _BT), BF16)

    for t in range(_BSUB):
        it = i * _BSUB + t
        k = jnp.concatenate([r[...] for r in k_refs[t:t + 3]], axis=0)
        v = jnp.concatenate([r[...] for r in v_refs[t:t + 3]], axis=0)
        vt = jnp.concatenate([v.astype(F32).T.astype(BF16), jnp.ones((SUM_ROWS, _BKT), BF16)], axis=0)
        q = q_ref[t]
        for h in range(B_HEADS):
            blk = q[h * B_HEAD_DIM:(h + 1) * B_HEAD_DIM, :]
            pair = [blk, zeros] if h < _B_GROUP else [zeros, blk]
            qt_ref[t, :, h * _BT:(h + 1) * _BT] = jnp.concatenate(pair, axis=0)
        variant = (it == 0).astype(jnp.int32) + 2 * (it == n_sub_total - 1).astype(jnp.int32)
        s = jnp.dot(k, qt_ref[t], preferred_element_type=F32) + bias_ref[variant]
        m = jnp.maximum(jnp.max(s, axis=0, keepdims=True), sink_row)
        pv = jnp.dot(vt, jnp.exp2(s - m).astype(BF16), preferred_element_type=F32)
        l = pv[2 * B_HEAD_DIM:2 * B_HEAD_DIM + 1, :] + jnp.exp2(sink_row - m)
        ot = pv[:2 * B_HEAD_DIM, :] * (1.0 / l)
        for slab in range(_B_SLABS):
            g = (2 * slab) // _B_GROUP
            rows = slice(g * B_HEAD_DIM, (g + 1) * B_HEAD_DIM)
            x = jnp.concatenate([ot[rows, (2 * slab) * _BT:(2 * slab + 1) * _BT],
                                 ot[rows, (2 * slab + 1) * _BT:(2 * slab + 2) * _BT]], axis=0)
            cols = slice(slab * LANES, (slab + 1) * LANES)
            z = z_ref[t * _BT:(t + 1) * _BT, cols].astype(F32)
            o_ref[t * _BT:(t + 1) * _BT, cols] = (x.T * (z * jax.nn.sigmoid(z))).astype(BF16)


def _attn_b(sink, qbt, kb_ext, vb_ext, bias, zb):
    nb, s, _ = zb.shape
    tq = _BT * _BSUB
    row = pl.BlockSpec((None, tq, B_WIDTH), lambda b, i: (b, i, 0))
    nkb = _BSUB + 2
    kv = [pl.BlockSpec((None, WINDOW, B_KV_WIDTH), (lambda b, i, t=t: (b, _BSUB * i + t, 0)))
          for t in range(nkb)]
    return pl.pallas_call(
        functools.partial(_attn_b_kernel, n_sub_total=s // _BT),
        grid=(nb, s // tq),
        in_specs=[pl.BlockSpec(memory_space=pltpu.SMEM),
                  pl.BlockSpec((None, _BSUB, B_WIDTH, _BT), lambda b, i: (b, i, 0, 0))] + kv + kv
                 + [pl.BlockSpec((4, _BKT, B_HEADS * _BT), lambda b, i: (0, 0, 0),
                                 pipeline_mode=pl.Buffered(1)), row],
        out_specs=row,
        out_shape=jax.ShapeDtypeStruct((nb, s, B_WIDTH), BF16),
        scratch_shapes=[pltpu.VMEM((_BSUB, 2 * B_HEAD_DIM, B_HEADS * _BT), BF16)],
        compiler_params=pltpu.CompilerParams(
            dimension_semantics=("arbitrary", "arbitrary"), vmem_limit_bytes=VMEM_LIMIT_BYTES),
        name="attn_b",
    )(sink, qbt, *([kb_ext] * nkb), *([vb_ext] * nkb), bias, zb)


def _out_kernel(x_ref, a_ref, b_ref, ga_ref, gb_ref, woa_ref, wob_ref, wo_ref, g_ref, o_ref):
    ya = jnp.dot(a_ref[...], woa_ref[...], preferred_element_type=F32)
    yb = jnp.dot(b_ref[...], wob_ref[...], preferred_element_type=F32)
    mixed = (jax.nn.sigmoid(ga_ref[...].astype(F32)) * ya
             + jax.nn.sigmoid(gb_ref[...].astype(F32)) * yb)
    y = jnp.dot(mixed.astype(BF16), wo_ref[...], preferred_element_type=F32)
    y = y * lax.rsqrt(jnp.mean(y * y, axis=-1, keepdims=True) + EPS) * g_ref[...]
    o_ref[...] = x_ref[...] + y


def _output(x3, a, b, ga, gb, woa, wob, wo, g, *, tm):
    nb, s, d = x3.shape
    row = lambda n: pl.BlockSpec((None, tm, n), lambda bi, i: (bi, i, 0))
    const = lambda shape: pl.BlockSpec(shape, lambda bi, i: (0, 0), pipeline_mode=pl.Buffered(1))
    return pl.pallas_call(
        _out_kernel,
        grid=(nb, s // tm),
        in_specs=[row(d), row(A_WIDTH), row(B_WIDTH), row(d), row(d),
                  const((A_WIDTH, d)), const((B_WIDTH, d)), const((d, d)), const((1, d))],
        out_specs=row(d),
        out_shape=jax.ShapeDtypeStruct((nb, s, d), F32),
        compiler_params=pltpu.CompilerParams(
            dimension_semantics=("arbitrary", "arbitrary"), vmem_limit_bytes=VMEM_LIMIT_BYTES),
        name="out_proj",
    )(x3, a, b, ga, gb, woa, wob, wo, g)


def kernel(x, meta_tokens, rel_bias, pre_norm_g, w_in, b_in, lambda_q1, lambda_k1, lambda_q2, lambda_k2,
           subln_g, sink, w_out_a, w_out_b, w_out, post_norm_g):
    nb, s, d = x.shape
    tm = min(512, s)
    tq = min(512, s)
    tk = min(256, s)
    assert s % tm == 0 and s % tq == 0 and tq % tk == 0 and tm % tq == 0 and tm % (_BT * _BSUB) == 0

    w = w_in[0].astype(BF16)
    b = b_in[0][None, :]
    g_pre = pre_norm_g[0][None, :]
    lam = (jnp.exp(jnp.sum(lambda_q1[0].astype(F32) * lambda_k1[0].astype(F32)))
           - jnp.exp(jnp.sum(lambda_q2[0].astype(F32) * lambda_k2[0].astype(F32))) + LAM_INIT).reshape(1)

    qat, ka, vat, za, qbt, kb, vb, zb, ga, gb = _project(x, g_pre, w, b, tm=tm, tq=tq, tqb=_BT, tkv=tk)
    meta = _project(meta_tokens[None], g_pre, w, b, tm=N_META, tq=N_META, tqb=N_META, tkv=N_META)
    ka_m, vat_m, kb_m, vb_m = meta[1], meta[2], meta[5], meta[6]

    ratio = tq // tk
    nv = ratio + 4
    off = (jnp.arange(nv, dtype=jnp.int32) - 2)[:, None, None] * tk
    rel_a = off + jnp.arange(tk, dtype=jnp.int32)[None, :, None] - jnp.arange(tq, dtype=jnp.int32)[None, None, :]
    bias_a = _bias_tiles(rel_bias, _t5_bucket(rel_a).reshape(nv * tk, tq), head0=0, n_heads=A_HEADS,
                         tr=tk).reshape(A_HEADS, nv, tk, tq)
    rel_m = jnp.arange(N_META, dtype=jnp.int32)[:, None] - (jnp.arange(s, dtype=jnp.int32)[None, :] + N_META)
    bias_m = _bias_tiles(rel_bias, _t5_bucket(rel_m), head0=0, n_heads=A_HEADS, tr=N_META)
    key_r = jnp.arange(_BKT, dtype=jnp.int32)[:, None]
    rel_b = (key_r - WINDOW) - jnp.arange(_BT, dtype=jnp.int32)[None, :]
    in_win = jnp.abs(rel_b) <= WINDOW
    idx_b = jnp.concatenate([
        jnp.where(in_win & ((key_r >= _PAD_FRONT) | (not first)) & ((key_r < _BT + WINDOW) | (not last)),
                  _t5_bucket(rel_b), N_BUCKETS)
        for last in (False, True) for first in (False, True)], axis=0)
    bias_b = _bias_tiles(rel_bias, idx_b, head0=A_HEADS, n_heads=B_HEADS, tr=_BKT,
                         heads_on_lanes=True).reshape(4, _BKT, B_HEADS * _BT)

    oa = _attn_a(lam, qat, ka, vat, ka_m, vat_m, bias_a, bias_m, za, subln_g[0][None, :], tq=tq, tk=tk)

    def extend(tok, m):
        return jnp.concatenate([
            jnp.zeros((nb, _PAD_FRONT, B_KV_WIDTH), BF16),
            jnp.broadcast_to(m, (nb, N_META, B_KV_WIDTH)), tok,
            jnp.zeros((nb, WINDOW, B_KV_WIDTH), BF16)], axis=1)

    ob = _attn_b(sink[0], qbt, extend(kb, kb_m), extend(vb, vb_m), bias_b, zb)

    return _output(x, oa, ob, ga, gb, w_out_a[0].astype(BF16), w_out_b[0].astype(BF16),
                   w_out[0].astype(BF16), post_norm_g[0][None, :], tm=tm)
```

```python
import functools
import math

import jax
import jax.numpy as jnp
from jax import lax
from jax.experimental import pallas as pl
from jax.experimental.pallas import tpu as pltpu

N_META = 16
WINDOW = 128
A_HEADS = 4
A_QK_DIM = 64
A_V_DIM = 2 * A_QK_DIM
A_WIDTH = A_HEADS * A_V_DIM
B_HEADS = 8
B_KV_HEADS = 2
B_HEAD_DIM = 64
B_WIDTH = B_HEADS * B_HEAD_DIM
B_KV_WIDTH = B_KV_HEADS * B_HEAD_DIM
N_BUCKETS = 32
MAX_DISTANCE = 128
EPS = 1e-6
NEG = -1e30
LOG2E = math.log2(math.e)
LAM_INIT = 0.8 - 0.6 * math.exp(-0.3 * 0)

LANES = 128
OVERFLOW_GUARD = 2.0 ** 100
F32_FINITE = 3.0e38
SUM_ROWS = 16
VMEM_LIMIT_BYTES = 56 * 1024 * 1024

_COLS = (A_WIDTH, A_WIDTH, A_WIDTH, A_WIDTH, B_WIDTH, B_KV_WIDTH, B_KV_WIDTH, B_WIDTH)
BF16 = jnp.bfloat16
F32 = jnp.float32


def _t5_bucket(rel):
    half = N_BUCKETS // 2
    max_exact = half // 2
    ret = jnp.where(rel > 0, half, 0)
    n = jnp.abs(rel)
    nf = jnp.maximum(n, 1).astype(F32)
    large = max_exact + (jnp.log(nf / max_exact) / math.log(MAX_DISTANCE / max_exact)
                         * (half - max_exact)).astype(jnp.int32)
    large = jnp.minimum(large, half - 1)
    return ret + jnp.where(n < max_exact, n, large)


def _proj_kernel(x_ref, g_ref, w_ref, b_ref,
                 qat_ref, ka_ref, vat_ref, za_ref, qbt_ref, kb_ref, vb_ref, zb_ref, ga_ref, gb_ref,
                 *, d_model, tq, tqb, tkv):
    x = x_ref[...]
    hn = (x * lax.rsqrt(jnp.mean(x * x, axis=-1, keepdims=True) + EPS) * g_ref[...]).astype(BF16)

    def seg(c0, n):
        return jnp.dot(hn, w_ref[:, c0:c0 + n], preferred_element_type=F32) + b_ref[:, c0:c0 + n]

    qscale = (A_QK_DIM ** -0.5) * LOG2E
    c = 0
    qa = seg(c, A_WIDTH) * qscale; c += A_WIDTH
    for t in range(qat_ref.shape[0]):
        qat_ref[t] = qa[t * tq:(t + 1) * tq, :].T.astype(BF16)
    ka_ref[...] = seg(c, A_WIDTH).astype(BF16); c += A_WIDTH
    va = seg(c, A_WIDTH); c += A_WIDTH
    for t in range(vat_ref.shape[0]):
        vat_ref[t] = va[t * tkv:(t + 1) * tkv, :].T.astype(BF16)
    za_ref[...] = seg(c, A_WIDTH).astype(BF16); c += A_WIDTH
    qb = seg(c, B_WIDTH) * qscale; c += B_WIDTH
    for t in range(qbt_ref.shape[0]):
        qbt_ref[t] = qb[t * tqb:(t + 1) * tqb, :].T.astype(BF16)
    kb_ref[...] = seg(c, B_KV_WIDTH).astype(BF16); c += B_KV_WIDTH
    vb_ref[...] = seg(c, B_KV_WIDTH).astype(BF16); c += B_KV_WIDTH
    zb_ref[...] = seg(c, B_WIDTH).astype(BF16); c += B_WIDTH
    ga_ref[...] = seg(c, d_model).astype(BF16); c += d_model
    gb_ref[...] = seg(c, d_model).astype(BF16)


def _project(x3, g, w, b, *, tm, tq, tqb, tkv):
    nb, s, d = x3.shape
    d_in = w.shape[1]
    widths = _COLS + (d, d)
    nt = s // tm
    row = lambda n: pl.BlockSpec((None, tm, n), lambda bi, i: (bi, i, 0))
    const = lambda shape: pl.BlockSpec(shape, lambda bi, i: (0, 0), pipeline_mode=pl.Buffered(1))
    out_specs, out_shapes = [], []
    for idx, n in enumerate(widths):
        if idx in (0, 2, 4):
            tt = {0: tq, 2: tkv, 4: tqb}[idx]
            out_specs.append(pl.BlockSpec((None, tm // tt, n, tt), lambda bi, i: (bi, i, 0, 0)))
            out_shapes.append(jax.ShapeDtypeStruct((nb, s // tt, n, tt), BF16))
        else:
            out_specs.append(row(n))
            out_shapes.append(jax.ShapeDtypeStruct((nb, s, n), BF16))
    return pl.pallas_call(
        functools.partial(_proj_kernel, d_model=d, tq=tq, tqb=tqb, tkv=tkv),
        grid=(nb, nt),
        in_specs=[row(d), const((1, d)), const((d, d_in)), const((1, d_in))],
        out_specs=out_specs,
        out_shape=out_shapes,
        compiler_params=pltpu.CompilerParams(
            dimension_semantics=("arbitrary", "arbitrary"), vmem_limit_bytes=VMEM_LIMIT_BYTES),
        name="proj",
    )(x3, g, w, b)


def _bias_kernel(tab_ref, idx_ref, o_ref, *, head0):
    h = pl.program_id(0) + head0
    idx = idx_ref[...]
    acc = jnp.full(idx.shape, NEG, F32)
    for bkt in range(N_BUCKETS):
        acc = jnp.where(idx == bkt, tab_ref[bkt, h] * LOG2E, acc)
    o_ref[...] = acc


def _bias_tiles(tab, idx, *, head0, n_heads, tr, heads_on_lanes=False):
    r, c = idx.shape
    if heads_on_lanes:
        out_spec = pl.BlockSpec((tr, c), lambda h, i: (i, h))
        out_shape = jax.ShapeDtypeStruct((r, n_heads * c), F32)
    else:
        out_spec = pl.BlockSpec((None, tr, c), lambda h, i: (h, i, 0))
        out_shape = jax.ShapeDtypeStruct((n_heads, r, c), F32)
    return pl.pallas_call(
        functools.partial(_bias_kernel, head0=head0),
        grid=(n_heads, r // tr),
        in_specs=[pl.BlockSpec(memory_space=pltpu.SMEM),
                  pl.BlockSpec((tr, c), lambda h, i: (i, 0))],
        out_specs=out_spec,
        out_shape=out_shape,
        compiler_params=pltpu.CompilerParams(dimension_semantics=("arbitrary", "arbitrary")),
        name="bias_tiles",
    )(tab, idx)


def _attn_a_kernel(lam_ref, q_ref, k_ref, vt_ref, km_ref, vmt_ref, bias_ref, bm_ref, z_ref, g_ref,
                   o_ref, qt_ref, m_ref, acc_ref, p_ref, *, tq, tk, nk, ratio, nv, tb):
    i = pl.program_id(2)
    q = jnp.concatenate([q_ref[u] for u in range(q_ref.shape[0])], axis=1)
    row = lax.broadcasted_iota(jnp.int32, q.shape, 0)
    qt_ref[:, :tq] = jnp.where(row < A_QK_DIM, q, jnp.zeros_like(q))
    qt_ref[:, tq:] = jnp.where(row >= A_QK_DIM, q, jnp.zeros_like(q))
    qt = qt_ref[...]

    def both(fn):
        return jnp.concatenate([fn(slice(0, tq)), fn(slice(tq, 2 * tq))], axis=1)

    def pv_and_sum(vt, pb):
        vt1 = jnp.concatenate([vt, jnp.ones((SUM_ROWS, vt.shape[1]), BF16)], axis=0)
        return both(lambda c: jnp.dot(vt1, pb[:, c], preferred_element_type=F32))

    def raw_scores(j):
        ks = pl.multiple_of(j * tk, tk)
        return jnp.dot(k_ref[pl.ds(ks, tk), :], qt, preferred_element_type=F32)

    def bias2(j):
        bt = bias_ref[jnp.clip(j - i * ratio + 2, 0, nv - 1)]
        return jnp.concatenate([bt, bt], axis=1)

    def tile_scores(j):
        return raw_scores(j) + bias2(j)

    bm = bm_ref[...]
    s_meta = jnp.dot(km_ref[...], qt, preferred_element_type=F32) + jnp.concatenate([bm, bm], axis=1)
    m0 = jnp.max(s_meta, axis=0, keepdims=True)

    def init_from_meta():
        acc_ref[...] = pv_and_sum(vmt_ref[...], jnp.exp2(s_meta - m0).astype(BF16))

    def stage_p(j, slot):
        p_ref[slot] = jnp.exp2(tile_scores(j) - m0).astype(BF16)

    stage_p(0, 0)
    init_from_meta()

    def group(g, carry):
        pv = None
        for u in range(tb):
            j = g * tb + u
            nxt = j + 1 if u < tb - 1 else jnp.minimum(j + 1, nk - 1)
            stage_p(nxt, (u + 1) % 2)
            d = pv_and_sum(vt_ref[j], p_ref[u % 2])
            pv = d if pv is None else pv + d
        acc_ref[...] += pv
        return carry

    lax.fori_loop(0, nk // tb, group, 0)

    def finish():
        acc = acc_ref[...]
        l = acc[A_V_DIM:A_V_DIM + 1, :]
        inv = 1.0 / l
        ot = acc[:A_V_DIM, :tq] * inv[:, :tq] - lam_ref[0] * (acc[:A_V_DIM, tq:] * inv[:, tq:])
        ssq = jnp.sum(ot * ot, axis=0, keepdims=True)
        yt = ot * lax.rsqrt(ssq * (1.0 / A_V_DIM) + EPS)
        z = z_ref[...].astype(F32)
        o_ref[...] = (yt.T * g_ref[...] * (1.0 - LAM_INIT) * (z * jax.nn.sigmoid(z))).astype(BF16)
        return l, ssq

    l, ssq = finish()
    overflow = (jnp.max(jnp.where(l < OVERFLOW_GUARD, 0.0, 1.0))
                + jnp.max(jnp.where(ssq < F32_FINITE, 0.0, 1.0))) > 0.0

    @pl.when(overflow)
    def _():
        init_from_meta()
        m_ref[...] = m0

        def step(j, carry):
            s = tile_scores(j)
            m_old = m_ref[...]
            m_new = jnp.maximum(m_old, jnp.max(s, axis=0, keepdims=True))
            alpha = jnp.exp2(m_old - m_new)
            acc_ref[...] = alpha * acc_ref[...] + pv_and_sum(vt_ref[j], jnp.exp2(s - m_new).astype(BF16))
            m_ref[...] = m_new
            return carry

        lax.fori_loop(0, nk, step, 0)
        finish()


def _attn_a(lam, qat, ka, vat, ka_m, vat_m, bias, bias_m, za, subln_g, *, tq, tk):
    nb, s, _ = ka.shape
    tqs = qat.shape[-1]
    nk = s // tk
    ratio = tq // tk
    nv = bias.shape[1]
    tb = next(c for c in (8, 4, 2, 1) if nk % c == 0)
    assert nk % tb == 0
    head = lambda n: pl.BlockSpec((None, n, A_V_DIM), lambda b, h, i: (b, i, h))
    return pl.pallas_call(
        functools.partial(_attn_a_kernel, tq=tq, tk=tk, nk=nk, ratio=ratio, nv=nv, tb=tb),
        grid=(nb, A_HEADS, s // tq),
        in_specs=[
            pl.BlockSpec(memory_space=pltpu.SMEM),
            pl.BlockSpec((None, tq // tqs, A_V_DIM, tqs), lambda b, h, i: (b, i, h, 0)),
            pl.BlockSpec((None, s, A_V_DIM), lambda b, h, i: (b, 0, h)),
            pl.BlockSpec((None, nk, A_V_DIM, tk), lambda b, h, i: (b, 0, h, 0)),
            pl.BlockSpec((None, N_META, A_V_DIM), lambda b, h, i: (0, 0, h)),
            pl.BlockSpec((None, None, A_V_DIM, N_META), lambda b, h, i: (0, 0, h, 0)),
            pl.BlockSpec((None, nv, tk, tq), lambda b, h, i: (h, 0, 0, 0)),
            pl.BlockSpec((None, N_META, tq), lambda b, h, i: (h, 0, i)),
            head(tq),
            pl.BlockSpec((1, A_V_DIM), lambda b, h, i: (0, 0)),
        ],
        out_specs=head(tq),
        out_shape=jax.ShapeDtypeStruct((nb, s, A_WIDTH), BF16),
        scratch_shapes=[
            pltpu.VMEM((A_V_DIM, 2 * tq), BF16),
            pltpu.VMEM((1, 2 * tq), F32),
            pltpu.VMEM((A_V_DIM + SUM_ROWS, 2 * tq), F32),
            pltpu.VMEM((2, tk, 2 * tq), BF16),
        ],
        compiler_params=pltpu.CompilerParams(
            dimension_semantics=("arbitrary", "arbitrary", "arbitrary"),
            vmem_limit_bytes=VMEM_LIMIT_BYTES),
        name="attn_a",
    )(lam, qat, ka, vat, ka_m, vat_m, bias, bias_m, za, subln_g)


_BT = WINDOW
_BSUB = 4
_BKT = _BT + 2 * WINDOW
_PAD_FRONT = WINDOW - N_META
_B_SLABS = B_HEADS // 2
_B_GROUP = B_HEADS // B_KV_HEADS


def _attn_b_kernel(sink_ref, q_ref, *rest, n_sub_total):
    nkb = _BSUB + 2
    k_refs, v_refs = rest[:nkb], rest[nkb:2 * nkb]
    bias_ref, z_ref, o_ref, qt_ref = rest[2 * nkb:]
    i = pl.program_id(1)
    sink_row = jnp.concatenate(
        [jnp.full((1, _BT), sink_ref[h] * LOG2E, F32) for h in range(B_HEADS)], axis=1)
    zeros = jnp.zeros((B_HEAD_DIM, _BT), BF16)

    for t in range(_BSUB):
        it = i * _BSUB + t
        k = jnp.concatenate([r[...] for r in k_refs[t:t + 3]], axis=0)
        v = jnp.concatenate([r[...] for r in v_refs[t:t + 3]], axis=0)
        vt = jnp.concatenate([v.astype(F32).T.astype(BF16), jnp.ones((SUM_ROWS, _BKT), BF16)], axis=0)
        q = q_ref[t]
        for h in range(B_HEADS):
            blk = q[h * B_HEAD_DIM:(h + 1) * B_HEAD_DIM, :]
            pair = [blk, zeros] if h < _B_GROUP else [zeros, blk]
            qt_ref[t, :, h * _BT:(h + 1) * _BT] = jnp.concatenate(pair, axis=0)
        variant = (it == 0).astype(jnp.int32) + 2 * (it == n_sub_total - 1).astype(jnp.int32)
        s = jnp.dot(k, qt_ref[t], preferred_element_type=F32) + bias_ref[variant]
        m = jnp.maximum(jnp.max(s, axis=0, keepdims=True), sink_row)
        pv = jnp.dot(vt, jnp.exp2(s - m).astype(BF16), preferred_element_type=F32)
        l = pv[2 * B_HEAD_DIM:2 * B_HEAD_DIM + 1, :] + jnp.exp2(sink_row - m)
        ot = pv[:2 * B_HEAD_DIM, :] * (1.0 / l)
        for slab in range(_B_SLABS):
            g = (2 * slab) // _B_GROUP
            rows = slice(g * B_HEAD_DIM, (g + 1) * B_HEAD_DIM)
            x = jnp.concatenate([ot[rows, (2 * slab) * _BT:(2 * slab + 1) * _BT],
                                 ot[rows, (2 * slab + 1) * _BT:(2 * slab + 2) * _BT]], axis=0)
            cols = slice(slab * LANES, (slab + 1) * LANES)
            z = z_ref[t * _BT:(t + 1) * _BT, cols].astype(F32)
            o_ref[t * _BT:(t + 1) * _BT, cols] = (x.T * (z * jax.nn.sigmoid(z))).astype(BF16)


def _attn_b(sink, qbt, kb_ext, vb_ext, bias, zb):
    nb, s, _ = zb.shape
    tq = _BT * _BSUB
    row = pl.BlockSpec((None, tq, B_WIDTH), lambda b, i: (b, i, 0))
    nkb = _BSUB + 2
    kv = [pl.BlockSpec((None, WINDOW, B_KV_WIDTH), (lambda b, i, t=t: (b, _BSUB * i + t, 0)))
          for t in range(nkb)]
    return pl.pallas_call(
        functools.partial(_attn_b_kernel, n_sub_total=s // _BT),
        grid=(nb, s // tq),
        in_specs=[pl.BlockSpec(memory_space=pltpu.SMEM),
                  pl.BlockSpec((None, _BSUB, B_WIDTH, _BT), lambda b, i: (b, i, 0, 0))] + kv + kv
                 + [pl.BlockSpec((4, _BKT, B_HEADS * _BT), lambda b, i: (0, 0, 0),
                                 pipeline_mode=pl.Buffered(1)), row],
        out_specs=row,
        out_shape=jax.ShapeDtypeStruct((nb, s, B_WIDTH), BF16),
        scratch_shapes=[pltpu.VMEM((_BSUB, 2 * B_HEAD_DIM, B_HEADS * _BT), BF16)],
        compiler_params=pltpu.CompilerParams(
            dimension_semantics=("arbitrary", "arbitrary"), vmem_limit_bytes=VMEM_LIMIT_BYTES),
        name="attn_b",
    )(sink, qbt, *([kb_ext] * nkb), *([vb_ext] * nkb), bias, zb)


def _out_kernel(x_ref, a_ref, b_ref, ga_ref, gb_ref, woa_ref, wob_ref, wo_ref, g_ref, o_ref):
    ya = jnp.dot(a_ref[...], woa_ref[...], preferred_element_type=F32)
    yb = jnp.dot(b_ref[...], wob_ref[...], preferred_element_type=F32)
    mixed = (jax.nn.sigmoid(ga_ref[...].astype(F32)) * ya
             + jax.nn.sigmoid(gb_ref[...].astype(F32)) * yb)
    y = jnp.dot(mixed.astype(BF16), wo_ref[...], preferred_element_type=F32)
    y = y * lax.rsqrt(jnp.mean(y * y, axis=-1, keepdims=True) + EPS) * g_ref[...]
    o_ref[...] = x_ref[...] + y


def _output(x3, a, b, ga, gb, woa, wob, wo, g, *, tm):
    nb, s, d = x3.shape
    row = lambda n: pl.BlockSpec((None, tm, n), lambda bi, i: (bi, i, 0))
    const = lambda shape: pl.BlockSpec(shape, lambda bi, i: (0, 0), pipeline_mode=pl.Buffered(1))
    return pl.pallas_call(
        _out_kernel,
        grid=(nb, s // tm),
        in_specs=[row(d), row(A_WIDTH), row(B_WIDTH), row(d), row(d),
                  const((A_WIDTH, d)), const((B_WIDTH, d)), const((d, d)), const((1, d))],
        out_specs=row(d),
        out_shape=jax.ShapeDtypeStruct((nb, s, d), F32),
        compiler_params=pltpu.CompilerParams(
            dimension_semantics=("arbitrary", "arbitrary"), vmem_limit_bytes=VMEM_LIMIT_BYTES),
        name="out_proj",
    )(x3, a, b, ga, gb, woa, wob, wo, g)


def kernel(x, meta_tokens, rel_bias, pre_norm_g, w_in, b_in, lambda_q1, lambda_k1, lambda_q2, lambda_k2,
           subln_g, sink, w_out_a, w_out_b, w_out, post_norm_g):
    nb, s, d = x.shape
    tm = min(512, s)
    tq = min(1024, s)
    tk = min(256, s)
    assert s % tm == 0 and s % tq == 0 and tq % tk == 0 and max(tq, tm) % min(tq, tm) == 0 and tm % (_BT * _BSUB) == 0

    w = w_in[0].astype(BF16)
    b = b_in[0][None, :]
    g_pre = pre_norm_g[0][None, :]
    lam = (jnp.exp(jnp.sum(lambda_q1[0].astype(F32) * lambda_k1[0].astype(F32)))
           - jnp.exp(jnp.sum(lambda_q2[0].astype(F32) * lambda_k2[0].astype(F32))) + LAM_INIT).reshape(1)

    qat, ka, vat, za, qbt, kb, vb, zb, ga, gb = _project(x, g_pre, w, b, tm=tm, tq=min(tq, tm), tqb=_BT, tkv=tk)
    meta = _project(meta_tokens[None], g_pre, w, b, tm=N_META, tq=N_META, tqb=N_META, tkv=N_META)
    ka_m, vat_m, kb_m, vb_m = meta[1], meta[2], meta[5], meta[6]

    ratio = tq // tk
    nv = ratio + 4
    off = (jnp.arange(nv, dtype=jnp.int32) - 2)[:, None, None] * tk
    rel_a = off + jnp.arange(tk, dtype=jnp.int32)[None, :, None] - jnp.arange(tq, dtype=jnp.int32)[None, None, :]
    bias_a = _bias_tiles(rel_bias, _t5_bucket(rel_a).reshape(nv * tk, tq), head0=0, n_heads=A_HEADS,
                         tr=tk).reshape(A_HEADS, nv, tk, tq)
    rel_m = jnp.arange(N_META, dtype=jnp.int32)[:, None] - (jnp.arange(s, dtype=jnp.int32)[None, :] + N_META)
    bias_m = _bias_tiles(rel_bias, _t5_bucket(rel_m), head0=0, n_heads=A_HEADS, tr=N_META)
    key_r = jnp.arange(_BKT, dtype=jnp.int32)[:, None]
    rel_b = (key_r - WINDOW) - jnp.arange(_BT, dtype=jnp.int32)[None, :]
    in_win = jnp.abs(rel_b) <= WINDOW
    idx_b = jnp.concatenate([
        jnp.where(in_win & ((key_r >= _PAD_FRONT) | (not first)) & ((key_r < _BT + WINDOW) | (not last)),
                  _t5_bucket(rel_b), N_BUCKETS)
        for last in (False, True) for first in (False, True)], axis=0)
    bias_b = _bias_tiles(rel_bias, idx_b, head0=A_HEADS, n_heads=B_HEADS, tr=_BKT,
                         heads_on_lanes=True).reshape(4, _BKT, B_HEADS * _BT)

    oa = _attn_a(lam, qat, ka, vat, ka_m, vat_m, bias_a, bias_m, za, subln_g[0][None, :], tq=tq, tk=tk)

    def extend(tok, m):
        return jnp.concatenate([
            jnp.zeros((nb, _PAD_FRONT, B_KV_WIDTH), BF16),
            jnp.broadcast_to(m, (nb, N_META, B_KV_WIDTH)), tok,
            jnp.zeros((nb, WINDOW, B_KV_WIDTH), BF16)], axis=1)

    ob = _attn_b(sink[0], qbt, extend(kb, kb_m), extend(vb, vb_m), bias_b, zb)

    return _output(x, oa, ob, ga, gb, w_out_a[0].astype(BF16), w_out_b[0].astype(BF16),
                   w_out[0].astype(BF16), post_norm_g[0][None, :], tm=min(1024, s))
```

```python
import functools
import math

import jax
import jax.numpy as jnp
from jax import lax
from jax.experimental import pallas as pl
from jax.experimental.pallas import tpu as pltpu

N_META = 16
WINDOW = 128
A_HEADS = 4
A_QK_DIM = 64
A_V_DIM = 2 * A_QK_DIM
A_WIDTH = A_HEADS * A_V_DIM
B_HEADS = 8
B_KV_HEADS = 2
B_HEAD_DIM = 64
B_WIDTH = B_HEADS * B_HEAD_DIM
B_KV_WIDTH = B_KV_HEADS * B_HEAD_DIM
N_BUCKETS = 32
MAX_DISTANCE = 128
EPS = 1e-6
NEG = -1e30
LOG2E = math.log2(math.e)
LAM_INIT = 0.8 - 0.6 * math.exp(-0.3 * 0)

LANES = 128
OVERFLOW_GUARD = 2.0 ** 100
F32_FINITE = 3.0e38
SUM_ROWS = 16
VMEM_LIMIT_BYTES = 56 * 1024 * 1024

_COLS = (A_WIDTH, A_WIDTH, A_WIDTH, A_WIDTH, B_WIDTH, B_KV_WIDTH, B_KV_WIDTH, B_WIDTH)
BF16 = jnp.bfloat16
F32 = jnp.float32


def _t5_bucket(rel):
    half = N_BUCKETS // 2
    max_exact = half // 2
    ret = jnp.where(rel > 0, half, 0)
    n = jnp.abs(rel)
    nf = jnp.maximum(n, 1).astype(F32)
    large = max_exact + (jnp.log(nf / max_exact) / math.log(MAX_DISTANCE / max_exact)
                         * (half - max_exact)).astype(jnp.int32)
    large = jnp.minimum(large, half - 1)
    return ret + jnp.where(n < max_exact, n, large)


def _proj_kernel(x_ref, g_ref, w_ref, b_ref,
                 qat_ref, ka_ref, vat_ref, za_ref, qbt_ref, kb_ref, vb_ref, zb_ref, ga_ref, gb_ref,
                 *, d_model, tq, tqb, tkv):
    x = x_ref[...]
    hn = (x * lax.rsqrt(jnp.mean(x * x, axis=-1, keepdims=True) + EPS) * g_ref[...]).astype(BF16)

    def seg(c0, n):
        return jnp.dot(hn, w_ref[:, c0:c0 + n], preferred_element_type=F32) + b_ref[:, c0:c0 + n]

    qscale = (A_QK_DIM ** -0.5) * LOG2E
    c = 0
    qa = seg(c, A_WIDTH) * qscale; c += A_WIDTH
    for t in range(qat_ref.shape[0]):
        qat_ref[t] = qa[t * tq:(t + 1) * tq, :].T.astype(BF16)
    ka_ref[...] = seg(c, A_WIDTH).astype(BF16); c += A_WIDTH
    va = seg(c, A_WIDTH); c += A_WIDTH
    for t in range(vat_ref.shape[0]):
        vat_ref[t] = va[t * tkv:(t + 1) * tkv, :].T.astype(BF16)
    za_ref[...] = seg(c, A_WIDTH).astype(BF16); c += A_WIDTH
    qb = seg(c, B_WIDTH) * qscale; c += B_WIDTH
    for t in range(qbt_ref.shape[0]):
        qbt_ref[t] = qb[t * tqb:(t + 1) * tqb, :].T.astype(BF16)
    kb_ref[...] = seg(c, B_KV_WIDTH).astype(BF16); c += B_KV_WIDTH
    vb_ref[...] = seg(c, B_KV_WIDTH).astype(BF16); c += B_KV_WIDTH
    zb_ref[...] = seg(c, B_WIDTH).astype(BF16); c += B_WIDTH
    ga_ref[...] = seg(c, d_model).astype(BF16); c += d_model
    gb_ref[...] = seg(c, d_model).astype(BF16)


def _project(x3, g, w, b, *, tm, tq, tqb, tkv):
    nb, s, d = x3.shape
    d_in = w.shape[1]
    widths = _COLS + (d, d)
    nt = s // tm
    row = lambda n: pl.BlockSpec((None, tm, n), lambda bi, i: (bi, i, 0))
    const = lambda shape: pl.BlockSpec(shape, lambda bi, i: (0, 0), pipeline_mode=pl.Buffered(1))
    out_specs, out_shapes = [], []
    for idx, n in enumerate(widths):
        if idx in (0, 2, 4):
            tt = {0: tq, 2: tkv, 4: tqb}[idx]
            out_specs.append(pl.BlockSpec((None, tm // tt, n, tt), lambda bi, i: (bi, i, 0, 0)))
            out_shapes.append(jax.ShapeDtypeStruct((nb, s // tt, n, tt), BF16))
        else:
            out_specs.append(row(n))
            out_shapes.append(jax.ShapeDtypeStruct((nb, s, n), BF16))
    return pl.pallas_call(
        functools.partial(_proj_kernel, d_model=d, tq=tq, tqb=tqb, tkv=tkv),
        grid=(nb, nt),
        in_specs=[row(d), const((1, d)), const((d, d_in)), const((1, d_in))],
        out_specs=out_specs,
        out_shape=out_shapes,
        compiler_params=pltpu.CompilerParams(
            dimension_semantics=("arbitrary", "arbitrary"), vmem_limit_bytes=VMEM_LIMIT_BYTES),
        name="proj",
    )(x3, g, w, b)


def _bias_kernel(tab_ref, idx_ref, o_ref, *, head0):
    h = pl.program_id(0) + head0
    idx = idx_ref[...]
    acc = jnp.full(idx.shape, NEG, F32)
    for bkt in range(N_BUCKETS):
        acc = jnp.where(idx == bkt, tab_ref[bkt, h] * LOG2E, acc)
    o_ref[...] = acc


def _bias_tiles(tab, idx, *, head0, n_heads, tr, heads_on_lanes=False):
    r, c = idx.shape
    if heads_on_lanes:
        out_spec = pl.BlockSpec((tr, c), lambda h, i: (i, h))
        out_shape = jax.ShapeDtypeStruct((r, n_heads * c), F32)
    else:
        out_spec = pl.BlockSpec((None, tr, c), lambda h, i: (h, i, 0))
        out_shape = jax.ShapeDtypeStruct((n_heads, r, c), F32)
    return pl.pallas_call(
        functools.partial(_bias_kernel, head0=head0),
        grid=(n_heads, r // tr),
        in_specs=[pl.BlockSpec(memory_space=pltpu.SMEM),
                  pl.BlockSpec((tr, c), lambda h, i: (i, 0))],
        out_specs=out_spec,
        out_shape=out_shape,
        compiler_params=pltpu.CompilerParams(dimension_semantics=("arbitrary", "arbitrary")),
        name="bias_tiles",
    )(tab, idx)


def _attn_a_kernel(lam_ref, q_ref, k_ref, vt_ref, km_ref, vmt_ref, bias_ref, bm_ref, z_ref, g_ref,
                   o_ref, qt_ref, m_ref, acc_ref, p_ref, *, tq, tk, nk, nv, tb):
    i = pl.program_id(2)
    q = jnp.concatenate([q_ref[u] for u in range(q_ref.shape[0])], axis=1)
    row = lax.broadcasted_iota(jnp.int32, q.shape, 0)
    qt_ref[:, :tq] = jnp.where(row < A_QK_DIM, q, jnp.zeros_like(q))
    qt_ref[:, tq:] = jnp.where(row >= A_QK_DIM, q, jnp.zeros_like(q))
    qt = qt_ref[...]

    def both(fn):
        return jnp.concatenate([fn(slice(0, tq)), fn(slice(tq, 2 * tq))], axis=1)

    def pv_and_sum(vt, pb):
        vt1 = jnp.concatenate([vt, jnp.ones((SUM_ROWS, vt.shape[1]), BF16)], axis=0)
        return both(lambda c: jnp.dot(vt1, pb[:, c], preferred_element_type=F32))

    def raw_scores(j):
        ks = pl.multiple_of(j * tk, tk)
        return jnp.dot(k_ref[pl.ds(ks, tk), :], qt, preferred_element_type=F32)

    def bias2(j):
        tbw = bias_ref.shape[-1]
        pieces = [bias_ref[jnp.clip(j - (i * (tq // tbw) + u) * (tbw // tk) + 2, 0, nv - 1)]
                  for u in range(tq // tbw)]
        return jnp.concatenate(pieces + pieces, axis=1)

    def tile_scores(j):
        return raw_scores(j) + bias2(j)

    bm = bm_ref[...]
    s_meta = jnp.dot(km_ref[...], qt, preferred_element_type=F32) + jnp.concatenate([bm, bm], axis=1)
    m0 = jnp.max(s_meta, axis=0, keepdims=True)

    def init_from_meta():
        acc_ref[...] = pv_and_sum(vmt_ref[...], jnp.exp2(s_meta - m0).astype(BF16))

    def stage_p(j, slot):
        p_ref[slot] = jnp.exp2(tile_scores(j) - m0).astype(BF16)

    stage_p(0, 0)
    init_from_meta()

    def group(g, carry):
        pv = None
        for u in range(tb):
            j = g * tb + u
            nxt = j + 1 if u < tb - 1 else jnp.minimum(j + 1, nk - 1)
            stage_p(nxt, (u + 1) % 2)
            d = pv_and_sum(vt_ref[j], p_ref[u % 2])
            pv = d if pv is None else pv + d
        acc_ref[...] += pv
        return carry

    lax.fori_loop(0, nk // tb, group, 0)

    def finish():
        acc = acc_ref[...]
        l = acc[A_V_DIM:A_V_DIM + 1, :]
        inv = 1.0 / l
        ot = acc[:A_V_DIM, :tq] * inv[:, :tq] - lam_ref[0] * (acc[:A_V_DIM, tq:] * inv[:, tq:])
        ssq = jnp.sum(ot * ot, axis=0, keepdims=True)
        yt = ot * lax.rsqrt(ssq * (1.0 / A_V_DIM) + EPS)
        z = z_ref[...].astype(F32)
        o_ref[...] = (yt.T * g_ref[...] * (1.0 - LAM_INIT) * (z * jax.nn.sigmoid(z))).astype(BF16)
        return l, ssq

    l, ssq = finish()
    overflow = (jnp.max(jnp.where(l < OVERFLOW_GUARD, 0.0, 1.0))
                + jnp.max(jnp.where(ssq < F32_FINITE, 0.0, 1.0))) > 0.0

    @pl.when(overflow)
    def _():
        init_from_meta()
        m_ref[...] = m0

        def step(j, carry):
            s = tile_scores(j)
            m_old = m_ref[...]
            m_new = jnp.maximum(m_old, jnp.max(s, axis=0, keepdims=True))
            alpha = jnp.exp2(m_old - m_new)
            acc_ref[...] = alpha * acc_ref[...] + pv_and_sum(vt_ref[j], jnp.exp2(s - m_new).astype(BF16))
            m_ref[...] = m_new
            return carry

        lax.fori_loop(0, nk, step, 0)
        finish()


def _attn_a(lam, qat, ka, vat, ka_m, vat_m, bias, bias_m, za, subln_g, *, tq, tk):
    nb, s, _ = ka.shape
    tqs = qat.shape[-1]
    nk = s // tk
    nv, tbw = bias.shape[1], bias.shape[3]
    tb = next(c for c in (8, 4, 2, 1) if nk % c == 0)
    assert nk % tb == 0
    head = lambda n: pl.BlockSpec((None, n, A_V_DIM), lambda b, h, i: (b, i, h))
    return pl.pallas_call(
        functools.partial(_attn_a_kernel, tq=tq, tk=tk, nk=nk, nv=nv, tb=tb),
        grid=(nb, A_HEADS, s // tq),
        in_specs=[
            pl.BlockSpec(memory_space=pltpu.SMEM),
            pl.BlockSpec((None, tq // tqs, A_V_DIM, tqs), lambda b, h, i: (b, i, h, 0)),
            pl.BlockSpec((None, s, A_V_DIM), lambda b, h, i: (b, 0, h)),
            pl.BlockSpec((None, nk, A_V_DIM, tk), lambda b, h, i: (b, 0, h, 0)),
            pl.BlockSpec((None, N_META, A_V_DIM), lambda b, h, i: (0, 0, h)),
            pl.BlockSpec((None, None, A_V_DIM, N_META), lambda b, h, i: (0, 0, h, 0)),
            pl.BlockSpec((None, nv, tk, tbw), lambda b, h, i: (h, 0, 0, 0)),
            pl.BlockSpec((None, N_META, tq), lambda b, h, i: (h, 0, i)),
            head(tq),
            pl.BlockSpec((1, A_V_DIM), lambda b, h, i: (0, 0)),
        ],
        out_specs=head(tq),
        out_shape=jax.ShapeDtypeStruct((nb, s, A_WIDTH), BF16),
        scratch_shapes=[
            pltpu.VMEM((A_V_DIM, 2 * tq), BF16),
            pltpu.VMEM((1, 2 * tq), F32),
            pltpu.VMEM((A_V_DIM + SUM_ROWS, 2 * tq), F32),
            pltpu.VMEM((2, tk, 2 * tq), BF16),
        ],
        compiler_params=pltpu.CompilerParams(
            dimension_semantics=("arbitrary", "arbitrary", "arbitrary"),
            vmem_limit_bytes=VMEM_LIMIT_BYTES),
        name="attn_a",
    )(lam, qat, ka, vat, ka_m, vat_m, bias, bias_m, za, subln_g)


_BT = WINDOW
_BSUB = 8
_BKT = _BT + 2 * WINDOW
_PAD_FRONT = WINDOW - N_META
_B_SLABS = B_HEADS // 2
_B_GROUP = B_HEADS // B_KV_HEADS


def _attn_b_kernel(sink_ref, q_ref, *rest, n_sub_total):
    nkb = _BSUB + 2
    k_refs, v_refs = rest[:nkb], rest[nkb:2 * nkb]
    bias_ref, z_ref, o_ref, qt_ref = rest[2 * nkb:]
    i = pl.program_id(1)
    sink_row = jnp.concatenate(
        [jnp.full((1, _BT), sink_ref[h] * LOG2E, F32) for h in range(B_HEADS)], axis=1)
    zeros = jnp.zeros((B_HEAD_DIM, _BT), BF16)

    def scores(t):
        it = i * _BSUB + t
        k = jnp.concatenate([r[...] for r in k_refs[t:t + 3]], axis=0)
        q = q_ref[t]
        for h in range(B_HEADS):
            blk = q[h * B_HEAD_DIM:(h + 1) * B_HEAD_DIM, :]
            pair = [blk, zeros] if h < _B_GROUP else [zeros, blk]
            qt_ref[t, :, h * _BT:(h + 1) * _BT] = jnp.concatenate(pair, axis=0)
        variant = (it == 0).astype(jnp.int32) + 2 * (it == n_sub_total - 1).astype(jnp.int32)
        return jnp.dot(k, qt_ref[t], preferred_element_type=F32) + bias_ref[variant]

    def finish(t, s):
        v = jnp.concatenate([r[...] for r in v_refs[t:t + 3]], axis=0)
        vt = jnp.concatenate([v.astype(F32).T.astype(BF16), jnp.ones((SUM_ROWS, _BKT), BF16)], axis=0)
        m = jnp.maximum(jnp.max(s, axis=0, keepdims=True), sink_row)
        pv = jnp.dot(vt, jnp.exp2(s - m).astype(BF16), preferred_element_type=F32)
        l = pv[2 * B_HEAD_DIM:2 * B_HEAD_DIM + 1, :] + jnp.exp2(sink_row - m)
        ot = pv[:2 * B_HEAD_DIM, :] * (1.0 / l)
        for slab in range(_B_SLABS):
            g = (2 * slab) // _B_GROUP
            rows = slice(g * B_HEAD_DIM, (g + 1) * B_HEAD_DIM)
            x = jnp.concatenate([ot[rows, (2 * slab) * _BT:(2 * slab + 1) * _BT],
                                 ot[rows, (2 * slab + 1) * _BT:(2 * slab + 2) * _BT]], axis=0)
            cols = slice(slab * LANES, (slab + 1) * LANES)
            z = z_ref[t * _BT:(t + 1) * _BT, cols].astype(F32)
            o_ref[t * _BT:(t + 1) * _BT, cols] = (x.T * (z * jax.nn.sigmoid(z))).astype(BF16)

    s_cur = scores(0)
    for t in range(_BSUB):
        s_nxt = scores(t + 1) if t + 1 < _BSUB else None
        finish(t, s_cur)
        s_cur = s_nxt


def _attn_b(sink, qbt, kb_ext, vb_ext, bias, zb):
    nb, s, _ = zb.shape
    tq = _BT * _BSUB
    row = pl.BlockSpec((None, tq, B_WIDTH), lambda b, i: (b, i, 0))
    nkb = _BSUB + 2
    kv = [pl.BlockSpec((None, WINDOW, B_KV_WIDTH), (lambda b, i, t=t: (b, _BSUB * i + t, 0)))
          for t in range(nkb)]
    return pl.pallas_call(
        functools.partial(_attn_b_kernel, n_sub_total=s // _BT),
        grid=(nb, s // tq),
        in_specs=[pl.BlockSpec(memory_space=pltpu.SMEM),
                  pl.BlockSpec((None, _BSUB, B_WIDTH, _BT), lambda b, i: (b, i, 0, 0))] + kv + kv
                 + [pl.BlockSpec((4, _BKT, B_HEADS * _BT), lambda b, i: (0, 0, 0),
                                 pipeline_mode=pl.Buffered(1)), row],
        out_specs=row,
        out_shape=jax.ShapeDtypeStruct((nb, s, B_WIDTH), BF16),
        scratch_shapes=[pltpu.VMEM((_BSUB, 2 * B_HEAD_DIM, B_HEADS * _BT), BF16)],
        compiler_params=pltpu.CompilerParams(
            dimension_semantics=("arbitrary", "arbitrary"), vmem_limit_bytes=VMEM_LIMIT_BYTES),
        name="attn_b",
    )(sink, qbt, *([kb_ext] * nkb), *([vb_ext] * nkb), bias, zb)


def _out_kernel(x_ref, a_ref, b_ref, ga_ref, gb_ref, woa_ref, wob_ref, wo_ref, g_ref, o_ref):
    ya = jnp.dot(a_ref[...], woa_ref[...], preferred_element_type=F32)
    yb = jnp.dot(b_ref[...], wob_ref[...], preferred_element_type=F32)
    mixed = (jax.nn.sigmoid(ga_ref[...].astype(F32)) * ya
             + jax.nn.sigmoid(gb_ref[...].astype(F32)) * yb)
    y = jnp.dot(mixed.astype(BF16), wo_ref[...], preferred_element_type=F32)
    y = y * lax.rsqrt(jnp.mean(y * y, axis=-1, keepdims=True) + EPS) * g_ref[...]
    o_ref[...] = x_ref[...] + y


def _output(x3, a, b, ga, gb, woa, wob, wo, g, *, tm):
    nb, s, d = x3.shape
    row = lambda n: pl.BlockSpec((None, tm, n), lambda bi, i: (bi, i, 0))
    const = lambda shape: pl.BlockSpec(shape, lambda bi, i: (0, 0), pipeline_mode=pl.Buffered(1))
    return pl.pallas_call(
        _out_kernel,
        grid=(nb, s // tm),
        in_specs=[row(d), row(A_WIDTH), row(B_WIDTH), row(d), row(d),
                  const((A_WIDTH, d)), const((B_WIDTH, d)), const((d, d)), const((1, d))],
        out_specs=row(d),
        out_shape=jax.ShapeDtypeStruct((nb, s, d), F32),
        compiler_params=pltpu.CompilerParams(
            dimension_semantics=("arbitrary", "arbitrary"), vmem_limit_bytes=VMEM_LIMIT_BYTES),
        name="out_proj",
    )(x3, a, b, ga, gb, woa, wob, wo, g)


def kernel(x, meta_tokens, rel_bias, pre_norm_g, w_in, b_in, lambda_q1, lambda_k1, lambda_q2, lambda_k2,
           subln_g, sink, w_out_a, w_out_b, w_out, post_norm_g):
    nb, s, d = x.shape
    tm = min(512, s)
    tq = min(1024, s)
    tk = min(256, s)
    assert s % tm == 0 and s % tq == 0 and tq % tk == 0 and max(tq, tm) % min(tq, tm) == 0 and tm % _BT == 0 and s % (_BT * _BSUB) == 0

    w = w_in[0].astype(BF16)
    b = b_in[0][None, :]
    g_pre = pre_norm_g[0][None, :]
    lam = (jnp.exp(jnp.sum(lambda_q1[0].astype(F32) * lambda_k1[0].astype(F32)))
           - jnp.exp(jnp.sum(lambda_q2[0].astype(F32) * lambda_k2[0].astype(F32))) + LAM_INIT).reshape(1)

    qat, ka, vat, za, qbt, kb, vb, zb, ga, gb = _project(x, g_pre, w, b, tm=tm, tq=min(tq, tm), tqb=_BT, tkv=tk)
    meta = _project(meta_tokens[None], g_pre, w, b, tm=N_META, tq=N_META, tqb=N_META, tkv=N_META)
    ka_m, vat_m, kb_m, vb_m = meta[1], meta[2], meta[5], meta[6]

    tbw = tk
    nv = tbw // tk + 4
    off = (jnp.arange(nv, dtype=jnp.int32) - 2)[:, None, None] * tk
    rel_a = off + jnp.arange(tk, dtype=jnp.int32)[None, :, None] - jnp.arange(tbw, dtype=jnp.int32)[None, None, :]
    bias_a = _bias_tiles(rel_bias, _t5_bucket(rel_a).reshape(nv * tk, tbw), head0=0, n_heads=A_HEADS,
                         tr=tk).reshape(A_HEADS, nv, tk, tbw)
    rel_m = jnp.arange(N_META, dtype=jnp.int32)[:, None] - (jnp.arange(s, dtype=jnp.int32)[None, :] + N_META)
    bias_m = _bias_tiles(rel_bias, _t5_bucket(rel_m), head0=0, n_heads=A_HEADS, tr=N_META)
    key_r = jnp.arange(_BKT, dtype=jnp.int32)[:, None]
    rel_b = (key_r - WINDOW) - jnp.arange(_BT, dtype=jnp.int32)[None, :]
    in_win = jnp.abs(rel_b) <= WINDOW
    idx_b = jnp.concatenate([
        jnp.where(in_win & ((key_r >= _PAD_FRONT) | (not first)) & ((key_r < _BT + WINDOW) | (not last)),
                  _t5_bucket(rel_b), N_BUCKETS)
        for last in (False, True) for first in (False, True)], axis=0)
    bias_b = _bias_tiles(rel_bias, idx_b, head0=A_HEADS, n_heads=B_HEADS, tr=_BKT,
                         heads_on_lanes=True).reshape(4, _BKT, B_HEADS * _BT)

    oa = _attn_a(lam, qat, ka, vat, ka_m, vat_m, bias_a, bias_m, za, subln_g[0][None, :], tq=tq, tk=tk)

    def extend(tok, m):
        return jnp.concatenate([
            jnp.zeros((nb, _PAD_FRONT, B_KV_WIDTH), BF16),
            jnp.broadcast_to(m, (nb, N_META, B_KV_WIDTH)), tok,
            jnp.zeros((nb, WINDOW, B_KV_WIDTH), BF16)], axis=1)

    ob = _attn_b(sink[0], qbt, extend(kb, kb_m), extend(vb, vb_m), bias_b, zb)

    return _output(x, oa, ob, ga, gb, w_out_a[0].astype(BF16), w_out_b[0].astype(BF16),
                   w_out[0].astype(BF16), post_norm_g[0][None, :], tm=min(1024, s))
```

```python
import functools
import math

import jax
import jax.numpy as jnp
from jax import lax
from jax.experimental import pallas as pl
from jax.experimental.pallas import tpu as pltpu

N_META = 16
WINDOW = 128
A_HEADS = 4
A_QK_DIM = 64
A_V_DIM = 2 * A_QK_DIM
A_WIDTH = A_HEADS * A_V_DIM
B_HEADS = 8
B_KV_HEADS = 2
B_HEAD_DIM = 64
B_WIDTH = B_HEADS * B_HEAD_DIM
B_KV_WIDTH = B_KV_HEADS * B_HEAD_DIM
N_BUCKETS = 32
MAX_DISTANCE = 128
EPS = 1e-6
NEG = -1e30
LOG2E = math.log2(math.e)
LAM_INIT = 0.8 - 0.6 * math.exp(-0.3 * 0)

LANES = 128
OVERFLOW_GUARD = 2.0 ** 100
F32_FINITE = 3.0e38
SUM_ROWS = 16
VMEM_LIMIT_BYTES = 56 * 1024 * 1024

_COLS = (A_WIDTH, A_WIDTH, A_WIDTH, A_WIDTH, B_WIDTH, B_KV_WIDTH, B_KV_WIDTH, B_WIDTH)
BF16 = jnp.bfloat16
F32 = jnp.float32


def _t5_bucket(rel):
    half = N_BUCKETS // 2
    max_exact = half // 2
    ret = jnp.where(rel > 0, half, 0)
    n = jnp.abs(rel)
    nf = jnp.maximum(n, 1).astype(F32)
    large = max_exact + (jnp.log(nf / max_exact) / math.log(MAX_DISTANCE / max_exact)
                         * (half - max_exact)).astype(jnp.int32)
    large = jnp.minimum(large, half - 1)
    return ret + jnp.where(n < max_exact, n, large)


def _proj_kernel(x_ref, g_ref, w_ref, b_ref,
                 qat_ref, ka_ref, vat_ref, za_ref, qbt_ref, kb_ref, vb_ref, zb_ref, ga_ref, gb_ref,
                 *, d_model, tq, tqb, tkv):
    x = x_ref[...]
    hn = (x * lax.rsqrt(jnp.mean(x * x, axis=-1, keepdims=True) + EPS) * g_ref[...]).astype(BF16)

    def seg(c0, n):
        return jnp.dot(hn, w_ref[:, c0:c0 + n], preferred_element_type=F32) + b_ref[:, c0:c0 + n]

    qscale = (A_QK_DIM ** -0.5) * LOG2E
    c = 0
    qa = seg(c, A_WIDTH) * qscale; c += A_WIDTH
    for t in range(qat_ref.shape[0]):
        qat_ref[t] = qa[t * tq:(t + 1) * tq, :].T.astype(BF16)
    ka_ref[...] = seg(c, A_WIDTH).astype(BF16); c += A_WIDTH
    va = seg(c, A_WIDTH); c += A_WIDTH
    for t in range(vat_ref.shape[0]):
        vat_ref[t] = va[t * tkv:(t + 1) * tkv, :].T.astype(BF16)
    za_ref[...] = seg(c, A_WIDTH).astype(BF16); c += A_WIDTH
    qb = seg(c, B_WIDTH) * qscale; c += B_WIDTH
    for t in range(qbt_ref.shape[0]):
        qbt_ref[t] = qb[t * tqb:(t + 1) * tqb, :].T.astype(BF16)
    kb_ref[...] = seg(c, B_KV_WIDTH).astype(BF16); c += B_KV_WIDTH
    vb_ref[...] = seg(c, B_KV_WIDTH).astype(BF16); c += B_KV_WIDTH
    zb_ref[...] = seg(c, B_WIDTH).astype(BF16); c += B_WIDTH
    ga_ref[...] = seg(c, d_model).astype(BF16); c += d_model
    gb_ref[...] = seg(c, d_model).astype(BF16)


def _project(x3, g, w, b, *, tm, tq, tqb, tkv):
    nb, s, d = x3.shape
    d_in = w.shape[1]
    widths = _COLS + (d, d)
    nt = s // tm
    row = lambda n: pl.BlockSpec((None, tm, n), lambda bi, i: (bi, i, 0))
    const = lambda shape: pl.BlockSpec(shape, lambda bi, i: (0, 0), pipeline_mode=pl.Buffered(1))
    out_specs, out_shapes = [], []
    for idx, n in enumerate(widths):
        if idx in (0, 2, 4):
            tt = {0: tq, 2: tkv, 4: tqb}[idx]
            out_specs.append(pl.BlockSpec((None, tm // tt, n, tt), lambda bi, i: (bi, i, 0, 0)))
            out_shapes.append(jax.ShapeDtypeStruct((nb, s // tt, n, tt), BF16))
        else:
            out_specs.append(row(n))
            out_shapes.append(jax.ShapeDtypeStruct((nb, s, n), BF16))
    return pl.pallas_call(
        functools.partial(_proj_kernel, d_model=d, tq=tq, tqb=tqb, tkv=tkv),
        grid=(nb, nt),
        in_specs=[row(d), const((1, d)), const((d, d_in)), const((1, d_in))],
        out_specs=out_specs,
        out_shape=out_shapes,
        compiler_params=pltpu.CompilerParams(
            dimension_semantics=("arbitrary", "arbitrary"), vmem_limit_bytes=VMEM_LIMIT_BYTES),
        name="proj",
    )(x3, g, w, b)


def _bias_kernel(tab_ref, idx_ref, o_ref, *, head0):
    h = pl.program_id(0) + head0
    idx = idx_ref[...]
    acc = jnp.full(idx.shape, NEG, F32)
    for bkt in range(N_BUCKETS):
        acc = jnp.where(idx == bkt, tab_ref[bkt, h] * LOG2E, acc)
    o_ref[...] = acc


def _bias_tiles(tab, idx, *, head0, n_heads, tr, heads_on_lanes=False):
    r, c = idx.shape
    if heads_on_lanes:
        out_spec = pl.BlockSpec((tr, c), lambda h, i: (i, h))
        out_shape = jax.ShapeDtypeStruct((r, n_heads * c), F32)
    else:
        out_spec = pl.BlockSpec((None, tr, c), lambda h, i: (h, i, 0))
        out_shape = jax.ShapeDtypeStruct((n_heads, r, c), F32)
    return pl.pallas_call(
        functools.partial(_bias_kernel, head0=head0),
        grid=(n_heads, r // tr),
        in_specs=[pl.BlockSpec(memory_space=pltpu.SMEM),
                  pl.BlockSpec((tr, c), lambda h, i: (i, 0))],
        out_specs=out_spec,
        out_shape=out_shape,
        compiler_params=pltpu.CompilerParams(dimension_semantics=("arbitrary", "arbitrary")),
        name="bias_tiles",
    )(tab, idx)


def _attn_a_kernel(lam_ref, q_ref, k_ref, vt_ref, km_ref, vmt_ref, bias_ref, bm_ref, z_ref, g_ref,
                   o_ref, qt_ref, m_ref, acc_ref, p_ref, *, tq, tk, nk, nv, tb):
    i = pl.program_id(2)
    q = jnp.concatenate([q_ref[u] for u in range(q_ref.shape[0])], axis=1)
    row = lax.broadcasted_iota(jnp.int32, q.shape, 0)
    qt_ref[:, :tq] = jnp.where(row < A_QK_DIM, q, jnp.zeros_like(q))
    qt_ref[:, tq:] = jnp.where(row >= A_QK_DIM, q, jnp.zeros_like(q))
    qt = qt_ref[...]

    def both(fn):
        return jnp.concatenate([fn(slice(0, tq)), fn(slice(tq, 2 * tq))], axis=1)

    def pv_and_sum(vt, pb):
        vt1 = jnp.concatenate([vt, jnp.ones((SUM_ROWS, vt.shape[1]), BF16)], axis=0)
        return both(lambda c: jnp.dot(vt1, pb[:, c], preferred_element_type=F32))

    def raw_scores(j):
        ks = pl.multiple_of(j * tk, tk)
        return jnp.dot(k_ref[pl.ds(ks, tk), :], qt, preferred_element_type=F32)

    def bias2(j):
        tbw = bias_ref.shape[-1]
        pieces = [bias_ref[jnp.clip(j - (i * (tq // tbw) + u) * (tbw // tk) + 2, 0, nv - 1)]
                  for u in range(tq // tbw)]
        return jnp.concatenate(pieces + pieces, axis=1)

    def tile_scores(j):
        return raw_scores(j) + bias2(j)

    bm = bm_ref[...]
    s_meta = jnp.dot(km_ref[...], qt, preferred_element_type=F32) + jnp.concatenate([bm, bm], axis=1)
    m0 = jnp.max(s_meta, axis=0, keepdims=True)

    def init_from_meta():
        acc_ref[...] = pv_and_sum(vmt_ref[...], jnp.exp2(s_meta - m0).astype(BF16))

    def stage_p(j, slot):
        p_ref[slot] = jnp.exp2(tile_scores(j) - m0).astype(BF16)

    stage_p(0, 0)
    init_from_meta()

    def group(g, last):
        pv = None
        for u in range(tb):
            j = g * tb + u
            if not (last and u == tb - 1):
                stage_p(j + 1, (u + 1) % 2)
            d = pv_and_sum(vt_ref[j], p_ref[u % 2])
            pv = d if pv is None else pv + d
        acc_ref[...] += pv

    def trip(g, carry):
        group(g, False)
        return carry

    n_trips = nk // tb
    if n_trips > 1:
        group(0, False)
    lax.fori_loop(1, n_trips - 1, trip, 0)
    group(n_trips - 1, True)

    def finish():
        acc = acc_ref[...]
        l = acc[A_V_DIM:A_V_DIM + 1, :]
        inv = 1.0 / l
        ot = acc[:A_V_DIM, :tq] * inv[:, :tq] - lam_ref[0] * (acc[:A_V_DIM, tq:] * inv[:, tq:])
        ssq = jnp.sum(ot * ot, axis=0, keepdims=True)
        yt = ot * lax.rsqrt(ssq * (1.0 / A_V_DIM) + EPS)
        z = z_ref[...].astype(F32)
        o_ref[...] = (yt.T * g_ref[...] * (1.0 - LAM_INIT) * (z * jax.nn.sigmoid(z))).astype(BF16)
        return l, ssq

    l, ssq = finish()
    overflow = (jnp.max(jnp.where(l < OVERFLOW_GUARD, 0.0, 1.0))
                + jnp.max(jnp.where(ssq < F32_FINITE, 0.0, 1.0))) > 0.0

    @pl.when(overflow)
    def _():
        init_from_meta()
        m_ref[...] = m0

        def step(j, carry):
            s = tile_scores(j)
            m_old = m_ref[...]
            m_new = jnp.maximum(m_old, jnp.max(s, axis=0, keepdims=True))
            alpha = jnp.exp2(m_old - m_new)
            acc_ref[...] = alpha * acc_ref[...] + pv_and_sum(vt_ref[j], jnp.exp2(s - m_new).astype(BF16))
            m_ref[...] = m_new
            return carry

        lax.fori_loop(0, nk, step, 0)
        finish()


def _attn_a(lam, qat, ka, vat, ka_m, vat_m, bias, bias_m, za, subln_g, *, tq, tk):
    nb, s, _ = ka.shape
    tqs = qat.shape[-1]
    nk = s // tk
    nv, tbw = bias.shape[1], bias.shape[3]
    tb = next(c for c in (8, 4, 2, 1) if nk % c == 0)
    assert nk % tb == 0
    head = lambda n: pl.BlockSpec((None, n, A_V_DIM), lambda b, h, i: (b, i, h))
    return pl.pallas_call(
        functools.partial(_attn_a_kernel, tq=tq, tk=tk, nk=nk, nv=nv, tb=tb),
        grid=(nb, A_HEADS, s // tq),
        in_specs=[
            pl.BlockSpec(memory_space=pltpu.SMEM),
            pl.BlockSpec((None, tq // tqs, A_V_DIM, tqs), lambda b, h, i: (b, i, h, 0)),
            pl.BlockSpec((None, s, A_V_DIM), lambda b, h, i: (b, 0, h)),
            pl.BlockSpec((None, nk, A_V_DIM, tk), lambda b, h, i: (b, 0, h, 0)),
            pl.BlockSpec((None, N_META, A_V_DIM), lambda b, h, i: (0, 0, h)),
            pl.BlockSpec((None, None, A_V_DIM, N_META), lambda b, h, i: (0, 0, h, 0)),
            pl.BlockSpec((None, nv, tk, tbw), lambda b, h, i: (h, 0, 0, 0)),
            pl.BlockSpec((None, N_META, tq), lambda b, h, i: (h, 0, i)),
            head(tq),
            pl.BlockSpec((1, A_V_DIM), lambda b, h, i: (0, 0)),
        ],
        out_specs=head(tq),
        out_shape=jax.ShapeDtypeStruct((nb, s, A_WIDTH), BF16),
        scratch_shapes=[
            pltpu.VMEM((A_V_DIM, 2 * tq), BF16),
            pltpu.VMEM((1, 2 * tq), F32),
            pltpu.VMEM((A_V_DIM + SUM_ROWS, 2 * tq), F32),
            pltpu.VMEM((2, tk, 2 * tq), BF16),
        ],
        compiler_params=pltpu.CompilerParams(
            dimension_semantics=("arbitrary", "arbitrary", "arbitrary"),
            vmem_limit_bytes=VMEM_LIMIT_BYTES),
        name="attn_a",
    )(lam, qat, ka, vat, ka_m, vat_m, bias, bias_m, za, subln_g)


_BT = WINDOW
_BSUB = 8
_BKT = _BT + 2 * WINDOW
_PAD_FRONT = WINDOW - N_META
_B_SLABS = B_HEADS // 2
_B_GROUP = B_HEADS // B_KV_HEADS


def _attn_b_kernel(sink_ref, q_ref, *rest, n_sub_total):
    nkb = _BSUB + 2
    k_refs, v_refs = rest[:nkb], rest[nkb:2 * nkb]
    bias_ref, z_ref, o_ref, qt_ref = rest[2 * nkb:]
    i = pl.program_id(1)
    sink_row = jnp.concatenate(
        [jnp.full((1, _BT), sink_ref[h] * LOG2E, F32) for h in range(B_HEADS)], axis=1)
    zeros = jnp.zeros((B_HEAD_DIM, _BT), BF16)

    def scores(t):
        it = i * _BSUB + t
        k = jnp.concatenate([r[...] for r in k_refs[t:t + 3]], axis=0)
        q = q_ref[t]
        for h in range(B_HEADS):
            blk = q[h * B_HEAD_DIM:(h + 1) * B_HEAD_DIM, :]
            pair = [blk, zeros] if h < _B_GROUP else [zeros, blk]
            qt_ref[t, :, h * _BT:(h + 1) * _BT] = jnp.concatenate(pair, axis=0)
        variant = (it == 0).astype(jnp.int32) + 2 * (it == n_sub_total - 1).astype(jnp.int32)
        return jnp.dot(k, qt_ref[t], preferred_element_type=F32) + bias_ref[variant]

    def finish(t, s):
        v = jnp.concatenate([r[...] for r in v_refs[t:t + 3]], axis=0)
        vt = jnp.concatenate([v.astype(F32).T.astype(BF16), jnp.ones((SUM_ROWS, _BKT), BF16)], axis=0)
        m = jnp.maximum(jnp.max(s, axis=0, keepdims=True), sink_row)
        pv = jnp.dot(vt, jnp.exp2(s - m).astype(BF16), preferred_element_type=F32)
        l = pv[2 * B_HEAD_DIM:2 * B_HEAD_DIM + 1, :] + jnp.exp2(sink_row - m)
        ot = pv[:2 * B_HEAD_DIM, :] * (1.0 / l)
        for slab in range(_B_SLABS):
            g = (2 * slab) // _B_GROUP
            rows = slice(g * B_HEAD_DIM, (g + 1) * B_HEAD_DIM)
            x = jnp.concatenate([ot[rows, (2 * slab) * _BT:(2 * slab + 1) * _BT],
                                 ot[rows, (2 * slab + 1) * _BT:(2 * slab + 2) * _BT]], axis=0)
            cols = slice(slab * LANES, (slab + 1) * LANES)
            z = z_ref[t * _BT:(t + 1) * _BT, cols].astype(F32)
            o_ref[t * _BT:(t + 1) * _BT, cols] = (x.T * (z * jax.nn.sigmoid(z))).astype(BF16)

    s_cur = scores(0)
    for t in range(_BSUB):
        s_nxt = scores(t + 1) if t + 1 < _BSUB else None
        finish(t, s_cur)
        s_cur = s_nxt


def _attn_b(sink, qbt, kb_ext, vb_ext, bias, zb):
    nb, s, _ = zb.shape
    tq = _BT * _BSUB
    row = pl.BlockSpec((None, tq, B_WIDTH), lambda b, i: (b, i, 0))
    nkb = _BSUB + 2
    kv = [pl.BlockSpec((None, WINDOW, B_KV_WIDTH), (lambda b, i, t=t: (b, _BSUB * i + t, 0)))
          for t in range(nkb)]
    return pl.pallas_call(
        functools.partial(_attn_b_kernel, n_sub_total=s // _BT),
        grid=(nb, s // tq),
        in_specs=[pl.BlockSpec(memory_space=pltpu.SMEM),
                  pl.BlockSpec((None, _BSUB, B_WIDTH, _BT), lambda b, i: (b, i, 0, 0))] + kv + kv
                 + [pl.BlockSpec((4, _BKT, B_HEADS * _BT), lambda b, i: (0, 0, 0),
                                 pipeline_mode=pl.Buffered(1)), row],
        out_specs=row,
        out_shape=jax.ShapeDtypeStruct((nb, s, B_WIDTH), BF16),
        scratch_shapes=[pltpu.VMEM((_BSUB, 2 * B_HEAD_DIM, B_HEADS * _BT), BF16)],
        compiler_params=pltpu.CompilerParams(
            dimension_semantics=("arbitrary", "arbitrary"), vmem_limit_bytes=VMEM_LIMIT_BYTES),
        name="attn_b",
    )(sink, qbt, *([kb_ext] * nkb), *([vb_ext] * nkb), bias, zb)


def _out_kernel(x_ref, a_ref, b_ref, ga_ref, gb_ref, woa_ref, wob_ref, wo_ref, g_ref, o_ref):
    ya = jnp.dot(a_ref[...], woa_ref[...], preferred_element_type=F32)
    yb = jnp.dot(b_ref[...], wob_ref[...], preferred_element_type=F32)
    mixed = (jax.nn.sigmoid(ga_ref[...].astype(F32)) * ya
             + jax.nn.sigmoid(gb_ref[...].astype(F32)) * yb)
    y = jnp.dot(mixed.astype(BF16), wo_ref[...], preferred_element_type=F32)
    y = y * lax.rsqrt(jnp.mean(y * y, axis=-1, keepdims=True) + EPS) * g_ref[...]
    o_ref[...] = x_ref[...] + y


def _output(x3, a, b, ga, gb, woa, wob, wo, g, *, tm):
    nb, s, d = x3.shape
    row = lambda n: pl.BlockSpec((None, tm, n), lambda bi, i: (bi, i, 0))
    const = lambda shape: pl.BlockSpec(shape, lambda bi, i: (0, 0), pipeline_mode=pl.Buffered(1))
    return pl.pallas_call(
        _out_kernel,
        grid=(nb, s // tm),
        in_specs=[row(d), row(A_WIDTH), row(B_WIDTH), row(d), row(d),
                  const((A_WIDTH, d)), const((B_WIDTH, d)), const((d, d)), const((1, d))],
        out_specs=row(d),
        out_shape=jax.ShapeDtypeStruct((nb, s, d), F32),
        compiler_params=pltpu.CompilerParams(
            dimension_semantics=("arbitrary", "arbitrary"), vmem_limit_bytes=VMEM_LIMIT_BYTES),
        name="out_proj",
    )(x3, a, b, ga, gb, woa, wob, wo, g)


def kernel(x, meta_tokens, rel_bias, pre_norm_g, w_in, b_in, lambda_q1, lambda_k1, lambda_q2, lambda_k2,
           subln_g, sink, w_out_a, w_out_b, w_out, post_norm_g):
    nb, s, d = x.shape
    tm = min(512, s)
    tq = min(1024, s)
    tk = min(256, s)
    assert s % tm == 0 and s % tq == 0 and tq % tk == 0 and max(tq, tm) % min(tq, tm) == 0 and tm % _BT == 0 and s % (_BT * _BSUB) == 0

    w = w_in[0].astype(BF16)
    b = b_in[0][None, :]
    g_pre = pre_norm_g[0][None, :]
    lam = (jnp.exp(jnp.sum(lambda_q1[0].astype(F32) * lambda_k1[0].astype(F32)))
           - jnp.exp(jnp.sum(lambda_q2[0].astype(F32) * lambda_k2[0].astype(F32))) + LAM_INIT).reshape(1)

    qat, ka, vat, za, qbt, kb, vb, zb, ga, gb = _project(x, g_pre, w, b, tm=tm, tq=min(tq, tm), tqb=_BT, tkv=tk)
    meta = _project(meta_tokens[None], g_pre, w, b, tm=N_META, tq=N_META, tqb=N_META, tkv=N_META)
    ka_m, vat_m, kb_m, vb_m = meta[1], meta[2], meta[5], meta[6]

    tbw = tk
    nv = tbw // tk + 4
    off = (jnp.arange(nv, dtype=jnp.int32) - 2)[:, None, None] * tk
    rel_a = off + jnp.arange(tk, dtype=jnp.int32)[None, :, None] - jnp.arange(tbw, dtype=jnp.int32)[None, None, :]
    bias_a = _bias_tiles(rel_bias, _t5_bucket(rel_a).reshape(nv * tk, tbw), head0=0, n_heads=A_HEADS,
                         tr=tk).reshape(A_HEADS, nv, tk, tbw)
    rel_m = jnp.arange(N_META, dtype=jnp.int32)[:, None] - (jnp.arange(s, dtype=jnp.int32)[None, :] + N_META)
    bias_m = _bias_tiles(rel_bias, _t5_bucket(rel_m), head0=0, n_heads=A_HEADS, tr=N_META)
    key_r = jnp.arange(_BKT, dtype=jnp.int32)[:, None]
    rel_b = (key_r - WINDOW) - jnp.arange(_BT, dtype=jnp.int32)[None, :]
    in_win = jnp.abs(rel_b) <= WINDOW
    idx_b = jnp.concatenate([
        jnp.where(in_win & ((key_r >= _PAD_FRONT) | (not first)) & ((key_r < _BT + WINDOW) | (not last)),
                  _t5_bucket(rel_b), N_BUCKETS)
        for last in (False, True) for first in (False, True)], axis=0)
    bias_b = _bias_tiles(rel_bias, idx_b, head0=A_HEADS, n_heads=B_HEADS, tr=_BKT,
                         heads_on_lanes=True).reshape(4, _BKT, B_HEADS * _BT)

    oa = _attn_a(lam, qat, ka, vat, ka_m, vat_m, bias_a, bias_m, za, subln_g[0][None, :], tq=tq, tk=tk)

    def extend(tok, m):
        return jnp.concatenate([
            jnp.zeros((nb, _PAD_FRONT, B_KV_WIDTH), BF16),
            jnp.broadcast_to(m, (nb, N_META, B_KV_WIDTH)), tok,
            jnp.zeros((nb, WINDOW, B_KV_WIDTH), BF16)], axis=1)

    ob = _attn_b(sink[0], qbt, extend(kb, kb_m), extend(vb, vb_m), bias_b, zb)

    return _output(x, oa, ob, ga, gb, w_out_a[0].astype(BF16), w_out_b[0].astype(BF16),
                   w_out[0].astype(BF16), post_norm_g[0][None, :], tm=min(1024, s))
```

```python
import functools
import math

import jax
import jax.numpy as jnp
from jax import lax
from jax.experimental import pallas as pl
from jax.experimental.pallas import tpu as pltpu

N_META = 16
WINDOW = 128
A_HEADS = 4
A_QK_DIM = 64
A_V_DIM = 2 * A_QK_DIM
A_WIDTH = A_HEADS * A_V_DIM
B_HEADS = 8
B_KV_HEADS = 2
B_HEAD_DIM = 64
B_WIDTH = B_HEADS * B_HEAD_DIM
B_KV_WIDTH = B_KV_HEADS * B_HEAD_DIM
N_BUCKETS = 32
MAX_DISTANCE = 128
EPS = 1e-6
NEG = -1e30
LOG2E = math.log2(math.e)
LAM_INIT = 0.8 - 0.6 * math.exp(-0.3 * 0)

LANES = 128
OVERFLOW_GUARD = 2.0 ** 100
F32_FINITE = 3.0e38
SUM_ROWS = 16
VMEM_LIMIT_BYTES = 56 * 1024 * 1024

_COLS = (A_WIDTH, A_WIDTH, A_WIDTH, A_WIDTH, B_WIDTH, B_KV_WIDTH, B_KV_WIDTH, B_WIDTH)
BF16 = jnp.bfloat16
F32 = jnp.float32


def _t5_bucket(rel):
    half = N_BUCKETS // 2
    max_exact = half // 2
    ret = jnp.where(rel > 0, half, 0)
    n = jnp.abs(rel)
    nf = jnp.maximum(n, 1).astype(F32)
    large = max_exact + (jnp.log(nf / max_exact) / math.log(MAX_DISTANCE / max_exact)
                         * (half - max_exact)).astype(jnp.int32)
    large = jnp.minimum(large, half - 1)
    return ret + jnp.where(n < max_exact, n, large)


def _proj_kernel(x_ref, g_ref, w_ref, b_ref,
                 qat_ref, ka_ref, vat_ref, za_ref, qbt_ref, kb_ref, vb_ref, zb_ref, ga_ref, gb_ref,
                 *, d_model, tq, tqb, tkv):
    x = x_ref[...]
    hn = (x * lax.rsqrt(jnp.mean(x * x, axis=-1, keepdims=True) + EPS) * g_ref[...]).astype(BF16)

    def seg(c0, n):
        return jnp.dot(hn, w_ref[:, c0:c0 + n], preferred_element_type=F32) + b_ref[:, c0:c0 + n]

    qscale = (A_QK_DIM ** -0.5) * LOG2E
    c = 0
    qa = seg(c, A_WIDTH) * qscale; c += A_WIDTH
    for t in range(qat_ref.shape[0]):
        qat_ref[t] = qa[t * tq:(t + 1) * tq, :].T.astype(BF16)
    ka_ref[...] = seg(c, A_WIDTH).astype(BF16); c += A_WIDTH
    va = seg(c, A_WIDTH); c += A_WIDTH
    for t in range(vat_ref.shape[0]):
        vat_ref[t] = va[t * tkv:(t + 1) * tkv, :].T.astype(BF16)
    za_ref[...] = seg(c, A_WIDTH).astype(BF16); c += A_WIDTH
    qb = seg(c, B_WIDTH) * qscale; c += B_WIDTH
    for t in range(qbt_ref.shape[0]):
        qbt_ref[t] = qb[t * tqb:(t + 1) * tqb, :].T.astype(BF16)
    kb_ref[...] = seg(c, B_KV_WIDTH).astype(BF16); c += B_KV_WIDTH
    vb_ref[...] = seg(c, B_KV_WIDTH).astype(BF16); c += B_KV_WIDTH
    zb_ref[...] = seg(c, B_WIDTH).astype(BF16); c += B_WIDTH
    ga_ref[...] = seg(c, d_model).astype(BF16); c += d_model
    gb_ref[...] = seg(c, d_model).astype(BF16)


def _project(x3, g, w, b, *, tm, tq, tqb, tkv):
    nb, s, d = x3.shape
    d_in = w.shape[1]
    widths = _COLS + (d, d)
    nt = s // tm
    row = lambda n: pl.BlockSpec((None, tm, n), lambda bi, i: (bi, i, 0))
    const = lambda shape: pl.BlockSpec(shape, lambda bi, i: (0, 0), pipeline_mode=pl.Buffered(1))
    out_specs, out_shapes = [], []
    for idx, n in enumerate(widths):
        if idx in (0, 2, 4):
            tt = {0: tq, 2: tkv, 4: tqb}[idx]
            out_specs.append(pl.BlockSpec((None, tm // tt, n, tt), lambda bi, i: (bi, i, 0, 0)))
            out_shapes.append(jax.ShapeDtypeStruct((nb, s // tt, n, tt), BF16))
        else:
            out_specs.append(row(n))
            out_shapes.append(jax.ShapeDtypeStruct((nb, s, n), BF16))
    return pl.pallas_call(
        functools.partial(_proj_kernel, d_model=d, tq=tq, tqb=tqb, tkv=tkv),
        grid=(nb, nt),
        in_specs=[row(d), const((1, d)), const((d, d_in)), const((1, d_in))],
        out_specs=out_specs,
        out_shape=out_shapes,
        compiler_params=pltpu.CompilerParams(
            dimension_semantics=("arbitrary", "arbitrary"), vmem_limit_bytes=VMEM_LIMIT_BYTES),
        name="proj",
    )(x3, g, w, b)


def _bias_kernel(tab_ref, idx_ref, o_ref, *, head0):
    h = pl.program_id(0) + head0
    idx = idx_ref[...]
    acc = jnp.full(idx.shape, NEG, F32)
    for bkt in range(N_BUCKETS):
        acc = jnp.where(idx == bkt, tab_ref[bkt, h] * LOG2E, acc)
    o_ref[...] = acc


def _bias_tiles(tab, idx, *, head0, n_heads, tr, heads_on_lanes=False):
    r, c = idx.shape
    if heads_on_lanes:
        out_spec = pl.BlockSpec((tr, c), lambda h, i: (i, h))
        out_shape = jax.ShapeDtypeStruct((r, n_heads * c), F32)
    else:
        out_spec = pl.BlockSpec((None, tr, c), lambda h, i: (h, i, 0))
        out_shape = jax.ShapeDtypeStruct((n_heads, r, c), F32)
    return pl.pallas_call(
        functools.partial(_bias_kernel, head0=head0),
        grid=(n_heads, r // tr),
        in_specs=[pl.BlockSpec(memory_space=pltpu.SMEM),
                  pl.BlockSpec((tr, c), lambda h, i: (i, 0))],
        out_specs=out_spec,
        out_shape=out_shape,
        compiler_params=pltpu.CompilerParams(dimension_semantics=("arbitrary", "arbitrary")),
        name="bias_tiles",
    )(tab, idx)


def _attn_a_kernel(lam_ref, q_ref, k_ref, vt_ref, km_ref, vmt_ref, bias_ref, bm_ref, z_ref, g_ref,
                   o_ref, qt_ref, m_ref, acc_ref, p_ref, flag_ref, *, tq, tk, nk, nv, tb, nq):
    tqs = q_ref.shape[-1]
    tbw = bias_ref.shape[-1]

    def both(fn):
        return jnp.concatenate([fn(slice(0, tq)), fn(slice(tq, 2 * tq))], axis=1)

    def pv_and_sum(vt, pb):
        vt1 = jnp.concatenate([vt, jnp.ones((SUM_ROWS, vt.shape[1]), BF16)], axis=0)
        return both(lambda c: jnp.dot(vt1, pb[:, c], preferred_element_type=F32))

    def tile_scores(i, j, s):
        ks = pl.multiple_of(j * tk, tk)
        raw = jnp.dot(k_ref[pl.ds(ks, tk), :], qt_ref[s], preferred_element_type=F32)
        pieces = [bias_ref[jnp.clip(j - (i * (tq // tbw) + u) * (tbw // tk) + 2, 0, nv - 1)]
                  for u in range(tq // tbw)]
        return raw + jnp.concatenate(pieces + pieces, axis=1)

    def meta_scores(i, s):
        bm = bm_ref[i]
        return (jnp.dot(km_ref[...], qt_ref[s], preferred_element_type=F32)
                + jnp.concatenate([bm, bm], axis=1))

    def stage_p(i, j, s, pslot):
        p_ref[pslot] = jnp.exp2(tile_scores(i, j, s) - m_ref[s]).astype(BF16)

    def prologue(i, s):
        q = jnp.concatenate([q_ref[i * (tq // tqs) + u] for u in range(tq // tqs)], axis=1)
        row = lax.broadcasted_iota(jnp.int32, q.shape, 0)
        qt_ref[s, :, :tq] = jnp.where(row < A_QK_DIM, q, jnp.zeros_like(q))
        qt_ref[s, :, tq:] = jnp.where(row >= A_QK_DIM, q, jnp.zeros_like(q))
        s_meta = meta_scores(i, s)
        m0 = jnp.max(s_meta, axis=0, keepdims=True)
        m_ref[s] = m0
        stage_p(i, 0, s, 0)
        acc_ref[s] = pv_and_sum(vmt_ref[...], jnp.exp2(s_meta - m0).astype(BF16))

    def group(i, s, g, last):
        pv = None
        for u in range(tb):
            j = g * tb + u
            if not (last and u == tb - 1):
                stage_p(i, j + 1, s, (u + 1) % 2)
            d = pv_and_sum(vt_ref[j], p_ref[u % 2])
            pv = d if pv is None else pv + d
        acc_ref[s] += pv

    n_trips = nk // tb

    def first_trip(i, s):
        if n_trips > 1:
            group(i, s, 0, False)

    def mid_trips(i, s):
        def trip(g, carry):
            group(i, s, g, False)
            return carry
        lax.fori_loop(1, n_trips - 1, trip, 0)

    def last_trip(i, s):
        group(i, s, n_trips - 1, True)

    def finish(i, s):
        acc = acc_ref[s]
        l = acc[A_V_DIM:A_V_DIM + 1, :]
        inv = 1.0 / l
        ot = acc[:A_V_DIM, :tq] * inv[:, :tq] - lam_ref[0] * (acc[:A_V_DIM, tq:] * inv[:, tq:])
        ssq = jnp.sum(ot * ot, axis=0, keepdims=True)
        yt = ot * lax.rsqrt(ssq * (1.0 / A_V_DIM) + EPS)
        rows = pl.ds(pl.multiple_of(i * tq, tq), tq)
        z = z_ref[rows, :].astype(F32)
        o_ref[rows, :] = (yt.T * g_ref[...] * (1.0 - LAM_INIT) * (z * jax.nn.sigmoid(z))).astype(BF16)
        return (jnp.max(jnp.where(l < OVERFLOW_GUARD, 0.0, 1.0))
                + jnp.max(jnp.where(ssq < F32_FINITE, 0.0, 1.0))) > 0.0

    def finish_and_flag(i, s):
        flag_ref[s] = finish(i, s).astype(jnp.int32)

    def redo_if_flagged(i, s):
        @pl.when(flag_ref[s] != 0)
        def _():
            s_meta = meta_scores(i, s)
            m_ref[s] = jnp.max(s_meta, axis=0, keepdims=True)
            acc_ref[s] = pv_and_sum(vmt_ref[...], jnp.exp2(s_meta - m_ref[s]).astype(BF16))

            def step(j, carry):
                sc = tile_scores(i, j, s)
                m_old = m_ref[s]
                m_new = jnp.maximum(m_old, jnp.max(sc, axis=0, keepdims=True))
                alpha = jnp.exp2(m_old - m_new)
                acc_ref[s] = alpha * acc_ref[s] + pv_and_sum(vt_ref[j], jnp.exp2(sc - m_new).astype(BF16))
                m_ref[s] = m_new
                return carry

            lax.fori_loop(0, nk, step, 0)
            finish(i, s)

    t0 = nq % 2
    if t0:
        prologue(0, 0)
        first_trip(0, 0)
        mid_trips(0, 0)
        last_trip(0, 0)
        finish_and_flag(0, 0)
        redo_if_flagged(0, 0)
    if nq > 1:
        prologue(t0, 0)

        def pair(u, carry):
            a = t0 + 2 * u
            first_trip(a, 0)
            mid_trips(a, 0)
            last_trip(a, 0)
            finish_and_flag(a, 0)
            prologue(a + 1, 1)
            first_trip(a + 1, 1)
            redo_if_flagged(a, 0)
            mid_trips(a + 1, 1)
            last_trip(a + 1, 1)
            finish_and_flag(a + 1, 1)
            prologue(jnp.minimum(a + 2, nq - 1), 0)
            redo_if_flagged(a + 1, 1)
            return carry

        lax.fori_loop(0, nq // 2, pair, 0)


def _attn_a(lam, qat, ka, vat, ka_m, vat_m, bias, bias_m, za, subln_g, *, tq, tk):
    nb, s, _ = ka.shape
    tqs = qat.shape[-1]
    nk = s // tk
    nq = s // tq
    nv, tbw = bias.shape[1], bias.shape[3]
    tb = next(c for c in (8, 4, 2, 1) if nk % c == 0)
    assert nk % tb == 0 and tb % 2 == 0 or nk == tb
    head = pl.BlockSpec((None, s, A_V_DIM), lambda b, h: (b, 0, h))
    return pl.pallas_call(
        functools.partial(_attn_a_kernel, tq=tq, tk=tk, nk=nk, nv=nv, tb=tb, nq=nq),
        grid=(nb, A_HEADS),
        in_specs=[
            pl.BlockSpec(memory_space=pltpu.SMEM),
            pl.BlockSpec((None, s // tqs, A_V_DIM, tqs), lambda b, h: (b, 0, h, 0)),
            head,
            pl.BlockSpec((None, nk, A_V_DIM, tk), lambda b, h: (b, 0, h, 0)),
            pl.BlockSpec((None, N_META, A_V_DIM), lambda b, h: (0, 0, h)),
            pl.BlockSpec((None, None, A_V_DIM, N_META), lambda b, h: (0, 0, h, 0)),
            pl.BlockSpec((None, nv, tk, tbw), lambda b, h: (h, 0, 0, 0)),
            pl.BlockSpec((None, nq, N_META, tq), lambda b, h: (h, 0, 0, 0)),
            head,
            pl.BlockSpec((1, A_V_DIM), lambda b, h: (0, 0)),
        ],
        out_specs=head,
        out_shape=jax.ShapeDtypeStruct((nb, s, A_WIDTH), BF16),
        scratch_shapes=[
            pltpu.VMEM((2, A_V_DIM, 2 * tq), BF16),
            pltpu.VMEM((2, 1, 2 * tq), F32),
            pltpu.VMEM((2, A_V_DIM + SUM_ROWS, 2 * tq), F32),
            pltpu.VMEM((2, tk, 2 * tq), BF16),
            pltpu.SMEM((2,), jnp.int32),
        ],
        compiler_params=pltpu.CompilerParams(
            dimension_semantics=("arbitrary", "arbitrary"),
            vmem_limit_bytes=VMEM_LIMIT_BYTES),
        name="attn_a",
    )(lam, qat, ka, vat, ka_m, vat_m, bias, bias_m, za, subln_g)


_BT = WINDOW
_BSUB = 8
_BKT = _BT + 2 * WINDOW
_PAD_FRONT = WINDOW - N_META
_B_SLABS = B_HEADS // 2
_B_GROUP = B_HEADS // B_KV_HEADS


def _attn_b_kernel(sink_ref, q_ref, *rest, n_sub_total):
    nkb = _BSUB + 2
    k_refs, v_refs = rest[:nkb], rest[nkb:2 * nkb]
    bias_ref, z_ref, o_ref, qt_ref = rest[2 * nkb:]
    i = pl.program_id(1)
    sink_row = jnp.concatenate(
        [jnp.full((1, _BT), sink_ref[h] * LOG2E, F32) for h in range(B_HEADS)], axis=1)
    zeros = jnp.zeros((B_HEAD_DIM, _BT), BF16)

    def scores(t):
        it = i * _BSUB + t
        k = jnp.concatenate([r[...] for r in k_refs[t:t + 3]], axis=0)
        q = q_ref[t]
        for h in range(B_HEADS):
            blk = q[h * B_HEAD_DIM:(h + 1) * B_HEAD_DIM, :]
            pair = [blk, zeros] if h < _B_GROUP else [zeros, blk]
            qt_ref[t, :, h * _BT:(h + 1) * _BT] = jnp.concatenate(pair, axis=0)
        variant = (it == 0).astype(jnp.int32) + 2 * (it == n_sub_total - 1).astype(jnp.int32)
        return jnp.dot(k, qt_ref[t], preferred_element_type=F32) + bias_ref[variant]

    def finish(t, s):
        v = jnp.concatenate([r[...] for r in v_refs[t:t + 3]], axis=0)
        vt = jnp.concatenate([v.astype(F32).T.astype(BF16), jnp.ones((SUM_ROWS, _BKT), BF16)], axis=0)
        m = jnp.maximum(jnp.max(s, axis=0, keepdims=True), sink_row)
        pv = jnp.dot(vt, jnp.exp2(s - m).astype(BF16), preferred_element_type=F32)
        l = pv[2 * B_HEAD_DIM:2 * B_HEAD_DIM + 1, :] + jnp.exp2(sink_row - m)
        ot = pv[:2 * B_HEAD_DIM, :] * (1.0 / l)
        for slab in range(_B_SLABS):
            g = (2 * slab) // _B_GROUP
            rows = slice(g * B_HEAD_DIM, (g + 1) * B_HEAD_DIM)
            x = jnp.concatenate([ot[rows, (2 * slab) * _BT:(2 * slab + 1) * _BT],
                                 ot[rows, (2 * slab + 1) * _BT:(2 * slab + 2) * _BT]], axis=0)
            cols = slice(slab * LANES, (slab + 1) * LANES)
            z = z_ref[t * _BT:(t + 1) * _BT, cols].astype(F32)
            o_ref[t * _BT:(t + 1) * _BT, cols] = (x.T * (z * jax.nn.sigmoid(z))).astype(BF16)

    s_cur = scores(0)
    for t in range(_BSUB):
        s_nxt = scores(t + 1) if t + 1 < _BSUB else None
        finish(t, s_cur)
        s_cur = s_nxt


def _attn_b(sink, qbt, kb_ext, vb_ext, bias, zb):
    nb, s, _ = zb.shape
    tq = _BT * _BSUB
    row = pl.BlockSpec((None, tq, B_WIDTH), lambda b, i: (b, i, 0))
    nkb = _BSUB + 2
    kv = [pl.BlockSpec((None, WINDOW, B_KV_WIDTH), (lambda b, i, t=t: (b, _BSUB * i + t, 0)))
          for t in range(nkb)]
    return pl.pallas_call(
        functools.partial(_attn_b_kernel, n_sub_total=s // _BT),
        grid=(nb, s // tq),
        in_specs=[pl.BlockSpec(memory_space=pltpu.SMEM),
                  pl.BlockSpec((None, _BSUB, B_WIDTH, _BT), lambda b, i: (b, i, 0, 0))] + kv + kv
                 + [pl.BlockSpec((4, _BKT, B_HEADS * _BT), lambda b, i: (0, 0, 0),
                                 pipeline_mode=pl.Buffered(1)), row],
        out_specs=row,
        out_shape=jax.ShapeDtypeStruct((nb, s, B_WIDTH), BF16),
        scratch_shapes=[pltpu.VMEM((_BSUB, 2 * B_HEAD_DIM, B_HEADS * _BT), BF16)],
        compiler_params=pltpu.CompilerParams(
            dimension_semantics=("arbitrary", "arbitrary"), vmem_limit_bytes=VMEM_LIMIT_BYTES),
        name="attn_b",
    )(sink, qbt, *([kb_ext] * nkb), *([vb_ext] * nkb), bias, zb)


def _out_kernel(x_ref, a_ref, b_ref, ga_ref, gb_ref, woa_ref, wob_ref, wo_ref, g_ref, o_ref):
    ya = jnp.dot(a_ref[...], woa_ref[...], preferred_element_type=F32)
    yb = jnp.dot(b_ref[...], wob_ref[...], preferred_element_type=F32)
    mixed = (jax.nn.sigmoid(ga_ref[...].astype(F32)) * ya
             + jax.nn.sigmoid(gb_ref[...].astype(F32)) * yb)
    y = jnp.dot(mixed.astype(BF16), wo_ref[...], preferred_element_type=F32)
    y = y * lax.rsqrt(jnp.mean(y * y, axis=-1, keepdims=True) + EPS) * g_ref[...]
    o_ref[...] = x_ref[...] + y


def _output(x3, a, b, ga, gb, woa, wob, wo, g, *, tm):
    nb, s, d = x3.shape
    row = lambda n: pl.BlockSpec((None, tm, n), lambda bi, i: (bi, i, 0))
    const = lambda shape: pl.BlockSpec(shape, lambda bi, i: (0, 0), pipeline_mode=pl.Buffered(1))
    return pl.pallas_call(
        _out_kernel,
        grid=(nb, s // tm),
        in_specs=[row(d), row(A_WIDTH), row(B_WIDTH), row(d), row(d),
                  const((A_WIDTH, d)), const((B_WIDTH, d)), const((d, d)), const((1, d))],
        out_specs=row(d),
        out_shape=jax.ShapeDtypeStruct((nb, s, d), F32),
        compiler_params=pltpu.CompilerParams(
            dimension_semantics=("arbitrary", "arbitrary"), vmem_limit_bytes=VMEM_LIMIT_BYTES),
        name="out_proj",
    )(x3, a, b, ga, gb, woa, wob, wo, g)


def kernel(x, meta_tokens, rel_bias, pre_norm_g, w_in, b_in, lambda_q1, lambda_k1, lambda_q2, lambda_k2,
           subln_g, sink, w_out_a, w_out_b, w_out, post_norm_g):
    nb, s, d = x.shape
    tm = min(1024, s)
    tq = min(1024, s)
    tk = min(256, s)
    assert s % tm == 0 and s % tq == 0 and tq % tk == 0 and max(tq, tm) % min(tq, tm) == 0 and tm % _BT == 0 and s % (_BT * _BSUB) == 0

    w = w_in[0].astype(BF16)
    b = b_in[0][None, :]
    g_pre = pre_norm_g[0][None, :]
    lam = (jnp.exp(jnp.sum(lambda_q1[0].astype(F32) * lambda_k1[0].astype(F32)))
           - jnp.exp(jnp.sum(lambda_q2[0].astype(F32) * lambda_k2[0].astype(F32))) + LAM_INIT).reshape(1)

    qat, ka, vat, za, qbt, kb, vb, zb, ga, gb = _project(x, g_pre, w, b, tm=tm, tq=min(tq, tm), tqb=_BT, tkv=tk)
    meta = _project(meta_tokens[None], g_pre, w, b, tm=N_META, tq=N_META, tqb=N_META, tkv=N_META)
    ka_m, vat_m, kb_m, vb_m = meta[1], meta[2], meta[5], meta[6]

    tbw = tk
    nv = tbw // tk + 4
    off = (jnp.arange(nv, dtype=jnp.int32) - 2)[:, None, None] * tk
    rel_a = off + jnp.arange(tk, dtype=jnp.int32)[None, :, None] - jnp.arange(tbw, dtype=jnp.int32)[None, None, :]
    bias_a = _bias_tiles(rel_bias, _t5_bucket(rel_a).reshape(nv * tk, tbw), head0=0, n_heads=A_HEADS,
                         tr=tk).reshape(A_HEADS, nv, tk, tbw)
    rel_m = jnp.arange(N_META, dtype=jnp.int32)[:, None] - (jnp.arange(s, dtype=jnp.int32)[None, :] + N_META)
    bias_m = _bias_tiles(rel_bias, _t5_bucket(rel_m), head0=0, n_heads=A_HEADS, tr=N_META)
    bias_m = bias_m.reshape(A_HEADS, N_META, s // tq, tq).transpose(0, 2, 1, 3)
    key_r = jnp.arange(_BKT, dtype=jnp.int32)[:, None]
    rel_b = (key_r - WINDOW) - jnp.arange(_BT, dtype=jnp.int32)[None, :]
    in_win = jnp.abs(rel_b) <= WINDOW
    idx_b = jnp.concatenate([
        jnp.where(in_win & ((key_r >= _PAD_FRONT) | (not first)) & ((key_r < _BT + WINDOW) | (not last)),
                  _t5_bucket(rel_b), N_BUCKETS)
        for last in (False, True) for first in (False, True)], axis=0)
    bias_b = _bias_tiles(rel_bias, idx_b, head0=A_HEADS, n_heads=B_HEADS, tr=_BKT,
                         heads_on_lanes=True).reshape(4, _BKT, B_HEADS * _BT)

    oa = _attn_a(lam, qat, ka, vat, ka_m, vat_m, bias_a, bias_m, za, subln_g[0][None, :], tq=tq, tk=tk)

    def extend(tok, m):
        return jnp.concatenate([
            jnp.zeros((nb, _PAD_FRONT, B_KV_WIDTH), BF16),
            jnp.broadcast_to(m, (nb, N_META, B_KV_WIDTH)), tok,
            jnp.zeros((nb, WINDOW, B_KV_WIDTH), BF16)], axis=1)

    ob = _attn_b(sink[0], qbt, extend(kb, kb_m), extend(vb, vb_m), bias_b, zb)

    return _output(x, oa, ob, ga, gb, w_out_a[0].astype(BF16), w_out_b[0].astype(BF16),
                   w_out[0].astype(BF16), post_norm_g[0][None, :], tm=min(1024, s))
```

```python
import functools
import math

import jax
import jax.numpy as jnp
from jax import lax
from jax.experimental import pallas as pl
from jax.experimental.pallas import tpu as pltpu

N_META = 16
WINDOW = 128
A_HEADS = 4
A_QK_DIM = 64
A_V_DIM = 2 * A_QK_DIM
A_WIDTH = A_HEADS * A_V_DIM
B_HEADS = 8
B_KV_HEADS = 2
B_HEAD_DIM = 64
B_WIDTH = B_HEADS * B_HEAD_DIM
B_KV_WIDTH = B_KV_HEADS * B_HEAD_DIM
N_BUCKETS = 32
MAX_DISTANCE = 128
EPS = 1e-6
NEG = -1e30
LOG2E = math.log2(math.e)
LAM_INIT = 0.8 - 0.6 * math.exp(-0.3 * 0)

LANES = 128
OVERFLOW_GUARD = 2.0 ** 100
F32_FINITE = 3.0e38
SUM_ROWS = 16
VMEM_LIMIT_BYTES = 56 * 1024 * 1024

_COLS = (A_WIDTH, A_WIDTH, A_WIDTH, A_WIDTH, B_WIDTH, B_KV_WIDTH, B_KV_WIDTH, B_WIDTH)
BF16 = jnp.bfloat16
F32 = jnp.float32


def _t5_bucket(rel):
    half = N_BUCKETS // 2
    max_exact = half // 2
    ret = jnp.where(rel > 0, half, 0)
    n = jnp.abs(rel)
    nf = jnp.maximum(n, 1).astype(F32)
    large = max_exact + (jnp.log(nf / max_exact) / math.log(MAX_DISTANCE / max_exact)
                         * (half - max_exact)).astype(jnp.int32)
    large = jnp.minimum(large, half - 1)
    return ret + jnp.where(n < max_exact, n, large)


def _proj_kernel(x_ref, g_ref, w_ref, b_ref,
                 qat_ref, ka_ref, vat_ref, za_ref, qbt_ref, kb_ref, vb_ref, zb_ref, ga_ref, gb_ref,
                 *, d_model, tq, tqb, tkv):
    x = x_ref[...]
    hn = (x * lax.rsqrt(jnp.mean(x * x, axis=-1, keepdims=True) + EPS) * g_ref[...]).astype(BF16)

    def seg(c0, n):
        return jnp.dot(hn, w_ref[:, c0:c0 + n], preferred_element_type=F32) + b_ref[:, c0:c0 + n]

    qscale = (A_QK_DIM ** -0.5) * LOG2E
    c = 0
    qa = seg(c, A_WIDTH) * qscale; c += A_WIDTH
    for t in range(qat_ref.shape[0]):
        qat_ref[t] = qa[t * tq:(t + 1) * tq, :].T.astype(BF16)
    ka_ref[...] = seg(c, A_WIDTH).astype(BF16); c += A_WIDTH
    va = seg(c, A_WIDTH); c += A_WIDTH
    for t in range(vat_ref.shape[0]):
        vat_ref[t] = va[t * tkv:(t + 1) * tkv, :].T.astype(BF16)
    za_ref[...] = seg(c, A_WIDTH).astype(BF16); c += A_WIDTH
    qb = seg(c, B_WIDTH) * qscale; c += B_WIDTH
    for t in range(qbt_ref.shape[0]):
        qbt_ref[t] = qb[t * tqb:(t + 1) * tqb, :].T.astype(BF16)
    kb_ref[...] = seg(c, B_KV_WIDTH).astype(BF16); c += B_KV_WIDTH
    vb_ref[...] = seg(c, B_KV_WIDTH).astype(BF16); c += B_KV_WIDTH
    zb_ref[...] = seg(c, B_WIDTH).astype(BF16); c += B_WIDTH
    ga_ref[...] = seg(c, d_model).astype(BF16); c += d_model
    gb_ref[...] = seg(c, d_model).astype(BF16)


def _project(x3, g, w, b, *, tm, tq, tqb, tkv):
    nb, s, d = x3.shape
    d_in = w.shape[1]
    widths = _COLS + (d, d)
    nt = s // tm
    row = lambda n: pl.BlockSpec((None, tm, n), lambda bi, i: (bi, i, 0))
    const = lambda shape: pl.BlockSpec(shape, lambda bi, i: (0, 0), pipeline_mode=pl.Buffered(1))
    out_specs, out_shapes = [], []
    for idx, n in enumerate(widths):
        if idx in (0, 2, 4):
            tt = {0: tq, 2: tkv, 4: tqb}[idx]
            out_specs.append(pl.BlockSpec((None, tm // tt, n, tt), lambda bi, i: (bi, i, 0, 0)))
            out_shapes.append(jax.ShapeDtypeStruct((nb, s // tt, n, tt), BF16))
        else:
            out_specs.append(row(n))
            out_shapes.append(jax.ShapeDtypeStruct((nb, s, n), BF16))
    return pl.pallas_call(
        functools.partial(_proj_kernel, d_model=d, tq=tq, tqb=tqb, tkv=tkv),
        grid=(nb, nt),
        in_specs=[row(d), const((1, d)), const((d, d_in)), const((1, d_in))],
        out_specs=out_specs,
        out_shape=out_shapes,
        compiler_params=pltpu.CompilerParams(
            dimension_semantics=("arbitrary", "arbitrary"), vmem_limit_bytes=VMEM_LIMIT_BYTES),
        name="proj",
    )(x3, g, w, b)


def _bias_kernel(tab_ref, idx_ref, o_ref, *, head0):
    h = pl.program_id(0) + head0
    idx = idx_ref[...]
    acc = jnp.full(idx.shape, NEG, F32)
    for bkt in range(N_BUCKETS):
        acc = jnp.where(idx == bkt, tab_ref[bkt, h] * LOG2E, acc)
    o_ref[...] = acc


def _bias_tiles(tab, idx, *, head0, n_heads, tr, heads_on_lanes=False):
    r, c = idx.shape
    if heads_on_lanes:
        out_spec = pl.BlockSpec((tr, c), lambda h, i: (i, h))
        out_shape = jax.ShapeDtypeStruct((r, n_heads * c), F32)
    else:
        out_spec = pl.BlockSpec((None, tr, c), lambda h, i: (h, i, 0))
        out_shape = jax.ShapeDtypeStruct((n_heads, r, c), F32)
    return pl.pallas_call(
        functools.partial(_bias_kernel, head0=head0),
        grid=(n_heads, r // tr),
        in_specs=[pl.BlockSpec(memory_space=pltpu.SMEM),
                  pl.BlockSpec((tr, c), lambda h, i: (i, 0))],
        out_specs=out_spec,
        out_shape=out_shape,
        compiler_params=pltpu.CompilerParams(dimension_semantics=("arbitrary", "arbitrary")),
        name="bias_tiles",
    )(tab, idx)


def _attn_a_kernel(lam_ref, q_ref, k_ref, vt_ref, km_ref, vmt_ref, bias_ref, bm_ref, z_ref, g_ref,
                   o_ref, qt_ref, m_ref, acc_ref, p_ref, *, tq, tk, nk, nv, tb):
    i = pl.program_id(2)
    q = jnp.concatenate([q_ref[u] for u in range(q_ref.shape[0])], axis=1)
    row = lax.broadcasted_iota(jnp.int32, q.shape, 0)
    qt_ref[:, :tq] = jnp.where(row < A_QK_DIM, q, jnp.zeros_like(q))
    qt_ref[:, tq:] = jnp.where(row >= A_QK_DIM, q, jnp.zeros_like(q))
    qt = qt_ref[...]

    def both(fn):
        return jnp.concatenate([fn(slice(0, tq)), fn(slice(tq, 2 * tq))], axis=1)

    def pv_and_sum(vt, pb):
        vt1 = jnp.concatenate([vt, jnp.ones((SUM_ROWS, vt.shape[1]), BF16)], axis=0)
        return both(lambda c: jnp.dot(vt1, pb[:, c], preferred_element_type=F32))

    def raw_scores(j):
        ks = pl.multiple_of(j * tk, tk)
        return jnp.dot(k_ref[pl.ds(ks, tk), :], qt, preferred_element_type=F32)

    def bias2(j):
        tbw = bias_ref.shape[-1]
        pieces = [bias_ref[jnp.clip(j - (i * (tq // tbw) + u) * (tbw // tk) + 2, 0, nv - 1)]
                  for u in range(tq // tbw)]
        return jnp.concatenate(pieces + pieces, axis=1)

    def tile_scores(j):
        return raw_scores(j) + bias2(j)

    bm = bm_ref[...]
    s_meta = jnp.dot(km_ref[...], qt, preferred_element_type=F32) + jnp.concatenate([bm, bm], axis=1)
    m0 = jnp.max(s_meta, axis=0, keepdims=True)

    def init_from_meta():
        acc_ref[...] = pv_and_sum(vmt_ref[...], jnp.exp2(s_meta - m0).astype(BF16))

    start = i * (tq // tk) - 1
    far_lo = jnp.concatenate([bias_ref[0, 0:1, :]] * (2 * tq // bias_ref.shape[-1]), axis=1)
    far_hi = jnp.concatenate([bias_ref[nv - 1, 0:1, :]] * (2 * tq // bias_ref.shape[-1]), axis=1)
    shift_lo, shift_hi = m0 - far_lo, m0 - far_hi

    def tile_at(pos):
        j = start + pos
        wrapped = j >= nk
        j = jnp.where(wrapped, j - nk, j)
        return jnp.where(j < 0, j + nk, j), wrapped

    def stage_p(pos, slot, near):
        j, wrapped = tile_at(pos)
        if near:
            p = jnp.exp2(tile_scores(j) - m0)
        else:
            p = jnp.exp2(raw_scores(j) - jnp.where(wrapped, shift_lo, shift_hi))
        p_ref[slot] = p.astype(BF16)

    def group(g, first, last):
        pv = None
        for u in range(tb):
            pos = g * tb + u
            if not (last and u == tb - 1):
                stage_p(pos + 1, (u + 1) % 2, first and u + 1 < tb)
            d = pv_and_sum(vt_ref[tile_at(pos)[0]], p_ref[u % 2])
            pv = d if pv is None else pv + d
        acc_ref[...] += pv

    def trip(g, carry):
        group(g, False, False)
        return carry

    stage_p(0, 0, True)
    init_from_meta()

    n_trips = nk // tb
    group(0, True, n_trips == 1)
    lax.fori_loop(1, n_trips - 1, trip, 0)
    if n_trips > 1:
        group(n_trips - 1, False, True)

    def finish():
        acc = acc_ref[...]
        l = acc[A_V_DIM:A_V_DIM + 1, :]
        inv = 1.0 / l
        ot = acc[:A_V_DIM, :tq] * inv[:, :tq] - lam_ref[0] * (acc[:A_V_DIM, tq:] * inv[:, tq:])
        ssq = jnp.sum(ot * ot, axis=0, keepdims=True)
        yt = ot * lax.rsqrt(ssq * (1.0 / A_V_DIM) + EPS)
        z = z_ref[...].astype(F32)
        o_ref[...] = (yt.T * g_ref[...] * (1.0 - LAM_INIT) * (z * jax.nn.sigmoid(z))).astype(BF16)
        return l, ssq

    l, ssq = finish()
    overflow = (jnp.max(jnp.where(l < OVERFLOW_GUARD, 0.0, 1.0))
                + jnp.max(jnp.where(ssq < F32_FINITE, 0.0, 1.0))) > 0.0

    @pl.when(overflow)
    def _():
        init_from_meta()
        m_ref[...] = m0

        def step(j, carry):
            s = tile_scores(j)
            m_old = m_ref[...]
            m_new = jnp.maximum(m_old, jnp.max(s, axis=0, keepdims=True))
            alpha = jnp.exp2(m_old - m_new)
            acc_ref[...] = alpha * acc_ref[...] + pv_and_sum(vt_ref[j], jnp.exp2(s - m_new).astype(BF16))
            m_ref[...] = m_new
            return carry

        lax.fori_loop(0, nk, step, 0)
        finish()


def _attn_a(lam, qat, ka, vat, ka_m, vat_m, bias, bias_m, za, subln_g, *, tq, tk):
    nb, s, _ = ka.shape
    tqs = qat.shape[-1]
    nk = s // tk
    nv, tbw = bias.shape[1], bias.shape[3]
    tb = next(c for c in (8, 4, 2, 1) if nk % c == 0)
    assert nk % tb == 0 and (nk == tb or tq // tk + 2 <= tb)
    head = lambda n: pl.BlockSpec((None, n, A_V_DIM), lambda b, h, i: (b, i, h))
    return pl.pallas_call(
        functools.partial(_attn_a_kernel, tq=tq, tk=tk, nk=nk, nv=nv, tb=tb),
        grid=(nb, A_HEADS, s // tq),
        in_specs=[
            pl.BlockSpec(memory_space=pltpu.SMEM),
            pl.BlockSpec((None, tq // tqs, A_V_DIM, tqs), lambda b, h, i: (b, i, h, 0)),
            pl.BlockSpec((None, s, A_V_DIM), lambda b, h, i: (b, 0, h)),
            pl.BlockSpec((None, nk, A_V_DIM, tk), lambda b, h, i: (b, 0, h, 0)),
            pl.BlockSpec((None, N_META, A_V_DIM), lambda b, h, i: (0, 0, h)),
            pl.BlockSpec((None, None, A_V_DIM, N_META), lambda b, h, i: (0, 0, h, 0)),
            pl.BlockSpec((None, nv, tk, tbw), lambda b, h, i: (h, 0, 0, 0)),
            pl.BlockSpec((None, N_META, tq), lambda b, h, i: (h, 0, i)),
            head(tq),
            pl.BlockSpec((1, A_V_DIM), lambda b, h, i: (0, 0)),
        ],
        out_specs=head(tq),
        out_shape=jax.ShapeDtypeStruct((nb, s, A_WIDTH), BF16),
        scratch_shapes=[
            pltpu.VMEM((A_V_DIM, 2 * tq), BF16),
            pltpu.VMEM((1, 2 * tq), F32),
            pltpu.VMEM((A_V_DIM + SUM_ROWS, 2 * tq), F32),
            pltpu.VMEM((2, tk, 2 * tq), BF16),
        ],
        compiler_params=pltpu.CompilerParams(
            dimension_semantics=("arbitrary", "arbitrary", "arbitrary"),
            vmem_limit_bytes=VMEM_LIMIT_BYTES),
        name="attn_a",
    )(lam, qat, ka, vat, ka_m, vat_m, bias, bias_m, za, subln_g)


_BT = WINDOW
_BSUB = 8
_BKT = _BT + 2 * WINDOW
_PAD_FRONT = WINDOW - N_META
_B_SLABS = B_HEADS // 2
_B_GROUP = B_HEADS // B_KV_HEADS


def _attn_b_kernel(sink_ref, q_ref, *rest, n_sub_total):
    nkb = _BSUB + 2
    k_refs, v_refs = rest[:nkb], rest[nkb:2 * nkb]
    bias_ref, z_ref, o_ref, qt_ref = rest[2 * nkb:]
    i = pl.program_id(1)
    sink_row = jnp.concatenate(
        [jnp.full((1, _BT), sink_ref[h] * LOG2E, F32) for h in range(B_HEADS)], axis=1)
    zeros = jnp.zeros((B_HEAD_DIM, _BT), BF16)

    def scores(t):
        it = i * _BSUB + t
        k = jnp.concatenate([r[...] for r in k_refs[t:t + 3]], axis=0)
        q = q_ref[t]
        for h in range(B_HEADS):
            blk = q[h * B_HEAD_DIM:(h + 1) * B_HEAD_DIM, :]
            pair = [blk, zeros] if h < _B_GROUP else [zeros, blk]
            qt_ref[t, :, h * _BT:(h + 1) * _BT] = jnp.concatenate(pair, axis=0)
        variant = (it == 0).astype(jnp.int32) + 2 * (it == n_sub_total - 1).astype(jnp.int32)
        return jnp.dot(k, qt_ref[t], preferred_element_type=F32) + bias_ref[variant]

    def finish(t, s):
        v = jnp.concatenate([r[...] for r in v_refs[t:t + 3]], axis=0)
        vt = jnp.concatenate([v.astype(F32).T.astype(BF16), jnp.ones((SUM_ROWS, _BKT), BF16)], axis=0)
        m = jnp.maximum(jnp.max(s, axis=0, keepdims=True), sink_row)
        pv = jnp.dot(vt, jnp.exp2(s - m).astype(BF16), preferred_element_type=F32)
        l = pv[2 * B_HEAD_DIM:2 * B_HEAD_DIM + 1, :] + jnp.exp2(sink_row - m)
        ot = pv[:2 * B_HEAD_DIM, :] * (1.0 / l)
        for slab in range(_B_SLABS):
            g = (2 * slab) // _B_GROUP
            rows = slice(g * B_HEAD_DIM, (g + 1) * B_HEAD_DIM)
            x = jnp.concatenate([ot[rows, (2 * slab) * _BT:(2 * slab + 1) * _BT],
                                 ot[rows, (2 * slab + 1) * _BT:(2 * slab + 2) * _BT]], axis=0)
            cols = slice(slab * LANES, (slab + 1) * LANES)
            z = z_ref[t * _BT:(t + 1) * _BT, cols].astype(F32)
            o_ref[t * _BT:(t + 1) * _BT, cols] = (x.T * (z * jax.nn.sigmoid(z))).astype(BF16)

    s_cur = scores(0)
    for t in range(_BSUB):
        s_nxt = scores(t + 1) if t + 1 < _BSUB else None
        finish(t, s_cur)
        s_cur = s_nxt


def _attn_b(sink, qbt, kb_ext, vb_ext, bias, zb):
    nb, s, _ = zb.shape
    tq = _BT * _BSUB
    row = pl.BlockSpec((None, tq, B_WIDTH), lambda b, i: (b, i, 0))
    nkb = _BSUB + 2
    kv = [pl.BlockSpec((None, WINDOW, B_KV_WIDTH), (lambda b, i, t=t: (b, _BSUB * i + t, 0)))
          for t in range(nkb)]
    return pl.pallas_call(
        functools.partial(_attn_b_kernel, n_sub_total=s // _BT),
        grid=(nb, s // tq),
        in_specs=[pl.BlockSpec(memory_space=pltpu.SMEM),
                  pl.BlockSpec((None, _BSUB, B_WIDTH, _BT), lambda b, i: (b, i, 0, 0))] + kv + kv
                 + [pl.BlockSpec((4, _BKT, B_HEADS * _BT), lambda b, i: (0, 0, 0),
                                 pipeline_mode=pl.Buffered(1)), row],
        out_specs=row,
        out_shape=jax.ShapeDtypeStruct((nb, s, B_WIDTH), BF16),
        scratch_shapes=[pltpu.VMEM((_BSUB, 2 * B_HEAD_DIM, B_HEADS * _BT), BF16)],
        compiler_params=pltpu.CompilerParams(
            dimension_semantics=("arbitrary", "arbitrary"), vmem_limit_bytes=VMEM_LIMIT_BYTES),
        name="attn_b",
    )(sink, qbt, *([kb_ext] * nkb), *([vb_ext] * nkb), bias, zb)


def _out_kernel(x_ref, a_ref, b_ref, ga_ref, gb_ref, woa_ref, wob_ref, wo_ref, g_ref, o_ref):
    ya = jnp.dot(a_ref[...], woa_ref[...], preferred_element_type=F32)
    yb = jnp.dot(b_ref[...], wob_ref[...], preferred_element_type=F32)
    mixed = (jax.nn.sigmoid(ga_ref[...].astype(F32)) * ya
             + jax.nn.sigmoid(gb_ref[...].astype(F32)) * yb)
    y = jnp.dot(mixed.astype(BF16), wo_ref[...], preferred_element_type=F32)
    y = y * lax.rsqrt(jnp.mean(y * y, axis=-1, keepdims=True) + EPS) * g_ref[...]
    o_ref[...] = x_ref[...] + y


def _output(x3, a, b, ga, gb, woa, wob, wo, g, *, tm):
    nb, s, d = x3.shape
    row = lambda n: pl.BlockSpec((None, tm, n), lambda bi, i: (bi, i, 0))
    const = lambda shape: pl.BlockSpec(shape, lambda bi, i: (0, 0), pipeline_mode=pl.Buffered(1))
    return pl.pallas_call(
        _out_kernel,
        grid=(nb, s // tm),
        in_specs=[row(d), row(A_WIDTH), row(B_WIDTH), row(d), row(d),
                  const((A_WIDTH, d)), const((B_WIDTH, d)), const((d, d)), const((1, d))],
        out_specs=row(d),
        out_shape=jax.ShapeDtypeStruct((nb, s, d), F32),
        compiler_params=pltpu.CompilerParams(
            dimension_semantics=("arbitrary", "arbitrary"), vmem_limit_bytes=VMEM_LIMIT_BYTES),
        name="out_proj",
    )(x3, a, b, ga, gb, woa, wob, wo, g)


def kernel(x, meta_tokens, rel_bias, pre_norm_g, w_in, b_in, lambda_q1, lambda_k1, lambda_q2, lambda_k2,
           subln_g, sink, w_out_a, w_out_b, w_out, post_norm_g):
    nb, s, d = x.shape
    tm = min(1024, s)
    tq = min(1024, s)
    tk = min(256, s)
    assert s % tm == 0 and s % tq == 0 and tq % tk == 0 and max(tq, tm) % min(tq, tm) == 0 and tm % _BT == 0 and s % (_BT * _BSUB) == 0

    w = w_in[0].astype(BF16)
    b = b_in[0][None, :]
    g_pre = pre_norm_g[0][None, :]
    lam = (jnp.exp(jnp.sum(lambda_q1[0].astype(F32) * lambda_k1[0].astype(F32)))
           - jnp.exp(jnp.sum(lambda_q2[0].astype(F32) * lambda_k2[0].astype(F32))) + LAM_INIT).reshape(1)

    qat, ka, vat, za, qbt, kb, vb, zb, ga, gb = _project(x, g_pre, w, b, tm=tm, tq=min(tq, tm), tqb=_BT, tkv=tk)
    meta = _project(meta_tokens[None], g_pre, w, b, tm=N_META, tq=N_META, tqb=N_META, tkv=N_META)
    ka_m, vat_m, kb_m, vb_m = meta[1], meta[2], meta[5], meta[6]

    tbw = tk
    nv = tbw // tk + 4
    off = (jnp.arange(nv, dtype=jnp.int32) - 2)[:, None, None] * tk
    rel_a = off + jnp.arange(tk, dtype=jnp.int32)[None, :, None] - jnp.arange(tbw, dtype=jnp.int32)[None, None, :]
    bias_a = _bias_tiles(rel_bias, _t5_bucket(rel_a).reshape(nv * tk, tbw), head0=0, n_heads=A_HEADS,
                         tr=tk).reshape(A_HEADS, nv, tk, tbw)
    rel_m = jnp.arange(N_META, dtype=jnp.int32)[:, None] - (jnp.arange(s, dtype=jnp.int32)[None, :] + N_META)
    bias_m = _bias_tiles(rel_bias, _t5_bucket(rel_m), head0=0, n_heads=A_HEADS, tr=N_META)
    key_r = jnp.arange(_BKT, dtype=jnp.int32)[:, None]
    rel_b = (key_r - WINDOW) - jnp.arange(_BT, dtype=jnp.int32)[None, :]
    in_win = jnp.abs(rel_b) <= WINDOW
    idx_b = jnp.concatenate([
        jnp.where(in_win & ((key_r >= _PAD_FRONT) | (not first)) & ((key_r < _BT + WINDOW) | (not last)),
                  _t5_bucket(rel_b), N_BUCKETS)
        for last in (False, True) for first in (False, True)], axis=0)
    bias_b = _bias_tiles(rel_bias, idx_b, head0=A_HEADS, n_heads=B_HEADS, tr=_BKT,
                         heads_on_lanes=True).reshape(4, _BKT, B_HEADS * _BT)

    oa = _attn_a(lam, qat, ka, vat, ka_m, vat_m, bias_a, bias_m, za, subln_g[0][None, :], tq=tq, tk=tk)

    def extend(tok, m):
        return jnp.concatenate([
            jnp.zeros((nb, _PAD_FRONT, B_KV_WIDTH), BF16),
            jnp.broadcast_to(m, (nb, N_META, B_KV_WIDTH)), tok,
            jnp.zeros((nb, WINDOW, B_KV_WIDTH), BF16)], axis=1)

    ob = _attn_b(sink[0], qbt, extend(kb, kb_m), extend(vb, vb_m), bias_b, zb)

    return _output(x, oa, ob, ga, gb, w_out_a[0].astype(BF16), w_out_b[0].astype(BF16),
                   w_out[0].astype(BF16), post_norm_g[0][None, :], tm=min(1024, s))
```

```python
import functools
import math
from typing import NamedTuple

import jax
import jax.numpy as jnp
from jax import lax
from jax.experimental import pallas as pl
from jax.experimental.pallas import tpu as pltpu

N_META = 16
WINDOW = 128
A_HEADS = 4
A_QK_DIM = 64
A_V_DIM = 2 * A_QK_DIM
A_WIDTH = A_HEADS * A_V_DIM
B_HEADS = 8
B_KV_HEADS = 2
B_HEAD_DIM = 64
B_WIDTH = B_HEADS * B_HEAD_DIM
B_KV_WIDTH = B_KV_HEADS * B_HEAD_DIM
N_BUCKETS = 32
MAX_DISTANCE = 128
EPS = 1e-6
NEG = -1e30
LOG2E = math.log2(math.e)
LAM_INIT = 0.8 - 0.6 * math.exp(-0.3 * 0)

LANES = 128
OVERFLOW_GUARD = 2.0 ** 100
F32_FINITE = 3.0e38
SUM_ROWS = 16
VMEM_LIMIT_BYTES = 56 * 1024 * 1024

_COLS = (A_WIDTH, A_WIDTH, A_WIDTH, A_WIDTH, B_WIDTH, B_KV_WIDTH, B_KV_WIDTH, B_WIDTH)
BF16 = jnp.bfloat16
F32 = jnp.float32


def _t5_bucket(rel):
    half = N_BUCKETS // 2
    max_exact = half // 2
    ret = jnp.where(rel > 0, half, 0)
    n = jnp.abs(rel)
    nf = jnp.maximum(n, 1).astype(F32)
    large = max_exact + (jnp.log(nf / max_exact) / math.log(MAX_DISTANCE / max_exact)
                         * (half - max_exact)).astype(jnp.int32)
    large = jnp.minimum(large, half - 1)
    return ret + jnp.where(n < max_exact, n, large)


def _proj_kernel(x_ref, g_ref, w_ref, b_ref,
                 qat_ref, ka_ref, vat_ref, za_ref, qbt_ref, kb_ref, vb_ref, zb_ref, ga_ref, gb_ref,
                 *, d_model, tq, tqb, tkv):
    x = x_ref[...]
    hn = (x * lax.rsqrt(jnp.mean(x * x, axis=-1, keepdims=True) + EPS) * g_ref[...]).astype(BF16)

    def seg(c0, n):
        return jnp.dot(hn, w_ref[:, c0:c0 + n], preferred_element_type=F32) + b_ref[:, c0:c0 + n]

    qscale = (A_QK_DIM ** -0.5) * LOG2E
    c = 0
    qa = seg(c, A_WIDTH) * qscale; c += A_WIDTH
    for t in range(qat_ref.shape[0]):
        qat_ref[t] = qa[t * tq:(t + 1) * tq, :].T.astype(BF16)
    ka_ref[...] = seg(c, A_WIDTH).astype(BF16); c += A_WIDTH
    va = seg(c, A_WIDTH); c += A_WIDTH
    for t in range(vat_ref.shape[0]):
        vat_ref[t] = va[t * tkv:(t + 1) * tkv, :].T.astype(BF16)
    za_ref[...] = seg(c, A_WIDTH).astype(BF16); c += A_WIDTH
    qb = seg(c, B_WIDTH) * qscale; c += B_WIDTH
    for t in range(qbt_ref.shape[0]):
        qbt_ref[t] = qb[t * tqb:(t + 1) * tqb, :].T.astype(BF16)
    kb_ref[...] = seg(c, B_KV_WIDTH).astype(BF16); c += B_KV_WIDTH
    vb_ref[...] = seg(c, B_KV_WIDTH).astype(BF16); c += B_KV_WIDTH
    zb_ref[...] = seg(c, B_WIDTH).astype(BF16); c += B_WIDTH
    ga_ref[...] = seg(c, d_model).astype(BF16); c += d_model
    gb_ref[...] = seg(c, d_model).astype(BF16)


def _project(x3, g, w, b, *, tm, tq, tqb, tkv):
    nb, s, d = x3.shape
    d_in = w.shape[1]
    widths = _COLS + (d, d)
    nt = s // tm
    row = lambda n: pl.BlockSpec((None, tm, n), lambda bi, i: (bi, i, 0))
    const = lambda shape: pl.BlockSpec(shape, lambda bi, i: (0, 0), pipeline_mode=pl.Buffered(1))
    out_specs, out_shapes = [], []
    for idx, n in enumerate(widths):
        if idx in (0, 2, 4):
            tt = {0: tq, 2: tkv, 4: tqb}[idx]
            out_specs.append(pl.BlockSpec((None, tm // tt, n, tt), lambda bi, i: (bi, i, 0, 0)))
            out_shapes.append(jax.ShapeDtypeStruct((nb, s // tt, n, tt), BF16))
        else:
            out_specs.append(row(n))
            out_shapes.append(jax.ShapeDtypeStruct((nb, s, n), BF16))
    return pl.pallas_call(
        functools.partial(_proj_kernel, d_model=d, tq=tq, tqb=tqb, tkv=tkv),
        grid=(nb, nt),
        in_specs=[row(d), const((1, d)), const((d, d_in)), const((1, d_in))],
        out_specs=out_specs,
        out_shape=out_shapes,
        compiler_params=pltpu.CompilerParams(
            dimension_semantics=("arbitrary", "arbitrary"), vmem_limit_bytes=VMEM_LIMIT_BYTES),
        name="proj",
    )(x3, g, w, b)


def _bias_kernel(tab_ref, idx_ref, o_ref, *, head0):
    h = pl.program_id(0) + head0
    idx = idx_ref[...]
    acc = jnp.full(idx.shape, NEG, F32)
    for bkt in range(N_BUCKETS):
        acc = jnp.where(idx == bkt, tab_ref[bkt, h] * LOG2E, acc)
    o_ref[...] = acc


def _bias_tiles(tab, idx, *, head0, n_heads, tr, heads_on_lanes=False):
    r, c = idx.shape
    if heads_on_lanes:
        out_spec = pl.BlockSpec((tr, c), lambda h, i: (i, h))
        out_shape = jax.ShapeDtypeStruct((r, n_heads * c), F32)
    else:
        out_spec = pl.BlockSpec((None, tr, c), lambda h, i: (h, i, 0))
        out_shape = jax.ShapeDtypeStruct((n_heads, r, c), F32)
    return pl.pallas_call(
        functools.partial(_bias_kernel, head0=head0),
        grid=(n_heads, r // tr),
        in_specs=[pl.BlockSpec(memory_space=pltpu.SMEM),
                  pl.BlockSpec((tr, c), lambda h, i: (i, 0))],
        out_specs=out_spec,
        out_shape=out_shape,
        compiler_params=pltpu.CompilerParams(dimension_semantics=("arbitrary", "arbitrary")),
        name="bias_tiles",
    )(tab, idx)


def _attn_a_kernel(lam_ref, q_ref, k_ref, vt_ref, km_ref, vmt_ref, bias_ref, bm_ref, z_ref, g_ref,
                   o_ref, qt_ref, m_ref, acc_ref, p_ref, *, tq, tk, nk, nv, tb):
    i = pl.program_id(2)
    q = jnp.concatenate([q_ref[u] for u in range(q_ref.shape[0])], axis=1)
    row = lax.broadcasted_iota(jnp.int32, q.shape, 0)
    qt_ref[:, :tq] = jnp.where(row < A_QK_DIM, q, jnp.zeros_like(q))
    qt_ref[:, tq:] = jnp.where(row >= A_QK_DIM, q, jnp.zeros_like(q))
    qt = qt_ref[...]

    def both(fn):
        return jnp.concatenate([fn(slice(0, tq)), fn(slice(tq, 2 * tq))], axis=1)

    def pv_and_sum(vt, pb):
        vt1 = jnp.concatenate([vt, jnp.ones((SUM_ROWS, vt.shape[1]), BF16)], axis=0)
        return both(lambda c: jnp.dot(vt1, pb[:, c], preferred_element_type=F32))

    def raw_scores(j):
        ks = pl.multiple_of(j * tk, tk)
        return jnp.dot(k_ref[pl.ds(ks, tk), :], qt, preferred_element_type=F32)

    def bias2(j):
        tbw = bias_ref.shape[-1]
        pieces = [bias_ref[jnp.clip(j - (i * (tq // tbw) + u) * (tbw // tk) + 2, 0, nv - 1)]
                  for u in range(tq // tbw)]
        return jnp.concatenate(pieces + pieces, axis=1)

    def tile_scores(j):
        return raw_scores(j) + bias2(j)

    bm = bm_ref[...]
    s_meta = jnp.dot(km_ref[...], qt, preferred_element_type=F32) + jnp.concatenate([bm, bm], axis=1)
    m0 = jnp.max(s_meta, axis=0, keepdims=True)

    def init_from_meta():
        acc_ref[...] = pv_and_sum(vmt_ref[...], jnp.exp2(s_meta - m0).astype(BF16))

    start = i * (tq // tk) - 1
    far_lo = jnp.concatenate([bias_ref[0, 0:1, :]] * (2 * tq // bias_ref.shape[-1]), axis=1)
    far_hi = jnp.concatenate([bias_ref[nv - 1, 0:1, :]] * (2 * tq // bias_ref.shape[-1]), axis=1)
    shift_lo, shift_hi = m0 - far_lo, m0 - far_hi

    def tile_at(pos):
        j = start + pos
        wrapped = j >= nk
        j = jnp.where(wrapped, j - nk, j)
        return jnp.where(j < 0, j + nk, j), wrapped

    def stage_p(pos, slot, near):
        j, wrapped = tile_at(pos)
        if near:
            p = jnp.exp2(tile_scores(j) - m0)
        else:
            p = jnp.exp2(raw_scores(j) - jnp.where(wrapped, shift_lo, shift_hi))
        p_ref[slot] = p.astype(BF16)

    def group(g, first, last):
        pv = None
        for u in range(tb):
            pos = g * tb + u
            if not (last and u == tb - 1):
                stage_p(pos + 1, (u + 1) % 2, first and u + 1 < tb)
            d = pv_and_sum(vt_ref[tile_at(pos)[0]], p_ref[u % 2])
            pv = d if pv is None else pv + d
        acc_ref[...] += pv

    def trip(g, carry):
        group(g, False, False)
        return carry

    stage_p(0, 0, True)
    init_from_meta()

    n_trips = nk // tb
    group(0, True, n_trips == 1)
    lax.fori_loop(1, n_trips - 1, trip, 0)
    if n_trips > 1:
        group(n_trips - 1, False, True)

    def finish():
        acc = acc_ref[...]
        l = acc[A_V_DIM:A_V_DIM + 1, :]
        inv = 1.0 / l
        ot = acc[:A_V_DIM, :tq] * inv[:, :tq] - lam_ref[0] * (acc[:A_V_DIM, tq:] * inv[:, tq:])
        ssq = jnp.sum(ot * ot, axis=0, keepdims=True)
        yt = ot * lax.rsqrt(ssq * (1.0 / A_V_DIM) + EPS)
        z = z_ref[...].astype(F32)
        o_ref[...] = (yt.T * g_ref[...] * (1.0 - LAM_INIT) * (z * jax.nn.sigmoid(z))).astype(BF16)
        return l, ssq

    l, ssq = finish()
    overflow = (jnp.max(jnp.where(l < OVERFLOW_GUARD, 0.0, 1.0))
                + jnp.max(jnp.where(ssq < F32_FINITE, 0.0, 1.0))) > 0.0

    @pl.when(overflow)
    def _():
        init_from_meta()
        m_ref[...] = m0

        def step(j, carry):
            s = tile_scores(j)
            m_old = m_ref[...]
            m_new = jnp.maximum(m_old, jnp.max(s, axis=0, keepdims=True))
            alpha = jnp.exp2(m_old - m_new)
            acc_ref[...] = alpha * acc_ref[...] + pv_and_sum(vt_ref[j], jnp.exp2(s - m_new).astype(BF16))
            m_ref[...] = m_new
            return carry

        lax.fori_loop(0, nk, step, 0)
        finish()


def _attn_a(lam, qat, ka, vat, ka_m, vat_m, bias, bias_m, za, subln_g, *, tq, tk):
    nb, s, _ = ka.shape
    tqs = qat.shape[-1]
    nk = s // tk
    nv, tbw = bias.shape[1], bias.shape[3]
    tb = next(c for c in (8, 4, 2, 1) if nk % c == 0)
    assert nk % tb == 0 and (nk == tb or tq // tk + 2 <= tb)
    head = lambda n: pl.BlockSpec((None, n, A_V_DIM), lambda b, h, i: (b, i, h))
    return pl.pallas_call(
        functools.partial(_attn_a_kernel, tq=tq, tk=tk, nk=nk, nv=nv, tb=tb),
        grid=(nb, A_HEADS, s // tq),
        in_specs=[
            pl.BlockSpec(memory_space=pltpu.SMEM),
            pl.BlockSpec((None, tq // tqs, A_V_DIM, tqs), lambda b, h, i: (b, i, h, 0)),
            pl.BlockSpec((None, s, A_V_DIM), lambda b, h, i: (b, 0, h)),
            pl.BlockSpec((None, nk, A_V_DIM, tk), lambda b, h, i: (b, 0, h, 0)),
            pl.BlockSpec((None, N_META, A_V_DIM), lambda b, h, i: (0, 0, h)),
            pl.BlockSpec((None, None, A_V_DIM, N_META), lambda b, h, i: (0, 0, h, 0)),
            pl.BlockSpec((None, nv, tk, tbw), lambda b, h, i: (h, 0, 0, 0)),
            pl.BlockSpec((None, N_META, tq), lambda b, h, i: (h, 0, i)),
            head(tq),
            pl.BlockSpec((1, A_V_DIM), lambda b, h, i: (0, 0)),
        ],
        out_specs=head(tq),
        out_shape=jax.ShapeDtypeStruct((nb, s, A_WIDTH), BF16),
        scratch_shapes=[
            pltpu.VMEM((A_V_DIM, 2 * tq), BF16),
            pltpu.VMEM((1, 2 * tq), F32),
            pltpu.VMEM((A_V_DIM + SUM_ROWS, 2 * tq), F32),
            pltpu.VMEM((2, tk, 2 * tq), BF16),
        ],
        compiler_params=pltpu.CompilerParams(
            dimension_semantics=("arbitrary", "arbitrary", "arbitrary"),
            vmem_limit_bytes=VMEM_LIMIT_BYTES),
        name="attn_a",
    )(lam, qat, ka, vat, ka_m, vat_m, bias, bias_m, za, subln_g)


_BT = WINDOW
_BSUB = 8
_BKT = _BT + 2 * WINDOW
_PAD_FRONT = WINDOW - N_META
_B_SLABS = B_HEADS // 2
_B_GROUP = B_HEADS // B_KV_HEADS


def _attn_b_kernel(sink_ref, q_ref, *rest, n_sub_total):
    nkb = _BSUB + 2
    k_refs, v_refs = rest[:nkb], rest[nkb:2 * nkb]
    bias_ref, z_ref, o_ref, qt_ref = rest[2 * nkb:]
    i = pl.program_id(1)
    sink_row = jnp.concatenate(
        [jnp.full((1, _BT), sink_ref[h] * LOG2E, F32) for h in range(B_HEADS)], axis=1)
    zeros = jnp.zeros((B_HEAD_DIM, _BT), BF16)

    def scores(t):
        it = i * _BSUB + t
        k = jnp.concatenate([r[...] for r in k_refs[t:t + 3]], axis=0)
        q = q_ref[t]
        for h in range(B_HEADS):
            blk = q[h * B_HEAD_DIM:(h + 1) * B_HEAD_DIM, :]
            pair = [blk, zeros] if h < _B_GROUP else [zeros, blk]
            qt_ref[t, :, h * _BT:(h + 1) * _BT] = jnp.concatenate(pair, axis=0)
        variant = (it == 0).astype(jnp.int32) + 2 * (it == n_sub_total - 1).astype(jnp.int32)
        return jnp.dot(k, qt_ref[t], preferred_element_type=F32) + bias_ref[variant]

    def finish(t, s):
        v = jnp.concatenate([r[...] for r in v_refs[t:t + 3]], axis=0)
        vt = jnp.concatenate([v.astype(F32).T.astype(BF16), jnp.ones((SUM_ROWS, _BKT), BF16)], axis=0)
        m = jnp.maximum(jnp.max(s, axis=0, keepdims=True), sink_row)
        pv = jnp.dot(vt, jnp.exp2(s - m).astype(BF16), preferred_element_type=F32)
        l = pv[2 * B_HEAD_DIM:2 * B_HEAD_DIM + 1, :] + jnp.exp2(sink_row - m)
        ot = pv[:2 * B_HEAD_DIM, :] * (1.0 / l)
        for slab in range(_B_SLABS):
            g = (2 * slab) // _B_GROUP
            rows = slice(g * B_HEAD_DIM, (g + 1) * B_HEAD_DIM)
            x = jnp.concatenate([ot[rows, (2 * slab) * _BT:(2 * slab + 1) * _BT],
                                 ot[rows, (2 * slab + 1) * _BT:(2 * slab + 2) * _BT]], axis=0)
            cols = slice(slab * LANES, (slab + 1) * LANES)
            z = z_ref[t * _BT:(t + 1) * _BT, cols].astype(F32)
            o_ref[t * _BT:(t + 1) * _BT, cols] = (x.T * (z * jax.nn.sigmoid(z))).astype(BF16)

    s_cur = scores(0)
    for t in range(_BSUB):
        s_nxt = scores(t + 1) if t + 1 < _BSUB else None
        finish(t, s_cur)
        s_cur = s_nxt


def _attn_b(sink, qbt, kb_ext, vb_ext, bias, zb):
    nb, s, _ = zb.shape
    tq = _BT * _BSUB
    row = pl.BlockSpec((None, tq, B_WIDTH), lambda b, i: (b, i, 0))
    nkb = _BSUB + 2
    kv = [pl.BlockSpec((None, WINDOW, B_KV_WIDTH), (lambda b, i, t=t: (b, _BSUB * i + t, 0)))
          for t in range(nkb)]
    return pl.pallas_call(
        functools.partial(_attn_b_kernel, n_sub_total=s // _BT),
        grid=(nb, s // tq),
        in_specs=[pl.BlockSpec(memory_space=pltpu.SMEM),
                  pl.BlockSpec((None, _BSUB, B_WIDTH, _BT), lambda b, i: (b, i, 0, 0))] + kv + kv
                 + [pl.BlockSpec((4, _BKT, B_HEADS * _BT), lambda b, i: (0, 0, 0),
                                 pipeline_mode=pl.Buffered(1)), row],
        out_specs=row,
        out_shape=jax.ShapeDtypeStruct((nb, s, B_WIDTH), BF16),
        scratch_shapes=[pltpu.VMEM((_BSUB, 2 * B_HEAD_DIM, B_HEADS * _BT), BF16)],
        compiler_params=pltpu.CompilerParams(
            dimension_semantics=("arbitrary", "arbitrary"), vmem_limit_bytes=VMEM_LIMIT_BYTES),
        name="attn_b",
    )(sink, qbt, *([kb_ext] * nkb), *([vb_ext] * nkb), bias, zb)


def _out_kernel(x_ref, a_ref, b_ref, ga_ref, gb_ref, woa_ref, wob_ref, wo_ref, g_ref, o_ref):
    ya = jnp.dot(a_ref[...], woa_ref[...], preferred_element_type=F32)
    yb = jnp.dot(b_ref[...], wob_ref[...], preferred_element_type=F32)
    mixed = (jax.nn.sigmoid(ga_ref[...].astype(F32)) * ya
             + jax.nn.sigmoid(gb_ref[...].astype(F32)) * yb)
    y = jnp.dot(mixed.astype(BF16), wo_ref[...], preferred_element_type=F32)
    y = y * lax.rsqrt(jnp.mean(y * y, axis=-1, keepdims=True) + EPS) * g_ref[...]
    o_ref[...] = x_ref[...] + y


def _output(x3, a, b, ga, gb, woa, wob, wo, g, *, tm):
    nb, s, d = x3.shape
    row = lambda n: pl.BlockSpec((None, tm, n), lambda bi, i: (bi, i, 0))
    const = lambda shape: pl.BlockSpec(shape, lambda bi, i: (0, 0), pipeline_mode=pl.Buffered(1))
    return pl.pallas_call(
        _out_kernel,
        grid=(nb, s // tm),
        in_specs=[row(d), row(A_WIDTH), row(B_WIDTH), row(d), row(d),
                  const((A_WIDTH, d)), const((B_WIDTH, d)), const((d, d)), const((1, d))],
        out_specs=row(d),
        out_shape=jax.ShapeDtypeStruct((nb, s, d), F32),
        compiler_params=pltpu.CompilerParams(
            dimension_semantics=("arbitrary", "arbitrary"), vmem_limit_bytes=VMEM_LIMIT_BYTES),
        name="out_proj",
    )(x3, a, b, ga, gb, woa, wob, wo, g)


class _Tiles(NamedTuple):
    tm: int
    tq: int
    tk: int
    tqs: int


def _tiles(s):
    t = _Tiles(tm=min(1024, s), tq=min(1024, s), tk=min(256, s), tqs=min(1024, s))
    assert s % t.tm == 0 and s % t.tq == 0 and t.tq % t.tk == 0 and t.tm % t.tqs == 0 and t.tq % t.tqs == 0
    assert t.tm % _BT == 0 and s % (_BT * _BSUB) == 0
    return t


def kernel(x, meta_tokens, rel_bias, pre_norm_g, w_in, b_in, lambda_q1, lambda_k1, lambda_q2, lambda_k2,
           subln_g, sink, w_out_a, w_out_b, w_out, post_norm_g):
    nb, s, d = x.shape
    tm, tq, tk, tqs = _tiles(s)

    w = w_in[0].astype(BF16)
    b = b_in[0][None, :]
    g_pre = pre_norm_g[0][None, :]
    lam = (jnp.exp(jnp.sum(lambda_q1[0].astype(F32) * lambda_k1[0].astype(F32)))
           - jnp.exp(jnp.sum(lambda_q2[0].astype(F32) * lambda_k2[0].astype(F32))) + LAM_INIT).reshape(1)

    qat, ka, vat, za, qbt, kb, vb, zb, ga, gb = _project(x, g_pre, w, b, tm=tm, tq=tqs, tqb=_BT, tkv=tk)
    meta = _project(meta_tokens[None], g_pre, w, b, tm=N_META, tq=N_META, tqb=N_META, tkv=N_META)
    ka_m, vat_m, kb_m, vb_m = meta[1], meta[2], meta[5], meta[6]

    tbw = tk
    nv = tbw // tk + 4
    off = (jnp.arange(nv, dtype=jnp.int32) - 2)[:, None, None] * tk
    rel_a = off + jnp.arange(tk, dtype=jnp.int32)[None, :, None] - jnp.arange(tbw, dtype=jnp.int32)[None, None, :]
    bias_a = _bias_tiles(rel_bias, _t5_bucket(rel_a).reshape(nv * tk, tbw), head0=0, n_heads=A_HEADS,
                         tr=tk).reshape(A_HEADS, nv, tk, tbw)
    rel_m = jnp.arange(N_META, dtype=jnp.int32)[:, None] - (jnp.arange(s, dtype=jnp.int32)[None, :] + N_META)
    bias_m = _bias_tiles(rel_bias, _t5_bucket(rel_m), head0=0, n_heads=A_HEADS, tr=N_META)
    key_r = jnp.arange(_BKT, dtype=jnp.int32)[:, None]
    rel_b = (key_r - WINDOW) - jnp.arange(_BT, dtype=jnp.int32)[None, :]
    idx_b = jnp.where(jnp.abs(rel_b) <= WINDOW, _t5_bucket(rel_b), N_BUCKETS)
    bias_b = _bias_tiles(rel_bias, idx_b, head0=A_HEADS, n_heads=B_HEADS, tr=_BKT, heads_on_lanes=True)
    bias_b = jnp.stack([
        jnp.where(((key_r >= _PAD_FRONT) | (not first)) & ((key_r < _BT + WINDOW) | (not last)), bias_b, NEG)
        for last in (False, True) for first in (False, True)])

    oa = _attn_a(lam, qat, ka, vat, ka_m, vat_m, bias_a, bias_m, za, subln_g[0][None, :], tq=tq, tk=tk)

    def extend(tok, m):
        return jnp.concatenate([
            jnp.zeros((nb, _PAD_FRONT, B_KV_WIDTH), BF16),
            jnp.broadcast_to(m, (nb, N_META, B_KV_WIDTH)), tok,
            jnp.zeros((nb, WINDOW, B_KV_WIDTH), BF16)], axis=1)

    ob = _attn_b(sink[0], qbt, extend(kb, kb_m), extend(vb, vb_m), bias_b, zb)

    return _output(x, oa, ob, ga, gb, w_out_a[0].astype(BF16), w_out_b[0].astype(BF16),
                   w_out[0].astype(BF16), post_norm_g[0][None, :], tm=tm)
```

```python
import functools
import math
from typing import NamedTuple

import jax
import jax.numpy as jnp
from jax import lax
from jax.experimental import pallas as pl
from jax.experimental.pallas import tpu as pltpu

N_META = 16
WINDOW = 128
A_HEADS = 4
A_QK_DIM = 64
A_V_DIM = 2 * A_QK_DIM
A_WIDTH = A_HEADS * A_V_DIM
B_HEADS = 8
B_KV_HEADS = 2
B_HEAD_DIM = 64
B_WIDTH = B_HEADS * B_HEAD_DIM
B_KV_WIDTH = B_KV_HEADS * B_HEAD_DIM
N_BUCKETS = 32
MAX_DISTANCE = 128
EPS = 1e-6
NEG = -1e30
LOG2E = math.log2(math.e)
LAM_INIT = 0.8 - 0.6 * math.exp(-0.3 * 0)

LANES = 128
OVERFLOW_GUARD = 2.0 ** 100
F32_FINITE = 3.0e38
SUM_ROWS = 16
VMEM_LIMIT_BYTES = 56 * 1024 * 1024

_COLS = (A_WIDTH, A_WIDTH, A_WIDTH, A_WIDTH, B_WIDTH, B_KV_WIDTH, B_KV_WIDTH, B_WIDTH)
BF16 = jnp.bfloat16
F32 = jnp.float32


def _t5_bucket(rel):
    half = N_BUCKETS // 2
    max_exact = half // 2
    ret = jnp.where(rel > 0, half, 0)
    n = jnp.abs(rel)
    nf = jnp.maximum(n, 1).astype(F32)
    large = max_exact + (jnp.log(nf / max_exact) / math.log(MAX_DISTANCE / max_exact)
                         * (half - max_exact)).astype(jnp.int32)
    large = jnp.minimum(large, half - 1)
    return ret + jnp.where(n < max_exact, n, large)


def _proj_kernel(x_ref, g_ref, w_ref, b_ref,
                 qat_ref, ka_ref, vat_ref, za_ref, qbt_ref, kb_ref, vb_ref, zb_ref, ga_ref, gb_ref,
                 *, d_model, tq, tqb, tkv):
    x = x_ref[...]
    hn = (x * lax.rsqrt(jnp.mean(x * x, axis=-1, keepdims=True) + EPS) * g_ref[...]).astype(BF16)

    def seg(c0, n):
        return jnp.dot(hn, w_ref[:, c0:c0 + n], preferred_element_type=F32) + b_ref[:, c0:c0 + n]

    qscale = (A_QK_DIM ** -0.5) * LOG2E
    c = 0
    qa = seg(c, A_WIDTH) * qscale; c += A_WIDTH
    for t in range(qat_ref.shape[0]):
        qat_ref[t] = qa[t * tq:(t + 1) * tq, :].T.astype(BF16)
    ka_ref[...] = seg(c, A_WIDTH).astype(BF16); c += A_WIDTH
    va = seg(c, A_WIDTH); c += A_WIDTH
    for t in range(vat_ref.shape[0]):
        vat_ref[t] = va[t * tkv:(t + 1) * tkv, :].T.astype(BF16)
    za_ref[...] = seg(c, A_WIDTH).astype(BF16); c += A_WIDTH
    qb = seg(c, B_WIDTH) * qscale; c += B_WIDTH
    for t in range(qbt_ref.shape[0]):
        qbt_ref[t] = qb[t * tqb:(t + 1) * tqb, :].T.astype(BF16)
    kb_ref[...] = seg(c, B_KV_WIDTH).astype(BF16); c += B_KV_WIDTH
    vb_ref[...] = seg(c, B_KV_WIDTH).astype(BF16); c += B_KV_WIDTH
    zb_ref[...] = seg(c, B_WIDTH).astype(BF16); c += B_WIDTH
    ga_ref[...] = seg(c, d_model).astype(BF16); c += d_model
    gb_ref[...] = seg(c, d_model).astype(BF16)


def _project(x3, g, w, b, *, tm, tq, tqb, tkv):
    nb, s, d = x3.shape
    d_in = w.shape[1]
    widths = _COLS + (d, d)
    nt = s // tm
    row = lambda n: pl.BlockSpec((None, tm, n), lambda bi, i: (bi, i, 0))
    const = lambda shape: pl.BlockSpec(shape, lambda bi, i: (0, 0), pipeline_mode=pl.Buffered(1))
    out_specs, out_shapes = [], []
    for idx, n in enumerate(widths):
        if idx in (0, 2, 4):
            tt = {0: tq, 2: tkv, 4: tqb}[idx]
            out_specs.append(pl.BlockSpec((None, tm // tt, n, tt), lambda bi, i: (bi, i, 0, 0)))
            out_shapes.append(jax.ShapeDtypeStruct((nb, s // tt, n, tt), BF16))
        else:
            out_specs.append(row(n))
            out_shapes.append(jax.ShapeDtypeStruct((nb, s, n), BF16))
    return pl.pallas_call(
        functools.partial(_proj_kernel, d_model=d, tq=tq, tqb=tqb, tkv=tkv),
        grid=(nb, nt),
        in_specs=[row(d), const((1, d)), const((d, d_in)), const((1, d_in))],
        out_specs=out_specs,
        out_shape=out_shapes,
        compiler_params=pltpu.CompilerParams(
            dimension_semantics=("arbitrary", "arbitrary"), vmem_limit_bytes=VMEM_LIMIT_BYTES),
        name="proj",
    )(x3, g, w, b)


def _bias_kernel(tab_ref, idx_ref, o_ref, *, head0):
    h = pl.program_id(0) + head0
    idx = idx_ref[...]
    acc = jnp.full(idx.shape, NEG, F32)
    for bkt in range(N_BUCKETS):
        acc = jnp.where(idx == bkt, tab_ref[bkt, h] * LOG2E, acc)
    o_ref[...] = acc


def _bias_tiles(tab, idx, *, head0, n_heads, tr, heads_on_lanes=False):
    r, c = idx.shape
    if heads_on_lanes:
        out_spec = pl.BlockSpec((tr, c), lambda h, i: (i, h))
        out_shape = jax.ShapeDtypeStruct((r, n_heads * c), F32)
    else:
        out_spec = pl.BlockSpec((None, tr, c), lambda h, i: (h, i, 0))
        out_shape = jax.ShapeDtypeStruct((n_heads, r, c), F32)
    return pl.pallas_call(
        functools.partial(_bias_kernel, head0=head0),
        grid=(n_heads, r // tr),
        in_specs=[pl.BlockSpec(memory_space=pltpu.SMEM),
                  pl.BlockSpec((tr, c), lambda h, i: (i, 0))],
        out_specs=out_spec,
        out_shape=out_shape,
        compiler_params=pltpu.CompilerParams(dimension_semantics=("arbitrary", "arbitrary")),
        name="bias_tiles",
    )(tab, idx)


def _attn_a_kernel(lam_ref, q_ref, k_ref, vt_ref, km_ref, vmt_ref, bias_ref, bm_ref, z_ref, g_ref,
                   o_ref, qt_ref, m_ref, acc_ref, p_ref, *, tq, tk, nk, nv, tb):
    i = pl.program_id(2)
    q = jnp.concatenate([q_ref[u] for u in range(q_ref.shape[0])], axis=1)
    row = lax.broadcasted_iota(jnp.int32, q.shape, 0)
    qt_ref[:, :tq] = jnp.where(row < A_QK_DIM, q, jnp.zeros_like(q))
    qt_ref[:, tq:] = jnp.where(row >= A_QK_DIM, q, jnp.zeros_like(q))
    qt = qt_ref[...]

    def both(fn):
        return jnp.concatenate([fn(slice(0, tq)), fn(slice(tq, 2 * tq))], axis=1)

    def pv_and_sum(vt, pb):
        vt1 = jnp.concatenate([vt, jnp.ones((SUM_ROWS, vt.shape[1]), BF16)], axis=0)
        return both(lambda c: jnp.dot(vt1, pb[:, c], preferred_element_type=F32))

    def raw_scores(j):
        ks = pl.multiple_of(j * tk, tk)
        return jnp.dot(k_ref[pl.ds(ks, tk), :], qt, preferred_element_type=F32)

    def bias2(j):
        tbw = bias_ref.shape[-1]
        pieces = [bias_ref[jnp.clip(j - (i * (tq // tbw) + u) * (tbw // tk) + 2, 0, nv - 1)]
                  for u in range(tq // tbw)]
        return jnp.concatenate(pieces + pieces, axis=1)

    def tile_scores(j):
        return raw_scores(j) + bias2(j)

    bm = bm_ref[...]
    s_meta = jnp.dot(km_ref[...], qt, preferred_element_type=F32) + jnp.concatenate([bm, bm], axis=1)
    m0 = jnp.max(s_meta, axis=0, keepdims=True)

    def init_from_meta():
        acc_ref[...] = pv_and_sum(vmt_ref[...], jnp.exp2(s_meta - m0).astype(BF16))

    start = i * (tq // tk) - 1
    n_near = min(tq // tk + 2, nk)
    far_lo = jnp.concatenate([bias_ref[0, 0:1, :]] * (2 * tq // bias_ref.shape[-1]), axis=1)
    far_hi = jnp.concatenate([bias_ref[nv - 1, 0:1, :]] * (2 * tq // bias_ref.shape[-1]), axis=1)
    shift_lo, shift_hi = m0 - far_lo, m0 - far_hi

    def tile_at(pos):
        j = start + pos
        wrapped = j >= nk
        j = jnp.where(wrapped, j - nk, j)
        return jnp.where(j < 0, j + nk, j), wrapped

    def stage_p(pos, slot, near):
        j, wrapped = tile_at(pos)
        if near:
            p = jnp.exp2(tile_scores(j) - m0)
        else:
            p = jnp.exp2(raw_scores(j) - jnp.where(wrapped, shift_lo, shift_hi))
        p_ref[slot] = p.astype(BF16)

    def group(g, first, last):
        pv = None
        for u in range(tb):
            pos = g * tb + u
            if not (last and u == tb - 1):
                stage_p(pos + 1, (u + 1) % 2, first and u + 1 < n_near)
            d = pv_and_sum(vt_ref[tile_at(pos)[0]], p_ref[u % 2])
            pv = d if pv is None else pv + d
        acc_ref[...] += pv

    def trip(g, carry):
        group(g, False, False)
        return carry

    stage_p(0, 0, True)
    init_from_meta()

    n_trips = nk // tb
    group(0, True, n_trips == 1)
    lax.fori_loop(1, n_trips - 1, trip, 0)
    if n_trips > 1:
        group(n_trips - 1, False, True)

    def finish():
        acc = acc_ref[...]
        l = acc[A_V_DIM:A_V_DIM + 1, :]
        inv = 1.0 / l
        ot = acc[:A_V_DIM, :tq] * inv[:, :tq] - lam_ref[0] * (acc[:A_V_DIM, tq:] * inv[:, tq:])
        ssq = jnp.sum(ot * ot, axis=0, keepdims=True)
        yt = ot * lax.rsqrt(ssq * (1.0 / A_V_DIM) + EPS)
        z = z_ref[...].astype(F32)
        o_ref[...] = (yt.T * g_ref[...] * (1.0 - LAM_INIT) * (z * jax.nn.sigmoid(z))).astype(BF16)
        return l, ssq

    l, ssq = finish()
    overflow = (jnp.max(jnp.where(l < OVERFLOW_GUARD, 0.0, 1.0))
                + jnp.max(jnp.where(ssq < F32_FINITE, 0.0, 1.0))) > 0.0

    @pl.when(overflow)
    def _():
        init_from_meta()
        m_ref[...] = m0

        def step(j, carry):
            s = tile_scores(j)
            m_old = m_ref[...]
            m_new = jnp.maximum(m_old, jnp.max(s, axis=0, keepdims=True))
            alpha = jnp.exp2(m_old - m_new)
            acc_ref[...] = alpha * acc_ref[...] + pv_and_sum(vt_ref[j], jnp.exp2(s - m_new).astype(BF16))
            m_ref[...] = m_new
            return carry

        lax.fori_loop(0, nk, step, 0)
        finish()


def _attn_a(lam, qat, ka, vat, ka_m, vat_m, bias, bias_m, za, subln_g, *, tq, tk):
    nb, s, _ = ka.shape
    tqs = qat.shape[-1]
    nk = s // tk
    nv, tbw = bias.shape[1], bias.shape[3]
    tb = next(c for c in (8, 4, 2, 1) if nk % c == 0)
    assert nk % tb == 0 and (nk == tb or tq // tk + 2 <= tb)
    head = lambda n: pl.BlockSpec((None, n, A_V_DIM), lambda b, h, i: (b, i, h))
    return pl.pallas_call(
        functools.partial(_attn_a_kernel, tq=tq, tk=tk, nk=nk, nv=nv, tb=tb),
        grid=(nb, A_HEADS, s // tq),
        in_specs=[
            pl.BlockSpec(memory_space=pltpu.SMEM),
            pl.BlockSpec((None, tq // tqs, A_V_DIM, tqs), lambda b, h, i: (b, i, h, 0)),
            pl.BlockSpec((None, s, A_V_DIM), lambda b, h, i: (b, 0, h)),
            pl.BlockSpec((None, nk, A_V_DIM, tk), lambda b, h, i: (b, 0, h, 0)),
            pl.BlockSpec((None, N_META, A_V_DIM), lambda b, h, i: (0, 0, h)),
            pl.BlockSpec((None, None, A_V_DIM, N_META), lambda b, h, i: (0, 0, h, 0)),
            pl.BlockSpec((None, nv, tk, tbw), lambda b, h, i: (h, 0, 0, 0)),
            pl.BlockSpec((None, N_META, tq), lambda b, h, i: (h, 0, i)),
            head(tq),
            pl.BlockSpec((1, A_V_DIM), lambda b, h, i: (0, 0)),
        ],
        out_specs=head(tq),
        out_shape=jax.ShapeDtypeStruct((nb, s, A_WIDTH), BF16),
        scratch_shapes=[
            pltpu.VMEM((A_V_DIM, 2 * tq), BF16),
            pltpu.VMEM((1, 2 * tq), F32),
            pltpu.VMEM((A_V_DIM + SUM_ROWS, 2 * tq), F32),
            pltpu.VMEM((2, tk, 2 * tq), BF16),
        ],
        compiler_params=pltpu.CompilerParams(
            dimension_semantics=("arbitrary", "arbitrary", "arbitrary"),
            vmem_limit_bytes=VMEM_LIMIT_BYTES),
        name="attn_a",
    )(lam, qat, ka, vat, ka_m, vat_m, bias, bias_m, za, subln_g)


_BT = WINDOW
_BSUB = 8
_BKT = _BT + 2 * WINDOW
_PAD_FRONT = WINDOW - N_META
_B_SLABS = B_HEADS // 2
_B_GROUP = B_HEADS // B_KV_HEADS


def _attn_b_kernel(sink_ref, q_ref, *rest, n_sub_total):
    nkb = _BSUB + 2
    k_refs, v_refs = rest[:nkb], rest[nkb:2 * nkb]
    bias_ref, z_ref, o_ref, qt_ref = rest[2 * nkb:]
    i = pl.program_id(1)
    sink_row = jnp.concatenate(
        [jnp.full((1, _BT), sink_ref[h] * LOG2E, F32) for h in range(B_HEADS)], axis=1)
    zeros = jnp.zeros((B_HEAD_DIM, _BT), BF16)

    def scores(t):
        it = i * _BSUB + t
        k = jnp.concatenate([r[...] for r in k_refs[t:t + 3]], axis=0)
        q = q_ref[t]
        for h in range(B_HEADS):
            blk = q[h * B_HEAD_DIM:(h + 1) * B_HEAD_DIM, :]
            pair = [blk, zeros] if h < _B_GROUP else [zeros, blk]
            qt_ref[t, :, h * _BT:(h + 1) * _BT] = jnp.concatenate(pair, axis=0)
        variant = (it == 0).astype(jnp.int32) + 2 * (it == n_sub_total - 1).astype(jnp.int32)
        return jnp.dot(k, qt_ref[t], preferred_element_type=F32) + bias_ref[variant]

    def finish(t, s, exact):
        v = jnp.concatenate([r[...] for r in v_refs[t:t + 3]], axis=0)
        vt = jnp.concatenate([v.astype(F32).T.astype(BF16), jnp.ones((SUM_ROWS, _BKT), BF16)], axis=0)
        m = jnp.maximum(jnp.max(s, axis=0, keepdims=True), sink_row) if exact else sink_row
        pv = jnp.dot(vt, jnp.exp2(s - m).astype(BF16), preferred_element_type=F32)
        l = pv[2 * B_HEAD_DIM:2 * B_HEAD_DIM + 1, :] + jnp.exp2(sink_row - m)
        ot = pv[:2 * B_HEAD_DIM, :] * (1.0 / l)
        for slab in range(_B_SLABS):
            g = (2 * slab) // _B_GROUP
            rows = slice(g * B_HEAD_DIM, (g + 1) * B_HEAD_DIM)
            x = jnp.concatenate([ot[rows, (2 * slab) * _BT:(2 * slab + 1) * _BT],
                                 ot[rows, (2 * slab + 1) * _BT:(2 * slab + 2) * _BT]], axis=0)
            cols = slice(slab * LANES, (slab + 1) * LANES)
            z = z_ref[t * _BT:(t + 1) * _BT, cols].astype(F32)
            o_ref[t * _BT:(t + 1) * _BT, cols] = (x.T * (z * jax.nn.sigmoid(z))).astype(BF16)
        if exact:
            return None
        return jnp.where(jnp.max(jnp.abs(pv), axis=0, keepdims=True) < OVERFLOW_GUARD, 0.0, 1.0)

    def sweep(exact):
        bad = None
        s_cur = scores(0)
        for t in range(_BSUB):
            s_nxt = scores(t + 1) if t + 1 < _BSUB else None
            b = finish(t, s_cur, exact)
            if not exact:
                bad = b if bad is None else jnp.maximum(bad, b)
            s_cur = s_nxt
        return bad

    overflow = jnp.max(sweep(False)) > 0.0

    @pl.when(overflow)
    def _():
        sweep(True)


def _attn_b(sink, qbt, kb_ext, vb_ext, bias, zb):
    nb, s, _ = zb.shape
    tq = _BT * _BSUB
    row = pl.BlockSpec((None, tq, B_WIDTH), lambda b, i: (b, i, 0))
    nkb = _BSUB + 2
    kv = [pl.BlockSpec((None, WINDOW, B_KV_WIDTH), (lambda b, i, t=t: (b, _BSUB * i + t, 0)))
          for t in range(nkb)]
    return pl.pallas_call(
        functools.partial(_attn_b_kernel, n_sub_total=s // _BT),
        grid=(nb, s // tq),
        in_specs=[pl.BlockSpec(memory_space=pltpu.SMEM),
                  pl.BlockSpec((None, _BSUB, B_WIDTH, _BT), lambda b, i: (b, i, 0, 0))] + kv + kv
                 + [pl.BlockSpec((4, _BKT, B_HEADS * _BT), lambda b, i: (0, 0, 0),
                                 pipeline_mode=pl.Buffered(1)), row],
        out_specs=row,
        out_shape=jax.ShapeDtypeStruct((nb, s, B_WIDTH), BF16),
        scratch_shapes=[pltpu.VMEM((_BSUB, 2 * B_HEAD_DIM, B_HEADS * _BT), BF16)],
        compiler_params=pltpu.CompilerParams(
            dimension_semantics=("arbitrary", "arbitrary"), vmem_limit_bytes=VMEM_LIMIT_BYTES),
        name="attn_b",
    )(sink, qbt, *([kb_ext] * nkb), *([vb_ext] * nkb), bias, zb)


def _out_kernel(x_ref, a_ref, b_ref, ga_ref, gb_ref, woa_ref, wob_ref, wo_ref, g_ref, o_ref):
    ya = jnp.dot(a_ref[...], woa_ref[...], preferred_element_type=F32)
    yb = jnp.dot(b_ref[...], wob_ref[...], preferred_element_type=F32)
    mixed = (jax.nn.sigmoid(ga_ref[...].astype(F32)) * ya
             + jax.nn.sigmoid(gb_ref[...].astype(F32)) * yb)
    y = jnp.dot(mixed.astype(BF16), wo_ref[...], preferred_element_type=F32)
    y = y * lax.rsqrt(jnp.mean(y * y, axis=-1, keepdims=True) + EPS) * g_ref[...]
    o_ref[...] = x_ref[...] + y


def _output(x3, a, b, ga, gb, woa, wob, wo, g, *, tm):
    nb, s, d = x3.shape
    row = lambda n: pl.BlockSpec((None, tm, n), lambda bi, i: (bi, i, 0))
    const = lambda shape: pl.BlockSpec(shape, lambda bi, i: (0, 0), pipeline_mode=pl.Buffered(1))
    return pl.pallas_call(
        _out_kernel,
        grid=(nb, s // tm),
        in_specs=[row(d), row(A_WIDTH), row(B_WIDTH), row(d), row(d),
                  const((A_WIDTH, d)), const((B_WIDTH, d)), const((d, d)), const((1, d))],
        out_specs=row(d),
        out_shape=jax.ShapeDtypeStruct((nb, s, d), F32),
        compiler_params=pltpu.CompilerParams(
            dimension_semantics=("arbitrary", "arbitrary"), vmem_limit_bytes=VMEM_LIMIT_BYTES),
        name="out_proj",
    )(x3, a, b, ga, gb, woa, wob, wo, g)


class _Tiles(NamedTuple):
    tm: int
    tq: int
    tk: int
    tqs: int


def _tiles(s):
    t = _Tiles(tm=min(1024, s), tq=min(1024, s), tk=min(256, s), tqs=min(1024, s))
    assert s % t.tm == 0 and s % t.tq == 0 and t.tq % t.tk == 0 and t.tm % t.tqs == 0 and t.tq % t.tqs == 0
    assert t.tm % _BT == 0 and s % (_BT * _BSUB) == 0
    return t


def kernel(x, meta_tokens, rel_bias, pre_norm_g, w_in, b_in, lambda_q1, lambda_k1, lambda_q2, lambda_k2,
           subln_g, sink, w_out_a, w_out_b, w_out, post_norm_g):
    nb, s, d = x.shape
    tm, tq, tk, tqs = _tiles(s)

    w = w_in[0].astype(BF16)
    b = b_in[0][None, :]
    g_pre = pre_norm_g[0][None, :]
    lam = (jnp.exp(jnp.sum(lambda_q1[0].astype(F32) * lambda_k1[0].astype(F32)))
           - jnp.exp(jnp.sum(lambda_q2[0].astype(F32) * lambda_k2[0].astype(F32))) + LAM_INIT).reshape(1)

    qat, ka, vat, za, qbt, kb, vb, zb, ga, gb = _project(x, g_pre, w, b, tm=tm, tq=tqs, tqb=_BT, tkv=tk)
    meta = _project(meta_tokens[None], g_pre, w, b, tm=N_META, tq=N_META, tqb=N_META, tkv=N_META)
    ka_m, vat_m, kb_m, vb_m = meta[1], meta[2], meta[5], meta[6]

    tbw = tk
    nv = tbw // tk + 4
    off = (jnp.arange(nv, dtype=jnp.int32) - 2)[:, None, None] * tk
    rel_a = off + jnp.arange(tk, dtype=jnp.int32)[None, :, None] - jnp.arange(tbw, dtype=jnp.int32)[None, None, :]
    bias_a = _bias_tiles(rel_bias, _t5_bucket(rel_a).reshape(nv * tk, tbw), head0=0, n_heads=A_HEADS,
                         tr=tk).reshape(A_HEADS, nv, tk, tbw)
    rel_m = jnp.arange(N_META, dtype=jnp.int32)[:, None] - (jnp.arange(s, dtype=jnp.int32)[None, :] + N_META)
    bias_m = _bias_tiles(rel_bias, _t5_bucket(rel_m), head0=0, n_heads=A_HEADS, tr=N_META)
    key_r = jnp.arange(_BKT, dtype=jnp.int32)[:, None]
    rel_b = (key_r - WINDOW) - jnp.arange(_BT, dtype=jnp.int32)[None, :]
    idx_b = jnp.where(jnp.abs(rel_b) <= WINDOW, _t5_bucket(rel_b), N_BUCKETS)
    bias_b = _bias_tiles(rel_bias, idx_b, head0=A_HEADS, n_heads=B_HEADS, tr=_BKT, heads_on_lanes=True)
    bias_b = jnp.stack([
        jnp.where(((key_r >= _PAD_FRONT) | (not first)) & ((key_r < _BT + WINDOW) | (not last)), bias_b, NEG)
        for last in (False, True) for first in (False, True)])

    oa = _attn_a(lam, qat, ka, vat, ka_m, vat_m, bias_a, bias_m, za, subln_g[0][None, :], tq=tq, tk=tk)

    def extend(tok, m):
        return jnp.concatenate([
            jnp.zeros((nb, _PAD_FRONT, B_KV_WIDTH), BF16),
            jnp.broadcast_to(m, (nb, N_META, B_KV_WIDTH)), tok,
            jnp.zeros((nb, WINDOW, B_KV_WIDTH), BF16)], axis=1)

    ob = _attn_b(sink[0], qbt, extend(kb, kb_m), extend(vb, vb_m), bias_b, zb)

    return _output(x, oa, ob, ga, gb, w_out_a[0].astype(BF16), w_out_b[0].astype(BF16),
                   w_out[0].astype(BF16), post_norm_g[0][None, :], tm=tm)
```

```python
import functools
import math
from typing import NamedTuple

import jax
import jax.numpy as jnp
from jax import lax
from jax.experimental import pallas as pl
from jax.experimental.pallas import tpu as pltpu

N_META = 16
WINDOW = 128
A_HEADS = 4
A_QK_DIM = 64
A_V_DIM = 2 * A_QK_DIM
A_WIDTH = A_HEADS * A_V_DIM
B_HEADS = 8
B_KV_HEADS = 2
B_HEAD_DIM = 64
B_WIDTH = B_HEADS * B_HEAD_DIM
B_KV_WIDTH = B_KV_HEADS * B_HEAD_DIM
N_BUCKETS = 32
MAX_DISTANCE = 128
EPS = 1e-6
NEG = -1e30
LOG2E = math.log2(math.e)
LAM_INIT = 0.8 - 0.6 * math.exp(-0.3 * 0)

LANES = 128
OVERFLOW_GUARD = 2.0 ** 100
F32_FINITE = 3.0e38
TRIP_KEYS = 2048
SUM_ROWS = 16
VMEM_LIMIT_BYTES = 56 * 1024 * 1024

_COLS = (A_WIDTH, A_WIDTH, A_WIDTH, A_WIDTH, B_WIDTH, B_KV_WIDTH, B_KV_WIDTH, B_WIDTH)
BF16 = jnp.bfloat16
F32 = jnp.float32


def _t5_bucket(rel):
    half = N_BUCKETS // 2
    max_exact = half // 2
    ret = jnp.where(rel > 0, half, 0)
    n = jnp.abs(rel)
    nf = jnp.maximum(n, 1).astype(F32)
    large = max_exact + (jnp.log(nf / max_exact) / math.log(MAX_DISTANCE / max_exact)
                         * (half - max_exact)).astype(jnp.int32)
    large = jnp.minimum(large, half - 1)
    return ret + jnp.where(n < max_exact, n, large)


def _proj_kernel(x_ref, g_ref, w_ref, b_ref,
                 qat_ref, ka_ref, vat_ref, za_ref, qbt_ref, kb_ref, vb_ref, zb_ref, ga_ref, gb_ref,
                 *, d_model, tq, tqb, tkv):
    x = x_ref[...]
    hn = (x * lax.rsqrt(jnp.mean(x * x, axis=-1, keepdims=True) + EPS) * g_ref[...]).astype(BF16)

    def seg(c0, n):
        return jnp.dot(hn, w_ref[:, c0:c0 + n], preferred_element_type=F32) + b_ref[:, c0:c0 + n]

    qscale = (A_QK_DIM ** -0.5) * LOG2E
    c = 0
    qa = seg(c, A_WIDTH) * qscale; c += A_WIDTH
    for t in range(qat_ref.shape[0]):
        qat_ref[t] = qa[t * tq:(t + 1) * tq, :].T.astype(BF16)
    ka_ref[...] = seg(c, A_WIDTH).astype(BF16); c += A_WIDTH
    va = seg(c, A_WIDTH); c += A_WIDTH
    for t in range(vat_ref.shape[0]):
        vat_ref[t] = va[t * tkv:(t + 1) * tkv, :].T.astype(BF16)
    za_ref[...] = seg(c, A_WIDTH).astype(BF16); c += A_WIDTH
    qb = seg(c, B_WIDTH) * qscale; c += B_WIDTH
    for t in range(qbt_ref.shape[0]):
        qbt_ref[t] = qb[t * tqb:(t + 1) * tqb, :].T.astype(BF16)
    kb_ref[...] = seg(c, B_KV_WIDTH).astype(BF16); c += B_KV_WIDTH
    vb_ref[...] = seg(c, B_KV_WIDTH).astype(BF16); c += B_KV_WIDTH
    zb_ref[...] = seg(c, B_WIDTH).astype(BF16); c += B_WIDTH
    ga_ref[...] = seg(c, d_model).astype(BF16); c += d_model
    gb_ref[...] = seg(c, d_model).astype(BF16)


def _project(x3, g, w, b, *, tm, tq, tqb, tkv):
    nb, s, d = x3.shape
    d_in = w.shape[1]
    widths = _COLS + (d, d)
    nt = s // tm
    row = lambda n: pl.BlockSpec((None, tm, n), lambda bi, i: (bi, i, 0))
    const = lambda shape: pl.BlockSpec(shape, lambda bi, i: (0, 0), pipeline_mode=pl.Buffered(1))
    out_specs, out_shapes = [], []
    for idx, n in enumerate(widths):
        if idx in (0, 2, 4):
            tt = {0: tq, 2: tkv, 4: tqb}[idx]
            out_specs.append(pl.BlockSpec((None, tm // tt, n, tt), lambda bi, i: (bi, i, 0, 0)))
            out_shapes.append(jax.ShapeDtypeStruct((nb, s // tt, n, tt), BF16))
        else:
            out_specs.append(row(n))
            out_shapes.append(jax.ShapeDtypeStruct((nb, s, n), BF16))
    return pl.pallas_call(
        functools.partial(_proj_kernel, d_model=d, tq=tq, tqb=tqb, tkv=tkv),
        grid=(nb, nt),
        in_specs=[row(d), const((1, d)), const((d, d_in)), const((1, d_in))],
        out_specs=out_specs,
        out_shape=out_shapes,
        compiler_params=pltpu.CompilerParams(
            dimension_semantics=("arbitrary", "arbitrary"), vmem_limit_bytes=VMEM_LIMIT_BYTES),
        name="proj",
    )(x3, g, w, b)


def _bias_kernel(tab_ref, idx_ref, o_ref, *, head0):
    h = pl.program_id(0) + head0
    idx = idx_ref[...]
    acc = jnp.full(idx.shape, NEG, F32)
    for bkt in range(N_BUCKETS):
        acc = jnp.where(idx == bkt, tab_ref[bkt, h] * LOG2E, acc)
    o_ref[...] = acc


def _bias_tiles(tab, idx, *, head0, n_heads, tr, heads_on_lanes=False):
    r, c = idx.shape
    if heads_on_lanes:
        out_spec = pl.BlockSpec((tr, c), lambda h, i: (i, h))
        out_shape = jax.ShapeDtypeStruct((r, n_heads * c), F32)
    else:
        out_spec = pl.BlockSpec((None, tr, c), lambda h, i: (h, i, 0))
        out_shape = jax.ShapeDtypeStruct((n_heads, r, c), F32)
    return pl.pallas_call(
        functools.partial(_bias_kernel, head0=head0),
        grid=(n_heads, r // tr),
        in_specs=[pl.BlockSpec(memory_space=pltpu.SMEM),
                  pl.BlockSpec((tr, c), lambda h, i: (i, 0))],
        out_specs=out_spec,
        out_shape=out_shape,
        compiler_params=pltpu.CompilerParams(dimension_semantics=("arbitrary", "arbitrary")),
        name="bias_tiles",
    )(tab, idx)


def _attn_a_kernel(lam_ref, q_ref, k_ref, vt_ref, km_ref, vmt_ref, bias_ref, bm_ref, z_ref, g_ref,
                   o_ref, qt_ref, m_ref, acc_ref, p_ref, *, tq, tk, nk, nv, tb):
    i = pl.program_id(2)
    q = jnp.concatenate([q_ref[u] for u in range(q_ref.shape[0])], axis=1)
    row = lax.broadcasted_iota(jnp.int32, q.shape, 0)
    qt_ref[:, :tq] = jnp.where(row < A_QK_DIM, q, jnp.zeros_like(q))
    qt_ref[:, tq:] = jnp.where(row >= A_QK_DIM, q, jnp.zeros_like(q))
    qt = qt_ref[...]

    def both(fn):
        return jnp.concatenate([fn(slice(0, tq)), fn(slice(tq, 2 * tq))], axis=1)

    def pv_and_sum(vt, pb):
        vt1 = jnp.concatenate([vt, jnp.ones((SUM_ROWS, vt.shape[1]), BF16)], axis=0)
        return both(lambda c: jnp.dot(vt1, pb[:, c], preferred_element_type=F32))

    def raw_scores(j):
        ks = pl.multiple_of(j * tk, tk)
        return jnp.dot(k_ref[pl.ds(ks, tk), :], qt, preferred_element_type=F32)

    def bias2(j):
        tbw = bias_ref.shape[-1]
        pieces = [bias_ref[jnp.clip(j - (i * (tq // tbw) + u) * (tbw // tk) + 2, 0, nv - 1)]
                  for u in range(tq // tbw)]
        return jnp.concatenate(pieces + pieces, axis=1)

    def tile_scores(j):
        return raw_scores(j) + bias2(j)

    bm = bm_ref[...]
    s_meta = jnp.dot(km_ref[...], qt, preferred_element_type=F32) + jnp.concatenate([bm, bm], axis=1)
    m0 = jnp.max(s_meta, axis=0, keepdims=True)

    def init_from_meta():
        acc_ref[...] = pv_and_sum(vmt_ref[...], jnp.exp2(s_meta - m0).astype(BF16))

    start = i * (tq // tk) - 1
    n_near = min(tq // tk + 2, nk)
    far_lo = jnp.concatenate([bias_ref[0, 0:1, :]] * (2 * tq // bias_ref.shape[-1]), axis=1)
    far_hi = jnp.concatenate([bias_ref[nv - 1, 0:1, :]] * (2 * tq // bias_ref.shape[-1]), axis=1)
    shift_lo, shift_hi = m0 - far_lo, m0 - far_hi

    def tile_at(pos):
        j = start + pos
        wrapped = j >= nk
        j = jnp.where(wrapped, j - nk, j)
        return jnp.where(j < 0, j + nk, j), wrapped

    def stage_p(pos, slot, near):
        j, wrapped = tile_at(pos)
        if near:
            p = jnp.exp2(tile_scores(j) - m0)
        else:
            p = jnp.exp2(raw_scores(j) - jnp.where(wrapped, shift_lo, shift_hi))
        p_ref[slot] = p.astype(BF16)

    def group(g, first, last):
        pv = None
        for u in range(tb):
            pos = g * tb + u
            if not (last and u == tb - 1):
                stage_p(pos + 1, (u + 1) % 2, first and u + 1 < n_near)
            d = pv_and_sum(vt_ref[tile_at(pos)[0]], p_ref[u % 2])
            pv = d if pv is None else pv + d
        acc_ref[...] += pv

    def trip(g, carry):
        group(g, False, False)
        return carry

    stage_p(0, 0, True)
    init_from_meta()

    n_trips = nk // tb
    group(0, True, n_trips == 1)
    lax.fori_loop(1, n_trips - 1, trip, 0)
    if n_trips > 1:
        group(n_trips - 1, False, True)

    def finish():
        acc = acc_ref[...]
        l = acc[A_V_DIM:A_V_DIM + 1, :]
        inv = 1.0 / l
        ot = acc[:A_V_DIM, :tq] * inv[:, :tq] - lam_ref[0] * (acc[:A_V_DIM, tq:] * inv[:, tq:])
        ssq = jnp.sum(ot * ot, axis=0, keepdims=True)
        yt = ot * lax.rsqrt(ssq * (1.0 / A_V_DIM) + EPS)
        z = z_ref[...].astype(F32)
        o_ref[...] = (yt.T * g_ref[...] * (1.0 - LAM_INIT) * (z * jax.nn.sigmoid(z))).astype(BF16)
        return l, ssq

    l, ssq = finish()
    overflow = (jnp.max(jnp.where(l < OVERFLOW_GUARD, 0.0, 1.0))
                + jnp.max(jnp.where(ssq < F32_FINITE, 0.0, 1.0))) > 0.0

    @pl.when(overflow)
    def _():
        init_from_meta()
        m_ref[...] = m0

        def step(j, carry):
            s = tile_scores(j)
            m_old = m_ref[...]
            m_new = jnp.maximum(m_old, jnp.max(s, axis=0, keepdims=True))
            alpha = jnp.exp2(m_old - m_new)
            acc_ref[...] = alpha * acc_ref[...] + pv_and_sum(vt_ref[j], jnp.exp2(s - m_new).astype(BF16))
            m_ref[...] = m_new
            return carry

        lax.fori_loop(0, nk, step, 0)
        finish()


def _attn_a(lam, qat, ka, vat, ka_m, vat_m, bias, bias_m, za, subln_g, *, tq, tk):
    nb, s, _ = ka.shape
    tqs = qat.shape[-1]
    nk = s // tk
    nv, tbw = bias.shape[1], bias.shape[3]
    tb = next(c for c in (TRIP_KEYS // tk, 4, 2, 1) if c <= nk and nk % c == 0)
    assert nk % tb == 0 and (nk == tb or tq // tk + 2 <= tb)
    head = lambda n: pl.BlockSpec((None, n, A_V_DIM), lambda b, h, i: (b, i, h))
    return pl.pallas_call(
        functools.partial(_attn_a_kernel, tq=tq, tk=tk, nk=nk, nv=nv, tb=tb),
        grid=(nb, A_HEADS, s // tq),
        in_specs=[
            pl.BlockSpec(memory_space=pltpu.SMEM),
            pl.BlockSpec((None, tq // tqs, A_V_DIM, tqs), lambda b, h, i: (b, i, h, 0)),
            pl.BlockSpec((None, s, A_V_DIM), lambda b, h, i: (b, 0, h)),
            pl.BlockSpec((None, nk, A_V_DIM, tk), lambda b, h, i: (b, 0, h, 0)),
            pl.BlockSpec((None, N_META, A_V_DIM), lambda b, h, i: (0, 0, h)),
            pl.BlockSpec((None, None, A_V_DIM, N_META), lambda b, h, i: (0, 0, h, 0)),
            pl.BlockSpec((None, nv, tk, tbw), lambda b, h, i: (h, 0, 0, 0)),
            pl.BlockSpec((None, N_META, tq), lambda b, h, i: (h, 0, i)),
            head(tq),
            pl.BlockSpec((1, A_V_DIM), lambda b, h, i: (0, 0)),
        ],
        out_specs=head(tq),
        out_shape=jax.ShapeDtypeStruct((nb, s, A_WIDTH), BF16),
        scratch_shapes=[
            pltpu.VMEM((A_V_DIM, 2 * tq), BF16),
            pltpu.VMEM((1, 2 * tq), F32),
            pltpu.VMEM((A_V_DIM + SUM_ROWS, 2 * tq), F32),
            pltpu.VMEM((2, tk, 2 * tq), BF16),
        ],
        compiler_params=pltpu.CompilerParams(
            dimension_semantics=("arbitrary", "arbitrary", "arbitrary"),
            vmem_limit_bytes=VMEM_LIMIT_BYTES),
        name="attn_a",
    )(lam, qat, ka, vat, ka_m, vat_m, bias, bias_m, za, subln_g)


_BT = WINDOW
_BSUB = 8
_BKT = _BT + 2 * WINDOW
_PAD_FRONT = WINDOW - N_META
_B_SLABS = B_HEADS // 2
_B_GROUP = B_HEADS // B_KV_HEADS


def _attn_b_kernel(sink_ref, q_ref, *rest, n_sub_total):
    nkb = _BSUB + 2
    k_refs, v_refs = rest[:nkb], rest[nkb:2 * nkb]
    bias_ref, z_ref, o_ref, qt_ref = rest[2 * nkb:]
    i = pl.program_id(1)
    sink_row = jnp.concatenate(
        [jnp.full((1, _BT), sink_ref[h] * LOG2E, F32) for h in range(B_HEADS)], axis=1)
    zeros = jnp.zeros((B_HEAD_DIM, _BT), BF16)

    def scores(t):
        it = i * _BSUB + t
        k = jnp.concatenate([r[...] for r in k_refs[t:t + 3]], axis=0)
        q = q_ref[t]
        for h in range(B_HEADS):
            blk = q[h * B_HEAD_DIM:(h + 1) * B_HEAD_DIM, :]
            pair = [blk, zeros] if h < _B_GROUP else [zeros, blk]
            qt_ref[t, :, h * _BT:(h + 1) * _BT] = jnp.concatenate(pair, axis=0)
        variant = (it == 0).astype(jnp.int32) + 2 * (it == n_sub_total - 1).astype(jnp.int32)
        return jnp.dot(k, qt_ref[t], preferred_element_type=F32) + bias_ref[variant]

    def finish(t, s, exact):
        v = jnp.concatenate([r[...] for r in v_refs[t:t + 3]], axis=0)
        vt = jnp.concatenate([v.astype(F32).T.astype(BF16), jnp.ones((SUM_ROWS, _BKT), BF16)], axis=0)
        m = jnp.maximum(jnp.max(s, axis=0, keepdims=True), sink_row) if exact else sink_row
        pv = jnp.dot(vt, jnp.exp2(s - m).astype(BF16), preferred_element_type=F32)
        l = pv[2 * B_HEAD_DIM:2 * B_HEAD_DIM + 1, :] + jnp.exp2(sink_row - m)
        ot = pv[:2 * B_HEAD_DIM, :] * (1.0 / l)
        for slab in range(_B_SLABS):
            g = (2 * slab) // _B_GROUP
            rows = slice(g * B_HEAD_DIM, (g + 1) * B_HEAD_DIM)
            x = jnp.concatenate([ot[rows, (2 * slab) * _BT:(2 * slab + 1) * _BT],
                                 ot[rows, (2 * slab + 1) * _BT:(2 * slab + 2) * _BT]], axis=0)
            cols = slice(slab * LANES, (slab + 1) * LANES)
            z = z_ref[t * _BT:(t + 1) * _BT, cols].astype(F32)
            o_ref[t * _BT:(t + 1) * _BT, cols] = (x.T * (z * jax.nn.sigmoid(z))).astype(BF16)
        if exact:
            return None
        return jnp.where(jnp.max(jnp.abs(pv), axis=0, keepdims=True) < OVERFLOW_GUARD, 0.0, 1.0)

    def sweep(exact):
        bad = None
        s_cur = scores(0)
        for t in range(_BSUB):
            s_nxt = scores(t + 1) if t + 1 < _BSUB else None
            b = finish(t, s_cur, exact)
            if not exact:
                bad = b if bad is None else jnp.maximum(bad, b)
            s_cur = s_nxt
        return bad

    overflow = jnp.max(sweep(False)) > 0.0

    @pl.when(overflow)
    def _():
        sweep(True)


def _attn_b(sink, qbt, kb_ext, vb_ext, bias, zb):
    nb, s, _ = zb.shape
    tq = _BT * _BSUB
    row = pl.BlockSpec((None, tq, B_WIDTH), lambda b, i: (b, i, 0))
    nkb = _BSUB + 2
    kv = [pl.BlockSpec((None, WINDOW, B_KV_WIDTH), (lambda b, i, t=t: (b, _BSUB * i + t, 0)))
          for t in range(nkb)]
    return pl.pallas_call(
        functools.partial(_attn_b_kernel, n_sub_total=s // _BT),
        grid=(nb, s // tq),
        in_specs=[pl.BlockSpec(memory_space=pltpu.SMEM),
                  pl.BlockSpec((None, _BSUB, B_WIDTH, _BT), lambda b, i: (b, i, 0, 0))] + kv + kv
                 + [pl.BlockSpec((4, _BKT, B_HEADS * _BT), lambda b, i: (0, 0, 0),
                                 pipeline_mode=pl.Buffered(1)), row],
        out_specs=row,
        out_shape=jax.ShapeDtypeStruct((nb, s, B_WIDTH), BF16),
        scratch_shapes=[pltpu.VMEM((_BSUB, 2 * B_HEAD_DIM, B_HEADS * _BT), BF16)],
        compiler_params=pltpu.CompilerParams(
            dimension_semantics=("arbitrary", "arbitrary"), vmem_limit_bytes=VMEM_LIMIT_BYTES),
        name="attn_b",
    )(sink, qbt, *([kb_ext] * nkb), *([vb_ext] * nkb), bias, zb)


def _out_kernel(x_ref, a_ref, b_ref, ga_ref, gb_ref, woa_ref, wob_ref, wo_ref, g_ref, o_ref):
    ya = jnp.dot(a_ref[...], woa_ref[...], preferred_element_type=F32)
    yb = jnp.dot(b_ref[...], wob_ref[...], preferred_element_type=F32)
    mixed = (jax.nn.sigmoid(ga_ref[...].astype(F32)) * ya
             + jax.nn.sigmoid(gb_ref[...].astype(F32)) * yb)
    y = jnp.dot(mixed.astype(BF16), wo_ref[...], preferred_element_type=F32)
    y = y * lax.rsqrt(jnp.mean(y * y, axis=-1, keepdims=True) + EPS) * g_ref[...]
    o_ref[...] = x_ref[...] + y


def _output(x3, a, b, ga, gb, woa, wob, wo, g, *, tm):
    nb, s, d = x3.shape
    row = lambda n: pl.BlockSpec((None, tm, n), lambda bi, i: (bi, i, 0))
    const = lambda shape: pl.BlockSpec(shape, lambda bi, i: (0, 0), pipeline_mode=pl.Buffered(1))
    return pl.pallas_call(
        _out_kernel,
        grid=(nb, s // tm),
        in_specs=[row(d), row(A_WIDTH), row(B_WIDTH), row(d), row(d),
                  const((A_WIDTH, d)), const((B_WIDTH, d)), const((d, d)), const((1, d))],
        out_specs=row(d),
        out_shape=jax.ShapeDtypeStruct((nb, s, d), F32),
        compiler_params=pltpu.CompilerParams(
            dimension_semantics=("arbitrary", "arbitrary"), vmem_limit_bytes=VMEM_LIMIT_BYTES),
        name="out_proj",
    )(x3, a, b, ga, gb, woa, wob, wo, g)


class _Tiles(NamedTuple):
    tm: int
    tq: int
    tk: int
    tqs: int


def _tiles(s):
    t = _Tiles(tm=min(1024, s), tq=min(1024, s), tk=min(512, s), tqs=min(1024, s))
    assert s % t.tm == 0 and s % t.tq == 0 and t.tq % t.tk == 0 and t.tm % t.tqs == 0 and t.tq % t.tqs == 0
    assert t.tm % _BT == 0 and s % (_BT * _BSUB) == 0
    return t


def kernel(x, meta_tokens, rel_bias, pre_norm_g, w_in, b_in, lambda_q1, lambda_k1, lambda_q2, lambda_k2,
           subln_g, sink, w_out_a, w_out_b, w_out, post_norm_g):
    nb, s, d = x.shape
    tm, tq, tk, tqs = _tiles(s)

    w = w_in[0].astype(BF16)
    b = b_in[0][None, :]
    g_pre = pre_norm_g[0][None, :]
    lam = (jnp.exp(jnp.sum(lambda_q1[0].astype(F32) * lambda_k1[0].astype(F32)))
           - jnp.exp(jnp.sum(lambda_q2[0].astype(F32) * lambda_k2[0].astype(F32))) + LAM_INIT).reshape(1)

    qat, ka, vat, za, qbt, kb, vb, zb, ga, gb = _project(x, g_pre, w, b, tm=tm, tq=tqs, tqb=_BT, tkv=tk)
    meta = _project(meta_tokens[None], g_pre, w, b, tm=N_META, tq=N_META, tqb=N_META, tkv=N_META)
    ka_m, vat_m, kb_m, vb_m = meta[1], meta[2], meta[5], meta[6]

    tbw = tk
    nv = tbw // tk + 4
    off = (jnp.arange(nv, dtype=jnp.int32) - 2)[:, None, None] * tk
    rel_a = off + jnp.arange(tk, dtype=jnp.int32)[None, :, None] - jnp.arange(tbw, dtype=jnp.int32)[None, None, :]
    bias_a = _bias_tiles(rel_bias, _t5_bucket(rel_a).reshape(nv * tk, tbw), head0=0, n_heads=A_HEADS,
                         tr=tk).reshape(A_HEADS, nv, tk, tbw)
    rel_m = jnp.arange(N_META, dtype=jnp.int32)[:, None] - (jnp.arange(s, dtype=jnp.int32)[None, :] + N_META)
    bias_m = _bias_tiles(rel_bias, _t5_bucket(rel_m), head0=0, n_heads=A_HEADS, tr=N_META)
    key_r = jnp.arange(_BKT, dtype=jnp.int32)[:, None]
    rel_b = (key_r - WINDOW) - jnp.arange(_BT, dtype=jnp.int32)[None, :]
    idx_b = jnp.where(jnp.abs(rel_b) <= WINDOW, _t5_bucket(rel_b), N_BUCKETS)
    bias_b = _bias_tiles(rel_bias, idx_b, head0=A_HEADS, n_heads=B_HEADS, tr=_BKT, heads_on_lanes=True)
    bias_b = jnp.stack([
        jnp.where(((key_r >= _PAD_FRONT) | (not first)) & ((key_r < _BT + WINDOW) | (not last)), bias_b, NEG)
        for last in (False, True) for first in (False, True)])

    oa = _attn_a(lam, qat, ka, vat, ka_m, vat_m, bias_a, bias_m, za, subln_g[0][None, :], tq=tq, tk=tk)

    def extend(tok, m):
        return jnp.concatenate([
            jnp.zeros((nb, _PAD_FRONT, B_KV_WIDTH), BF16),
            jnp.broadcast_to(m, (nb, N_META, B_KV_WIDTH)), tok,
            jnp.zeros((nb, WINDOW, B_KV_WIDTH), BF16)], axis=1)

    ob = _attn_b(sink[0], qbt, extend(kb, kb_m), extend(vb, vb_m), bias_b, zb)

    return _output(x, oa, ob, ga, gb, w_out_a[0].astype(BF16), w_out_b[0].astype(BF16),
                   w_out[0].astype(BF16), post_norm_g[0][None, :], tm=tm)
```

```python
import functools
import math
from typing import NamedTuple

import jax
import jax.numpy as jnp
from jax import lax
from jax.experimental import pallas as pl
from jax.experimental.pallas import tpu as pltpu

N_META = 16
WINDOW = 128
A_HEADS = 4
A_QK_DIM = 64
A_V_DIM = 2 * A_QK_DIM
A_WIDTH = A_HEADS * A_V_DIM
B_HEADS = 8
B_KV_HEADS = 2
B_HEAD_DIM = 64
B_WIDTH = B_HEADS * B_HEAD_DIM
B_KV_WIDTH = B_KV_HEADS * B_HEAD_DIM
N_BUCKETS = 32
MAX_DISTANCE = 128
EPS = 1e-6
NEG = -1e30
LOG2E = math.log2(math.e)
LAM_INIT = 0.8 - 0.6 * math.exp(-0.3 * 0)

LANES = 128
OVERFLOW_GUARD = 2.0 ** 100
F32_FINITE = 3.0e38
TRIP_KEYS = 2048
SUM_ROWS = 16
VMEM_LIMIT_BYTES = 56 * 1024 * 1024

_COLS = (A_WIDTH, A_WIDTH, A_WIDTH, A_WIDTH, B_WIDTH, B_KV_WIDTH, B_KV_WIDTH, B_WIDTH)
BF16 = jnp.bfloat16
F32 = jnp.float32


def _t5_bucket(rel):
    half = N_BUCKETS // 2
    max_exact = half // 2
    ret = jnp.where(rel > 0, half, 0)
    n = jnp.abs(rel)
    nf = jnp.maximum(n, 1).astype(F32)
    large = max_exact + (jnp.log(nf / max_exact) / math.log(MAX_DISTANCE / max_exact)
                         * (half - max_exact)).astype(jnp.int32)
    large = jnp.minimum(large, half - 1)
    return ret + jnp.where(n < max_exact, n, large)


def _proj_kernel(x_ref, g_ref, w_ref, b_ref,
                 qat_ref, ka_ref, vat_ref, za_ref, qbt_ref, kb_ref, vb_ref, zb_ref, ga_ref, gb_ref,
                 *, d_model, tq, tqb, tkv):
    x = x_ref[...]
    hn = (x * lax.rsqrt(jnp.mean(x * x, axis=-1, keepdims=True) + EPS) * g_ref[...]).astype(BF16)

    def seg(c0, n):
        return jnp.dot(hn, w_ref[:, c0:c0 + n], preferred_element_type=F32) + b_ref[:, c0:c0 + n]

    qscale = (A_QK_DIM ** -0.5) * LOG2E
    c = 0
    qa = seg(c, A_WIDTH) * qscale; c += A_WIDTH
    for t in range(qat_ref.shape[0]):
        qat_ref[t] = qa[t * tq:(t + 1) * tq, :].T.astype(BF16)
    ka_ref[...] = seg(c, A_WIDTH).astype(BF16); c += A_WIDTH
    va = seg(c, A_WIDTH); c += A_WIDTH
    for t in range(vat_ref.shape[0]):
        vat_ref[t] = va[t * tkv:(t + 1) * tkv, :].T.astype(BF16)
    za_ref[...] = seg(c, A_WIDTH).astype(BF16); c += A_WIDTH
    qb = seg(c, B_WIDTH) * qscale; c += B_WIDTH
    for t in range(qbt_ref.shape[0]):
        qbt_ref[t] = qb[t * tqb:(t + 1) * tqb, :].T.astype(BF16)
    kb_ref[...] = seg(c, B_KV_WIDTH).astype(BF16); c += B_KV_WIDTH
    vb_ref[...] = seg(c, B_KV_WIDTH).astype(BF16); c += B_KV_WIDTH
    zb_ref[...] = seg(c, B_WIDTH).astype(BF16); c += B_WIDTH
    ga_ref[...] = seg(c, d_model).astype(BF16); c += d_model
    gb_ref[...] = seg(c, d_model).astype(BF16)


def _project(x3, g, w, b, *, tm, tq, tqb, tkv):
    nb, s, d = x3.shape
    d_in = w.shape[1]
    widths = _COLS + (d, d)
    nt = s // tm
    row = lambda n: pl.BlockSpec((None, tm, n), lambda bi, i: (bi, i, 0))
    const = lambda shape: pl.BlockSpec(shape, lambda bi, i: (0, 0), pipeline_mode=pl.Buffered(1))
    out_specs, out_shapes = [], []
    for idx, n in enumerate(widths):
        if idx in (0, 2, 4):
            tt = {0: tq, 2: tkv, 4: tqb}[idx]
            out_specs.append(pl.BlockSpec((None, tm // tt, n, tt), lambda bi, i: (bi, i, 0, 0)))
            out_shapes.append(jax.ShapeDtypeStruct((nb, s // tt, n, tt), BF16))
        else:
            out_specs.append(row(n))
            out_shapes.append(jax.ShapeDtypeStruct((nb, s, n), BF16))
    return pl.pallas_call(
        functools.partial(_proj_kernel, d_model=d, tq=tq, tqb=tqb, tkv=tkv),
        grid=(nb, nt),
        in_specs=[row(d), const((1, d)), const((d, d_in)), const((1, d_in))],
        out_specs=out_specs,
        out_shape=out_shapes,
        compiler_params=pltpu.CompilerParams(
            dimension_semantics=("arbitrary", "arbitrary"), vmem_limit_bytes=VMEM_LIMIT_BYTES),
        name="proj",
    )(x3, g, w, b)


def _bias_kernel(tab_ref, idx_ref, o_ref, *, head0):
    h = pl.program_id(0) + head0
    idx = idx_ref[...]
    level = [tab_ref[bkt, h] * LOG2E for bkt in range(N_BUCKETS)]
    bit = 1
    while len(level) > 1:
        on = (idx & bit) != 0
        level = [jnp.where(on, level[k + 1], level[k]) for k in range(0, len(level), 2)]
        bit *= 2
    o_ref[...] = jnp.where(idx >= N_BUCKETS, NEG, level[0])


def _bias_tiles(tab, idx, *, head0, n_heads, tr, heads_on_lanes=False):
    r, c = idx.shape
    if heads_on_lanes:
        out_spec = pl.BlockSpec((tr, c), lambda h, i: (i, h))
        out_shape = jax.ShapeDtypeStruct((r, n_heads * c), F32)
    else:
        out_spec = pl.BlockSpec((None, tr, c), lambda h, i: (h, i, 0))
        out_shape = jax.ShapeDtypeStruct((n_heads, r, c), F32)
    return pl.pallas_call(
        functools.partial(_bias_kernel, head0=head0),
        grid=(n_heads, r // tr),
        in_specs=[pl.BlockSpec(memory_space=pltpu.SMEM),
                  pl.BlockSpec((tr, c), lambda h, i: (i, 0))],
        out_specs=out_spec,
        out_shape=out_shape,
        compiler_params=pltpu.CompilerParams(dimension_semantics=("arbitrary", "arbitrary")),
        name="bias_tiles",
    )(tab, idx)


def _attn_a_kernel(lam_ref, q_ref, k_ref, vt_ref, km_ref, vmt_ref, bias_ref, bm_ref, z_ref, g_ref,
                   o_ref, qt_ref, m_ref, acc_ref, p_ref, *, tq, tk, nk, nv, tb):
    i = pl.program_id(2)
    q = jnp.concatenate([q_ref[u] for u in range(q_ref.shape[0])], axis=1)
    row = lax.broadcasted_iota(jnp.int32, q.shape, 0)
    qt_ref[:, :tq] = jnp.where(row < A_QK_DIM, q, jnp.zeros_like(q))
    qt_ref[:, tq:] = jnp.where(row >= A_QK_DIM, q, jnp.zeros_like(q))
    qt = qt_ref[...]

    def both(fn):
        return jnp.concatenate([fn(slice(0, tq)), fn(slice(tq, 2 * tq))], axis=1)

    def pv_and_sum(vt, pb):
        vt1 = jnp.concatenate([vt, jnp.ones((SUM_ROWS, vt.shape[1]), BF16)], axis=0)
        return both(lambda c: jnp.dot(vt1, pb[:, c], preferred_element_type=F32))

    def raw_scores(j):
        ks = pl.multiple_of(j * tk, tk)
        return jnp.dot(k_ref[pl.ds(ks, tk), :], qt, preferred_element_type=F32)

    def bias2(j):
        tbw = bias_ref.shape[-1]
        pieces = [bias_ref[jnp.clip(j - (i * (tq // tbw) + u) * (tbw // tk) + 2, 0, nv - 1)]
                  for u in range(tq // tbw)]
        return jnp.concatenate(pieces + pieces, axis=1)

    def tile_scores(j):
        return raw_scores(j) + bias2(j)

    bm = bm_ref[...]
    s_meta = jnp.dot(km_ref[...], qt, preferred_element_type=F32) + jnp.concatenate([bm, bm], axis=1)
    m0 = jnp.max(s_meta, axis=0, keepdims=True)

    def init_from_meta():
        acc_ref[...] = pv_and_sum(vmt_ref[...], jnp.exp2(s_meta - m0).astype(BF16))

    start = i * (tq // tk) - 1
    n_near = min(tq // tk + 2, nk)
    far_lo = jnp.concatenate([bias_ref[0, 0:1, :]] * (2 * tq // bias_ref.shape[-1]), axis=1)
    far_hi = jnp.concatenate([bias_ref[nv - 1, 0:1, :]] * (2 * tq // bias_ref.shape[-1]), axis=1)
    shift_lo, shift_hi = m0 - far_lo, m0 - far_hi

    def tile_at(pos):
        j = start + pos
        wrapped = j >= nk
        j = jnp.where(wrapped, j - nk, j)
        return jnp.where(j < 0, j + nk, j), wrapped

    def stage_p(pos, slot, near):
        j, wrapped = tile_at(pos)
        if near:
            p = jnp.exp2(tile_scores(j) - m0)
        else:
            p = jnp.exp2(raw_scores(j) - jnp.where(wrapped, shift_lo, shift_hi))
        p_ref[slot] = p.astype(BF16)

    def group(g, first, last):
        pv = None
        for u in range(tb):
            pos = g * tb + u
            if not (last and u == tb - 1):
                stage_p(pos + 1, (u + 1) % 2, first and u + 1 < n_near)
            d = pv_and_sum(vt_ref[tile_at(pos)[0]], p_ref[u % 2])
            pv = d if pv is None else pv + d
        acc_ref[...] += pv

    def trip(g, carry):
        group(g, False, False)
        return carry

    stage_p(0, 0, True)
    init_from_meta()

    n_trips = nk // tb
    group(0, True, n_trips == 1)
    lax.fori_loop(1, n_trips - 1, trip, 0)
    if n_trips > 1:
        group(n_trips - 1, False, True)

    def finish():
        acc = acc_ref[...]
        l = acc[A_V_DIM:A_V_DIM + 1, :]
        inv = 1.0 / l
        ot = acc[:A_V_DIM, :tq] * inv[:, :tq] - lam_ref[0] * (acc[:A_V_DIM, tq:] * inv[:, tq:])
        ssq = jnp.sum(ot * ot, axis=0, keepdims=True)
        yt = ot * lax.rsqrt(ssq * (1.0 / A_V_DIM) + EPS)
        z = z_ref[...].astype(F32)
        o_ref[...] = (yt.T * g_ref[...] * (1.0 - LAM_INIT) * (z * jax.nn.sigmoid(z))).astype(BF16)
        return l, ssq

    l, ssq = finish()
    overflow = (jnp.max(jnp.where(l < OVERFLOW_GUARD, 0.0, 1.0))
                + jnp.max(jnp.where(ssq < F32_FINITE, 0.0, 1.0))) > 0.0

    @pl.when(overflow)
    def _():
        init_from_meta()
        m_ref[...] = m0

        def step(j, carry):
            s = tile_scores(j)
            m_old = m_ref[...]
            m_new = jnp.maximum(m_old, jnp.max(s, axis=0, keepdims=True))
            alpha = jnp.exp2(m_old - m_new)
            acc_ref[...] = alpha * acc_ref[...] + pv_and_sum(vt_ref[j], jnp.exp2(s - m_new).astype(BF16))
            m_ref[...] = m_new
            return carry

        lax.fori_loop(0, nk, step, 0)
        finish()


def _attn_a(lam, qat, ka, vat, ka_m, vat_m, bias, bias_m, za, subln_g, *, tq, tk):
    nb, s, _ = ka.shape
    tqs = qat.shape[-1]
    nk = s // tk
    nv, tbw = bias.shape[1], bias.shape[3]
    tb = next(c for c in (TRIP_KEYS // tk, 4, 2, 1) if c <= nk and nk % c == 0)
    assert nk % tb == 0 and (nk == tb or tq // tk + 2 <= tb)
    head = lambda n: pl.BlockSpec((None, n, A_V_DIM), lambda b, h, i: (b, i, h))
    return pl.pallas_call(
        functools.partial(_attn_a_kernel, tq=tq, tk=tk, nk=nk, nv=nv, tb=tb),
        grid=(nb, A_HEADS, s // tq),
        in_specs=[
            pl.BlockSpec(memory_space=pltpu.SMEM),
            pl.BlockSpec((None, tq // tqs, A_V_DIM, tqs), lambda b, h, i: (b, i, h, 0)),
            pl.BlockSpec((None, s, A_V_DIM), lambda b, h, i: (b, 0, h)),
            pl.BlockSpec((None, nk, A_V_DIM, tk), lambda b, h, i: (b, 0, h, 0)),
            pl.BlockSpec((None, N_META, A_V_DIM), lambda b, h, i: (0, 0, h)),
            pl.BlockSpec((None, None, A_V_DIM, N_META), lambda b, h, i: (0, 0, h, 0)),
            pl.BlockSpec((None, nv, tk, tbw), lambda b, h, i: (h, 0, 0, 0)),
            pl.BlockSpec((None, N_META, tq), lambda b, h, i: (h, 0, i)),
            head(tq),
            pl.BlockSpec((1, A_V_DIM), lambda b, h, i: (0, 0)),
        ],
        out_specs=head(tq),
        out_shape=jax.ShapeDtypeStruct((nb, s, A_WIDTH), BF16),
        scratch_shapes=[
            pltpu.VMEM((A_V_DIM, 2 * tq), BF16),
            pltpu.VMEM((1, 2 * tq), F32),
            pltpu.VMEM((A_V_DIM + SUM_ROWS, 2 * tq), F32),
            pltpu.VMEM((2, tk, 2 * tq), BF16),
        ],
        compiler_params=pltpu.CompilerParams(
            dimension_semantics=("arbitrary", "arbitrary", "arbitrary"),
            vmem_limit_bytes=VMEM_LIMIT_BYTES),
        name="attn_a",
    )(lam, qat, ka, vat, ka_m, vat_m, bias, bias_m, za, subln_g)


_BT = WINDOW
_BSUB = 8
_BKT = _BT + 2 * WINDOW
_PAD_FRONT = WINDOW - N_META
_B_SLABS = B_HEADS // 2
_B_GROUP = B_HEADS // B_KV_HEADS


def _attn_b_kernel(sink_ref, q_ref, *rest, n_sub_total):
    nkb = _BSUB + 2
    k_refs, v_refs = rest[:nkb], rest[nkb:2 * nkb]
    bias_ref, z_ref, o_ref, qt_ref = rest[2 * nkb:]
    i = pl.program_id(1)
    sink_row = jnp.concatenate(
        [jnp.full((1, _BT), sink_ref[h] * LOG2E, F32) for h in range(B_HEADS)], axis=1)
    zeros = jnp.zeros((B_HEAD_DIM, _BT), BF16)

    def scores(t):
        it = i * _BSUB + t
        k = jnp.concatenate([r[...] for r in k_refs[t:t + 3]], axis=0)
        q = q_ref[t]
        for h in range(B_HEADS):
            blk = q[h * B_HEAD_DIM:(h + 1) * B_HEAD_DIM, :]
            pair = [blk, zeros] if h < _B_GROUP else [zeros, blk]
            qt_ref[t, :, h * _BT:(h + 1) * _BT] = jnp.concatenate(pair, axis=0)
        variant = (it == 0).astype(jnp.int32) + 2 * (it == n_sub_total - 1).astype(jnp.int32)
        return jnp.dot(k, qt_ref[t], preferred_element_type=F32) + bias_ref[variant]

    def finish(t, s, exact):
        v = jnp.concatenate([r[...] for r in v_refs[t:t + 3]], axis=0)
        vt = jnp.concatenate([v.astype(F32).T.astype(BF16), jnp.ones((SUM_ROWS, _BKT), BF16)], axis=0)
        m = jnp.maximum(jnp.max(s, axis=0, keepdims=True), sink_row) if exact else sink_row
        pv = jnp.dot(vt, jnp.exp2(s - m).astype(BF16), preferred_element_type=F32)
        l = pv[2 * B_HEAD_DIM:2 * B_HEAD_DIM + 1, :] + jnp.exp2(sink_row - m)
        ot = pv[:2 * B_HEAD_DIM, :] * (1.0 / l)
        for slab in range(_B_SLABS):
            g = (2 * slab) // _B_GROUP
            rows = slice(g * B_HEAD_DIM, (g + 1) * B_HEAD_DIM)
            x = jnp.concatenate([ot[rows, (2 * slab) * _BT:(2 * slab + 1) * _BT],
                                 ot[rows, (2 * slab + 1) * _BT:(2 * slab + 2) * _BT]], axis=0)
            cols = slice(slab * LANES, (slab + 1) * LANES)
            z = z_ref[t * _BT:(t + 1) * _BT, cols].astype(F32)
            o_ref[t * _BT:(t + 1) * _BT, cols] = (x.T * (z * jax.nn.sigmoid(z))).astype(BF16)
        if exact:
            return None
        return jnp.where(jnp.max(jnp.abs(pv), axis=0, keepdims=True) < OVERFLOW_GUARD, 0.0, 1.0)

    def sweep(exact):
        bad = None
        s_cur = scores(0)
        for t in range(_BSUB):
            s_nxt = scores(t + 1) if t + 1 < _BSUB else None
            b = finish(t, s_cur, exact)
            if not exact:
                bad = b if bad is None else jnp.maximum(bad, b)
            s_cur = s_nxt
        return bad

    overflow = jnp.max(sweep(False)) > 0.0

    @pl.when(overflow)
    def _():
        sweep(True)


def _attn_b(sink, qbt, kb_ext, vb_ext, bias, zb):
    nb, s, _ = zb.shape
    tq = _BT * _BSUB
    row = pl.BlockSpec((None, tq, B_WIDTH), lambda b, i: (b, i, 0))
    nkb = _BSUB + 2
    kv = [pl.BlockSpec((None, WINDOW, B_KV_WIDTH), (lambda b, i, t=t: (b, _BSUB * i + t, 0)))
          for t in range(nkb)]
    return pl.pallas_call(
        functools.partial(_attn_b_kernel, n_sub_total=s // _BT),
        grid=(nb, s // tq),
        in_specs=[pl.BlockSpec(memory_space=pltpu.SMEM),
                  pl.BlockSpec((None, _BSUB, B_WIDTH, _BT), lambda b, i: (b, i, 0, 0))] + kv + kv
                 + [pl.BlockSpec((4, _BKT, B_HEADS * _BT), lambda b, i: (0, 0, 0),
                                 pipeline_mode=pl.Buffered(1)), row],
        out_specs=row,
        out_shape=jax.ShapeDtypeStruct((nb, s, B_WIDTH), BF16),
        scratch_shapes=[pltpu.VMEM((_BSUB, 2 * B_HEAD_DIM, B_HEADS * _BT), BF16)],
        compiler_params=pltpu.CompilerParams(
            dimension_semantics=("arbitrary", "arbitrary"), vmem_limit_bytes=VMEM_LIMIT_BYTES),
        name="attn_b",
    )(sink, qbt, *([kb_ext] * nkb), *([vb_ext] * nkb), bias, zb)


def _out_kernel(x_ref, a_ref, b_ref, ga_ref, gb_ref, woa_ref, wob_ref, wo_ref, g_ref, o_ref):
    ya = jnp.dot(a_ref[...], woa_ref[...], preferred_element_type=F32)
    yb = jnp.dot(b_ref[...], wob_ref[...], preferred_element_type=F32)
    mixed = (jax.nn.sigmoid(ga_ref[...].astype(F32)) * ya
             + jax.nn.sigmoid(gb_ref[...].astype(F32)) * yb)
    y = jnp.dot(mixed.astype(BF16), wo_ref[...], preferred_element_type=F32)
    y = y * lax.rsqrt(jnp.mean(y * y, axis=-1, keepdims=True) + EPS) * g_ref[...]
    o_ref[...] = x_ref[...] + y


def _output(x3, a, b, ga, gb, woa, wob, wo, g, *, tm):
    nb, s, d = x3.shape
    row = lambda n: pl.BlockSpec((None, tm, n), lambda bi, i: (bi, i, 0))
    const = lambda shape: pl.BlockSpec(shape, lambda bi, i: (0, 0), pipeline_mode=pl.Buffered(1))
    return pl.pallas_call(
        _out_kernel,
        grid=(nb, s // tm),
        in_specs=[row(d), row(A_WIDTH), row(B_WIDTH), row(d), row(d),
                  const((A_WIDTH, d)), const((B_WIDTH, d)), const((d, d)), const((1, d))],
        out_specs=row(d),
        out_shape=jax.ShapeDtypeStruct((nb, s, d), F32),
        compiler_params=pltpu.CompilerParams(
            dimension_semantics=("arbitrary", "arbitrary"), vmem_limit_bytes=VMEM_LIMIT_BYTES),
        name="out_proj",
    )(x3, a, b, ga, gb, woa, wob, wo, g)


class _Tiles(NamedTuple):
    tm: int
    tq: int
    tk: int
    tqs: int


def _tiles(s):
    t = _Tiles(tm=min(1024, s), tq=min(1024, s), tk=min(512, s), tqs=min(1024, s))
    assert s % t.tm == 0 and s % t.tq == 0 and t.tq % t.tk == 0 and t.tm % t.tqs == 0 and t.tq % t.tqs == 0
    assert t.tm % _BT == 0 and s % (_BT * _BSUB) == 0
    return t


def kernel(x, meta_tokens, rel_bias, pre_norm_g, w_in, b_in, lambda_q1, lambda_k1, lambda_q2, lambda_k2,
           subln_g, sink, w_out_a, w_out_b, w_out, post_norm_g):
    nb, s, d = x.shape
    tm, tq, tk, tqs = _tiles(s)

    w = w_in[0].astype(BF16)
    b = b_in[0][None, :]
    g_pre = pre_norm_g[0][None, :]
    lam = (jnp.exp(jnp.sum(lambda_q1[0].astype(F32) * lambda_k1[0].astype(F32)))
           - jnp.exp(jnp.sum(lambda_q2[0].astype(F32) * lambda_k2[0].astype(F32))) + LAM_INIT).reshape(1)

    qat, ka, vat, za, qbt, kb, vb, zb, ga, gb = _project(x, g_pre, w, b, tm=tm, tq=tqs, tqb=_BT, tkv=tk)
    meta = _project(meta_tokens[None], g_pre, w, b, tm=N_META, tq=N_META, tqb=N_META, tkv=N_META)
    ka_m, vat_m, kb_m, vb_m = meta[1], meta[2], meta[5], meta[6]

    tbw = tk
    nv = tbw // tk + 4
    off = (jnp.arange(nv, dtype=jnp.int32) - 2)[:, None, None] * tk
    rel_a = off + jnp.arange(tk, dtype=jnp.int32)[None, :, None] - jnp.arange(tbw, dtype=jnp.int32)[None, None, :]
    bias_a = _bias_tiles(rel_bias, _t5_bucket(rel_a).reshape(nv * tk, tbw), head0=0, n_heads=A_HEADS,
                         tr=tk).reshape(A_HEADS, nv, tk, tbw)
    rel_m = jnp.arange(N_META, dtype=jnp.int32)[:, None] - (jnp.arange(s, dtype=jnp.int32)[None, :] + N_META)
    bias_m = _bias_tiles(rel_bias, _t5_bucket(rel_m), head0=0, n_heads=A_HEADS, tr=N_META)
    key_r = jnp.arange(_BKT, dtype=jnp.int32)[:, None]
    rel_b = (key_r - WINDOW) - jnp.arange(_BT, dtype=jnp.int32)[None, :]
    idx_b = jnp.where(jnp.abs(rel_b) <= WINDOW, _t5_bucket(rel_b), N_BUCKETS)
    bias_b = _bias_tiles(rel_bias, idx_b, head0=A_HEADS, n_heads=B_HEADS, tr=_BKT, heads_on_lanes=True)
    bias_b = jnp.stack([
        jnp.where(((key_r >= _PAD_FRONT) | (not first)) & ((key_r < _BT + WINDOW) | (not last)), bias_b, NEG)
        for last in (False, True) for first in (False, True)])

    oa = _attn_a(lam, qat, ka, vat, ka_m, vat_m, bias_a, bias_m, za, subln_g[0][None, :], tq=tq, tk=tk)

    def extend(tok, m):
        return jnp.concatenate([
            jnp.zeros((nb, _PAD_FRONT, B_KV_WIDTH), BF16),
            jnp.broadcast_to(m, (nb, N_META, B_KV_WIDTH)), tok,
            jnp.zeros((nb, WINDOW, B_KV_WIDTH), BF16)], axis=1)

    ob = _attn_b(sink[0], qbt, extend(kb, kb_m), extend(vb, vb_m), bias_b, zb)

    return _output(x, oa, ob, ga, gb, w_out_a[0].astype(BF16), w_out_b[0].astype(BF16),
                   w_out[0].astype(BF16), post_norm_g[0][None, :], tm=tm)
```

```python
import functools
import math
from typing import NamedTuple

import jax
import jax.numpy as jnp
from jax import lax
from jax.experimental import pallas as pl
from jax.experimental.pallas import tpu as pltpu

N_META = 16
WINDOW = 128
A_HEADS = 4
A_QK_DIM = 64
A_V_DIM = 2 * A_QK_DIM
A_WIDTH = A_HEADS * A_V_DIM
B_HEADS = 8
B_KV_HEADS = 2
B_HEAD_DIM = 64
B_WIDTH = B_HEADS * B_HEAD_DIM
B_KV_WIDTH = B_KV_HEADS * B_HEAD_DIM
N_BUCKETS = 32
MAX_DISTANCE = 128
EPS = 1e-6
NEG = -1e30
LOG2E = math.log2(math.e)
LAM_INIT = 0.8 - 0.6 * math.exp(-0.3 * 0)

LANES = 128
OVERFLOW_GUARD = 2.0 ** 100
F32_FINITE = 3.0e38
TRIP_KEYS = 2048
SUM_ROWS = 16
VMEM_LIMIT_BYTES = 56 * 1024 * 1024

_COLS = (A_WIDTH, A_WIDTH, A_WIDTH, A_WIDTH, B_WIDTH, B_KV_WIDTH, B_KV_WIDTH, B_WIDTH)
BF16 = jnp.bfloat16
F32 = jnp.float32


def _t5_bucket(rel):
    half = N_BUCKETS // 2
    max_exact = half // 2
    ret = jnp.where(rel > 0, half, 0)
    n = jnp.abs(rel)
    nf = jnp.maximum(n, 1).astype(F32)
    large = max_exact + (jnp.log(nf / max_exact) / math.log(MAX_DISTANCE / max_exact)
                         * (half - max_exact)).astype(jnp.int32)
    large = jnp.minimum(large, half - 1)
    return ret + jnp.where(n < max_exact, n, large)


def _proj_kernel(x_ref, g_ref, w_ref, b_ref,
                 qat_ref, ka_ref, vat_ref, za_ref, qbt_ref, kb_ref, vb_ref, zb_ref, ga_ref, gb_ref,
                 *, d_model, tq, tqb, tkv):
    x = x_ref[...]
    hn = (x * lax.rsqrt(jnp.mean(x * x, axis=-1, keepdims=True) + EPS) * g_ref[...]).astype(BF16)

    def seg(c0, n):
        return jnp.dot(hn, w_ref[:, c0:c0 + n], preferred_element_type=F32) + b_ref[:, c0:c0 + n]

    qscale = (A_QK_DIM ** -0.5) * LOG2E
    c = 0
    qa = seg(c, A_WIDTH) * qscale; c += A_WIDTH
    for t in range(qat_ref.shape[0]):
        qat_ref[t] = qa[t * tq:(t + 1) * tq, :].T.astype(BF16)
    ka_ref[...] = seg(c, A_WIDTH).astype(BF16); c += A_WIDTH
    va = seg(c, A_WIDTH); c += A_WIDTH
    for t in range(vat_ref.shape[0]):
        vat_ref[t] = va[t * tkv:(t + 1) * tkv, :].T.astype(BF16)
    za_ref[...] = seg(c, A_WIDTH).astype(BF16); c += A_WIDTH
    qb = seg(c, B_WIDTH) * qscale; c += B_WIDTH
    for t in range(qbt_ref.shape[0]):
        qbt_ref[t] = qb[t * tqb:(t + 1) * tqb, :].T.astype(BF16)
    kb_ref[...] = seg(c, B_KV_WIDTH).astype(BF16); c += B_KV_WIDTH
    vb_ref[...] = seg(c, B_KV_WIDTH).astype(BF16); c += B_KV_WIDTH
    zb_ref[...] = seg(c, B_WIDTH).astype(BF16); c += B_WIDTH
    ga_ref[...] = seg(c, d_model).astype(BF16); c += d_model
    gb_ref[...] = seg(c, d_model).astype(BF16)


def _project(x3, g, w, b, *, tm, tq, tqb, tkv):
    nb, s, d = x3.shape
    d_in = w.shape[1]
    widths = _COLS + (d, d)
    nt = s // tm
    row = lambda n: pl.BlockSpec((None, tm, n), lambda bi, i: (bi, i, 0))
    const = lambda shape: pl.BlockSpec(shape, lambda bi, i: (0, 0), pipeline_mode=pl.Buffered(1))
    out_specs, out_shapes = [], []
    for idx, n in enumerate(widths):
        if idx in (0, 2, 4):
            tt = {0: tq, 2: tkv, 4: tqb}[idx]
            out_specs.append(pl.BlockSpec((None, tm // tt, n, tt), lambda bi, i: (bi, i, 0, 0)))
            out_shapes.append(jax.ShapeDtypeStruct((nb, s // tt, n, tt), BF16))
        else:
            out_specs.append(row(n))
            out_shapes.append(jax.ShapeDtypeStruct((nb, s, n), BF16))
    return pl.pallas_call(
        functools.partial(_proj_kernel, d_model=d, tq=tq, tqb=tqb, tkv=tkv),
        grid=(nb, nt),
        in_specs=[row(d), const((1, d)), const((d, d_in)), const((1, d_in))],
        out_specs=out_specs,
        out_shape=out_shapes,
        compiler_params=pltpu.CompilerParams(
            dimension_semantics=("arbitrary", "arbitrary"), vmem_limit_bytes=VMEM_LIMIT_BYTES),
        name="proj",
    )(x3, g, w, b)


def _bias_kernel(tab_ref, idx_ref, o_ref, *, head0):
    h = pl.program_id(0) + head0
    idx = idx_ref[...]
    level = [tab_ref[bkt, h] * LOG2E for bkt in range(N_BUCKETS)]
    bit = 1
    while len(level) > 1:
        on = (idx & bit) != 0
        level = [jnp.where(on, level[k + 1], level[k]) for k in range(0, len(level), 2)]
        bit *= 2
    o_ref[...] = jnp.where(idx >= N_BUCKETS, NEG, level[0])


def _bias_tiles(tab, idx, *, head0, n_heads, tr, heads_on_lanes=False):
    r, c = idx.shape
    if heads_on_lanes:
        out_spec = pl.BlockSpec((tr, c), lambda h, i: (i, h))
        out_shape = jax.ShapeDtypeStruct((r, n_heads * c), F32)
    else:
        out_spec = pl.BlockSpec((None, tr, c), lambda h, i: (h, i, 0))
        out_shape = jax.ShapeDtypeStruct((n_heads, r, c), F32)
    return pl.pallas_call(
        functools.partial(_bias_kernel, head0=head0),
        grid=(n_heads, r // tr),
        in_specs=[pl.BlockSpec(memory_space=pltpu.SMEM),
                  pl.BlockSpec((tr, c), lambda h, i: (i, 0))],
        out_specs=out_spec,
        out_shape=out_shape,
        compiler_params=pltpu.CompilerParams(dimension_semantics=("arbitrary", "arbitrary")),
        name="bias_tiles",
    )(tab, idx)


def _attn_a_kernel(lam_ref, q_ref, k_ref, vt_ref, km_ref, vmt_ref, bias_ref, bm_ref, z_ref, g_ref,
                   o_ref, qt_ref, m_ref, acc_ref, p_ref, *, tq, tk, nk, nv, tb):
    i = pl.program_id(2)
    q = jnp.concatenate([q_ref[u] for u in range(q_ref.shape[0])], axis=1)
    row = lax.broadcasted_iota(jnp.int32, q.shape, 0)
    qt_ref[:, :tq] = jnp.where(row < A_QK_DIM, q, jnp.zeros_like(q))
    qt_ref[:, tq:] = jnp.where(row >= A_QK_DIM, q, jnp.zeros_like(q))
    qt = qt_ref[...]

    def both(fn):
        return jnp.concatenate([fn(slice(0, tq)), fn(slice(tq, 2 * tq))], axis=1)

    def pv_and_sum(vt, pb):
        vt1 = jnp.concatenate([vt, jnp.ones((SUM_ROWS, vt.shape[1]), BF16)], axis=0)
        return both(lambda c: jnp.dot(vt1, pb[:, c], preferred_element_type=F32))

    def raw_scores(j):
        ks = pl.multiple_of(j * tk, tk)
        return jnp.dot(k_ref[pl.ds(ks, tk), :], qt, preferred_element_type=F32)

    def bias2(j):
        tbw = bias_ref.shape[-1]
        pieces = [bias_ref[jnp.clip(j - (i * (tq // tbw) + u) * (tbw // tk) + 2, 0, nv - 1)]
                  for u in range(tq // tbw)]
        return jnp.concatenate(pieces + pieces, axis=1)

    def tile_scores(j):
        return raw_scores(j) + bias2(j)

    far_meta = jnp.concatenate([bias_ref[0, 0:N_META, :]] * (tq // bias_ref.shape[-1]), axis=1)
    bm = jnp.where(i == 0, bm_ref[...], far_meta)
    s_meta = jnp.dot(km_ref[...], qt, preferred_element_type=F32) + jnp.concatenate([bm, bm], axis=1)
    m0 = jnp.max(s_meta, axis=0, keepdims=True)

    def init_from_meta():
        acc_ref[...] = pv_and_sum(vmt_ref[...], jnp.exp2(s_meta - m0).astype(BF16))

    start = i * (tq // tk) - 1
    n_near = min(tq // tk + 2, nk)
    far_lo = jnp.concatenate([bias_ref[0, 0:1, :]] * (2 * tq // bias_ref.shape[-1]), axis=1)
    far_hi = jnp.concatenate([bias_ref[nv - 1, 0:1, :]] * (2 * tq // bias_ref.shape[-1]), axis=1)
    shift_lo, shift_hi = m0 - far_lo, m0 - far_hi

    def tile_at(pos):
        j = start + pos
        wrapped = j >= nk
        j = jnp.where(wrapped, j - nk, j)
        return jnp.where(j < 0, j + nk, j), wrapped

    def stage_p(pos, slot, near):
        j, wrapped = tile_at(pos)
        if near:
            p = jnp.exp2(tile_scores(j) - m0)
        else:
            p = jnp.exp2(raw_scores(j) - jnp.where(wrapped, shift_lo, shift_hi))
        p_ref[slot] = p.astype(BF16)

    def group(g, first, last):
        pv = None
        for u in range(tb):
            pos = g * tb + u
            if not (last and u == tb - 1):
                stage_p(pos + 1, (u + 1) % 2, first and u + 1 < n_near)
            d = pv_and_sum(vt_ref[tile_at(pos)[0]], p_ref[u % 2])
            pv = d if pv is None else pv + d
        acc_ref[...] += pv

    def trip(g, carry):
        group(g, False, False)
        return carry

    stage_p(0, 0, True)
    init_from_meta()

    n_trips = nk // tb
    group(0, True, n_trips == 1)
    lax.fori_loop(1, n_trips - 1, trip, 0)
    if n_trips > 1:
        group(n_trips - 1, False, True)

    def finish():
        acc = acc_ref[...]
        l = acc[A_V_DIM:A_V_DIM + 1, :]
        inv = 1.0 / l
        ot = acc[:A_V_DIM, :tq] * inv[:, :tq] - lam_ref[0] * (acc[:A_V_DIM, tq:] * inv[:, tq:])
        ssq = jnp.sum(ot * ot, axis=0, keepdims=True)
        yt = ot * lax.rsqrt(ssq * (1.0 / A_V_DIM) + EPS)
        z = z_ref[...].astype(F32)
        o_ref[...] = (yt.T * g_ref[...] * (1.0 - LAM_INIT) * (z * jax.nn.sigmoid(z))).astype(BF16)
        return l, ssq

    l, ssq = finish()
    overflow = (jnp.max(jnp.where(l < OVERFLOW_GUARD, 0.0, 1.0))
                + jnp.max(jnp.where(ssq < F32_FINITE, 0.0, 1.0))) > 0.0

    @pl.when(overflow)
    def _():
        init_from_meta()
        m_ref[...] = m0

        def step(j, carry):
            s = tile_scores(j)
            m_old = m_ref[...]
            m_new = jnp.maximum(m_old, jnp.max(s, axis=0, keepdims=True))
            alpha = jnp.exp2(m_old - m_new)
            acc_ref[...] = alpha * acc_ref[...] + pv_and_sum(vt_ref[j], jnp.exp2(s - m_new).astype(BF16))
            m_ref[...] = m_new
            return carry

        lax.fori_loop(0, nk, step, 0)
        finish()


def _attn_a(lam, qat, ka, vat, ka_m, vat_m, bias, bias_m, za, subln_g, *, tq, tk):
    nb, s, _ = ka.shape
    tqs = qat.shape[-1]
    nk = s // tk
    nv, tbw = bias.shape[1], bias.shape[3]
    tb = next(c for c in (TRIP_KEYS // tk, 4, 2, 1) if c <= nk and nk % c == 0)
    assert nk % tb == 0 and (nk == tb or tq // tk + 2 <= tb)
    head = lambda n: pl.BlockSpec((None, n, A_V_DIM), lambda b, h, i: (b, i, h))
    return pl.pallas_call(
        functools.partial(_attn_a_kernel, tq=tq, tk=tk, nk=nk, nv=nv, tb=tb),
        grid=(nb, A_HEADS, s // tq),
        in_specs=[
            pl.BlockSpec(memory_space=pltpu.SMEM),
            pl.BlockSpec((None, tq // tqs, A_V_DIM, tqs), lambda b, h, i: (b, i, h, 0)),
            pl.BlockSpec((None, s, A_V_DIM), lambda b, h, i: (b, 0, h)),
            pl.BlockSpec((None, nk, A_V_DIM, tk), lambda b, h, i: (b, 0, h, 0)),
            pl.BlockSpec((None, N_META, A_V_DIM), lambda b, h, i: (0, 0, h)),
            pl.BlockSpec((None, None, A_V_DIM, N_META), lambda b, h, i: (0, 0, h, 0)),
            pl.BlockSpec((None, nv, tk, tbw), lambda b, h, i: (h, 0, 0, 0)),
            pl.BlockSpec((None, N_META, tq), lambda b, h, i: (h, 0, 0)),
            head(tq),
            pl.BlockSpec((1, A_V_DIM), lambda b, h, i: (0, 0)),
        ],
        out_specs=head(tq),
        out_shape=jax.ShapeDtypeStruct((nb, s, A_WIDTH), BF16),
        scratch_shapes=[
            pltpu.VMEM((A_V_DIM, 2 * tq), BF16),
            pltpu.VMEM((1, 2 * tq), F32),
            pltpu.VMEM((A_V_DIM + SUM_ROWS, 2 * tq), F32),
            pltpu.VMEM((2, tk, 2 * tq), BF16),
        ],
        compiler_params=pltpu.CompilerParams(
            dimension_semantics=("arbitrary", "arbitrary", "arbitrary"),
            vmem_limit_bytes=VMEM_LIMIT_BYTES),
        name="attn_a",
    )(lam, qat, ka, vat, ka_m, vat_m, bias, bias_m, za, subln_g)


_BT = WINDOW
_BSUB = 8
_BKT = _BT + 2 * WINDOW
_PAD_FRONT = WINDOW - N_META
_B_SLABS = B_HEADS // 2
_B_GROUP = B_HEADS // B_KV_HEADS


def _attn_b_kernel(sink_ref, q_ref, *rest, n_sub_total):
    nkb = _BSUB + 2
    k_refs, v_refs = rest[:nkb], rest[nkb:2 * nkb]
    bias_ref, z_ref, o_ref, qt_ref = rest[2 * nkb:]
    i = pl.program_id(1)
    sink_row = jnp.concatenate(
        [jnp.full((1, _BT), sink_ref[h] * LOG2E, F32) for h in range(B_HEADS)], axis=1)
    zeros = jnp.zeros((B_HEAD_DIM, _BT), BF16)

    def scores(t):
        it = i * _BSUB + t
        k = jnp.concatenate([r[...] for r in k_refs[t:t + 3]], axis=0)
        q = q_ref[t]
        for h in range(B_HEADS):
            blk = q[h * B_HEAD_DIM:(h + 1) * B_HEAD_DIM, :]
            pair = [blk, zeros] if h < _B_GROUP else [zeros, blk]
            qt_ref[t, :, h * _BT:(h + 1) * _BT] = jnp.concatenate(pair, axis=0)
        variant = (it == 0).astype(jnp.int32) + 2 * (it == n_sub_total - 1).astype(jnp.int32)
        return jnp.dot(k, qt_ref[t], preferred_element_type=F32) + bias_ref[variant]

    def finish(t, s, exact):
        v = jnp.concatenate([r[...] for r in v_refs[t:t + 3]], axis=0)
        vt = jnp.concatenate([v.astype(F32).T.astype(BF16), jnp.ones((SUM_ROWS, _BKT), BF16)], axis=0)
        m = jnp.maximum(jnp.max(s, axis=0, keepdims=True), sink_row) if exact else sink_row
        pv = jnp.dot(vt, jnp.exp2(s - m).astype(BF16), preferred_element_type=F32)
        l = pv[2 * B_HEAD_DIM:2 * B_HEAD_DIM + 1, :] + jnp.exp2(sink_row - m)
        ot = pv[:2 * B_HEAD_DIM, :] * (1.0 / l)
        for slab in range(_B_SLABS):
            g = (2 * slab) // _B_GROUP
            rows = slice(g * B_HEAD_DIM, (g + 1) * B_HEAD_DIM)
            x = jnp.concatenate([ot[rows, (2 * slab) * _BT:(2 * slab + 1) * _BT],
                                 ot[rows, (2 * slab + 1) * _BT:(2 * slab + 2) * _BT]], axis=0)
            cols = slice(slab * LANES, (slab + 1) * LANES)
            z = z_ref[t * _BT:(t + 1) * _BT, cols].astype(F32)
            o_ref[t * _BT:(t + 1) * _BT, cols] = (x.T * (z * jax.nn.sigmoid(z))).astype(BF16)
        if exact:
            return None
        return jnp.where(jnp.max(jnp.abs(pv), axis=0, keepdims=True) < OVERFLOW_GUARD, 0.0, 1.0)

    def sweep(exact):
        bad = None
        s_cur = scores(0)
        for t in range(_BSUB):
            s_nxt = scores(t + 1) if t + 1 < _BSUB else None
            b = finish(t, s_cur, exact)
            if not exact:
                bad = b if bad is None else jnp.maximum(bad, b)
            s_cur = s_nxt
        return bad

    overflow = jnp.max(sweep(False)) > 0.0

    @pl.when(overflow)
    def _():
        sweep(True)


def _attn_b(sink, qbt, kb_ext, vb_ext, bias, zb):
    nb, s, _ = zb.shape
    tq = _BT * _BSUB
    row = pl.BlockSpec((None, tq, B_WIDTH), lambda b, i: (b, i, 0))
    nkb = _BSUB + 2
    kv = [pl.BlockSpec((None, WINDOW, B_KV_WIDTH), (lambda b, i, t=t: (b, _BSUB * i + t, 0)))
          for t in range(nkb)]
    return pl.pallas_call(
        functools.partial(_attn_b_kernel, n_sub_total=s // _BT),
        grid=(nb, s // tq),
        in_specs=[pl.BlockSpec(memory_space=pltpu.SMEM),
                  pl.BlockSpec((None, _BSUB, B_WIDTH, _BT), lambda b, i: (b, i, 0, 0))] + kv + kv
                 + [pl.BlockSpec((4, _BKT, B_HEADS * _BT), lambda b, i: (0, 0, 0),
                                 pipeline_mode=pl.Buffered(1)), row],
        out_specs=row,
        out_shape=jax.ShapeDtypeStruct((nb, s, B_WIDTH), BF16),
        scratch_shapes=[pltpu.VMEM((_BSUB, 2 * B_HEAD_DIM, B_HEADS * _BT), BF16)],
        compiler_params=pltpu.CompilerParams(
            dimension_semantics=("arbitrary", "arbitrary"), vmem_limit_bytes=VMEM_LIMIT_BYTES),
        name="attn_b",
    )(sink, qbt, *([kb_ext] * nkb), *([vb_ext] * nkb), bias, zb)


def _out_kernel(x_ref, a_ref, b_ref, ga_ref, gb_ref, woa_ref, wob_ref, wo_ref, g_ref, o_ref):
    ya = jnp.dot(a_ref[...], woa_ref[...], preferred_element_type=F32)
    yb = jnp.dot(b_ref[...], wob_ref[...], preferred_element_type=F32)
    mixed = (jax.nn.sigmoid(ga_ref[...].astype(F32)) * ya
             + jax.nn.sigmoid(gb_ref[...].astype(F32)) * yb)
    y = jnp.dot(mixed.astype(BF16), wo_ref[...], preferred_element_type=F32)
    y = y * lax.rsqrt(jnp.mean(y * y, axis=-1, keepdims=True) + EPS) * g_ref[...]
    o_ref[...] = x_ref[...] + y


def _output(x3, a, b, ga, gb, woa, wob, wo, g, *, tm):
    nb, s, d = x3.shape
    row = lambda n: pl.BlockSpec((None, tm, n), lambda bi, i: (bi, i, 0))
    const = lambda shape: pl.BlockSpec(shape, lambda bi, i: (0, 0), pipeline_mode=pl.Buffered(1))
    return pl.pallas_call(
        _out_kernel,
        grid=(nb, s // tm),
        in_specs=[row(d), row(A_WIDTH), row(B_WIDTH), row(d), row(d),
                  const((A_WIDTH, d)), const((B_WIDTH, d)), const((d, d)), const((1, d))],
        out_specs=row(d),
        out_shape=jax.ShapeDtypeStruct((nb, s, d), F32),
        compiler_params=pltpu.CompilerParams(
            dimension_semantics=("arbitrary", "arbitrary"), vmem_limit_bytes=VMEM_LIMIT_BYTES),
        name="out_proj",
    )(x3, a, b, ga, gb, woa, wob, wo, g)


class _Tiles(NamedTuple):
    tm: int
    tq: int
    tk: int
    tqs: int


def _tiles(s):
    t = _Tiles(tm=min(1024, s), tq=min(1024, s), tk=min(512, s), tqs=min(1024, s))
    assert s % t.tm == 0 and s % t.tq == 0 and t.tq % t.tk == 0 and t.tm % t.tqs == 0 and t.tq % t.tqs == 0
    assert t.tm % _BT == 0 and s % (_BT * _BSUB) == 0
    return t


def kernel(x, meta_tokens, rel_bias, pre_norm_g, w_in, b_in, lambda_q1, lambda_k1, lambda_q2, lambda_k2,
           subln_g, sink, w_out_a, w_out_b, w_out, post_norm_g):
    nb, s, d = x.shape
    tm, tq, tk, tqs = _tiles(s)

    w = w_in[0].astype(BF16)
    b = b_in[0][None, :]
    g_pre = pre_norm_g[0][None, :]
    lam = (jnp.exp(jnp.sum(lambda_q1[0].astype(F32) * lambda_k1[0].astype(F32)))
           - jnp.exp(jnp.sum(lambda_q2[0].astype(F32) * lambda_k2[0].astype(F32))) + LAM_INIT).reshape(1)

    qat, ka, vat, za, qbt, kb, vb, zb, ga, gb = _project(x, g_pre, w, b, tm=tm, tq=tqs, tqb=_BT, tkv=tk)
    meta = _project(meta_tokens[None], g_pre, w, b, tm=N_META, tq=N_META, tqb=N_META, tkv=N_META)
    ka_m, vat_m, kb_m, vb_m = meta[1], meta[2], meta[5], meta[6]

    tbw = tk
    nv = tbw // tk + 4
    off = (jnp.arange(nv, dtype=jnp.int32) - 2)[:, None, None] * tk
    rel_a = off + jnp.arange(tk, dtype=jnp.int32)[None, :, None] - jnp.arange(tbw, dtype=jnp.int32)[None, None, :]
    bias_a = _bias_tiles(rel_bias, _t5_bucket(rel_a).reshape(nv * tk, tbw), head0=0, n_heads=A_HEADS,
                         tr=tk).reshape(A_HEADS, nv, tk, tbw)
    assert tq >= MAX_DISTANCE
    rel_m = jnp.arange(N_META, dtype=jnp.int32)[:, None] - (jnp.arange(tq, dtype=jnp.int32)[None, :] + N_META)
    bias_m = _bias_tiles(rel_bias, _t5_bucket(rel_m), head0=0, n_heads=A_HEADS, tr=N_META)
    key_r = jnp.arange(_BKT, dtype=jnp.int32)[:, None]
    rel_b = (key_r - WINDOW) - jnp.arange(_BT, dtype=jnp.int32)[None, :]
    idx_b = jnp.where(jnp.abs(rel_b) <= WINDOW, _t5_bucket(rel_b), N_BUCKETS)
    bias_b = _bias_tiles(rel_bias, idx_b, head0=A_HEADS, n_heads=B_HEADS, tr=_BKT, heads_on_lanes=True)
    bias_b = jnp.stack([
        jnp.where(((key_r >= _PAD_FRONT) | (not first)) & ((key_r < _BT + WINDOW) | (not last)), bias_b, NEG)
        for last in (False, True) for first in (False, True)])

    oa = _attn_a(lam, qat, ka, vat, ka_m, vat_m, bias_a, bias_m, za, subln_g[0][None, :], tq=tq, tk=tk)

    def extend(tok, m):
        return jnp.concatenate([
            jnp.zeros((nb, _PAD_FRONT, B_KV_WIDTH), BF16),
            jnp.broadcast_to(m, (nb, N_META, B_KV_WIDTH)), tok,
            jnp.zeros((nb, WINDOW, B_KV_WIDTH), BF16)], axis=1)

    ob = _attn_b(sink[0], qbt, extend(kb, kb_m), extend(vb, vb_m), bias_b, zb)

    return _output(x, oa, ob, ga, gb, w_out_a[0].astype(BF16), w_out_b[0].astype(BF16),
                   w_out[0].astype(BF16), post_norm_g[0][None, :], tm=tm)
```
